```python
import math
import jax, jax.numpy as jnp
from jax import lax
import numpy as np

D_MODEL = 1024
BATCH = 4
SEQ = 4096
DEPTH = 1

HEAD_DIM = 64
D_MIX = D_MODEL
N_GMLP_HEADS = 8
D_GMLP = N_GMLP_HEADS * HEAD_DIM
N_Q_HEADS = 8
N_KV_HEADS = 2
GQA_GROUP = N_Q_HEADS // N_KV_HEADS
D_ATTN = N_Q_HEADS * HEAD_DIM
D_KV = N_KV_HEADS * HEAD_DIM
D_IN = 2 * D_GMLP + D_ATTN + 2 * D_KV
CHUNK = 128
WINDOW = 128
ATTN_BLOCK = 128
ROPE_THETA = 10000.0
D_FF = 4 * D_MODEL
LN_EPS = 1e-5
DEEPNORM_ALPHA = (2.0 * DEPTH) ** 0.25
DEEPNORM_BETA = (8.0 * DEPTH) ** -0.25
NEG_INF = -1e30

kernel_name = "hymba_gmlp_swa_sink_deepnorm"


def layer_norm(x, g, b):
    xf = x.astype(jnp.float32)
    mu = jnp.mean(xf, axis=-1, keepdims=True)
    var = jnp.mean(jnp.square(xf - mu), axis=-1, keepdims=True)
    y = (xf - mu) * lax.rsqrt(var + LN_EPS)
    return (y * g.astype(jnp.float32) + b.astype(jnp.float32)).astype(x.dtype)


def rope(t, positions):
    half = HEAD_DIM // 2
    inv_freq = ROPE_THETA ** (-jnp.arange(0, HEAD_DIM, 2, dtype=jnp.float32) / HEAD_DIM)
    ang = positions.astype(jnp.float32)[..., None] * inv_freq
    cos = jnp.cos(ang)[:, :, None, :]
    sin = jnp.sin(ang)[:, :, None, :]
    tf = t.astype(jnp.float32)
    t1, t2 = tf[..., :half], tf[..., half:]
    out = jnp.concatenate([t1 * cos - t2 * sin, t2 * cos + t1 * sin], axis=-1)
    return out.astype(t.dtype)


def gmlp_mixer(u, v, v_ln_g, v_ln_b, w_spatial, b_spatial):
    B, S, _ = u.shape
    nc = S // CHUNK
    u = jax.nn.gelu(u)
    v = layer_norm(jax.nn.gelu(v), v_ln_g, v_ln_b)
    vc = v.reshape(B, nc, CHUNK, N_GMLP_HEADS, HEAD_DIM)
    causal = jnp.tril(jnp.ones((CHUNK, CHUNK), dtype=w_spatial.dtype))
    w = w_spatial * causal
    mixed = jnp.einsum('hts,bcshd->bcthd', w, vc) + b_spatial.T[None, None, :, :, None]
    out = u.reshape(B, nc, CHUNK, N_GMLP_HEADS, HEAD_DIM) * mixed
    return out.reshape(B, S, D_GMLP)


def swa_sink_attention(q, k, v, positions, sinks):
    B, S, _ = q.shape
    nb = S // ATTN_BLOCK
    q = rope(q.reshape(B, S, N_Q_HEADS, HEAD_DIM), positions)
    k = rope(k.reshape(B, S, N_KV_HEADS, HEAD_DIM), positions)
    v = v.reshape(B, S, N_KV_HEADS, HEAD_DIM)
    qb = q.reshape(B, nb, ATTN_BLOCK, N_KV_HEADS, GQA_GROUP, HEAD_DIM)

    def banded(t):
        tb = t.reshape(B, nb, ATTN_BLOCK, N_KV_HEADS, HEAD_DIM)
        prev = jnp.pad(tb[:, :-1], ((0, 0), (1, 0), (0, 0), (0, 0), (0, 0)))
        return jnp.concatenate([prev, tb], axis=2)

    kb, vb = banded(k), banded(v)
    scores = jnp.einsum('bnqkgd,bnskd->bnkgqs', qb, kb).astype(jnp.float32)
    scores = scores * (1.0 / math.sqrt(HEAD_DIM))

    qi = jnp.arange(ATTN_BLOCK)[:, None]
    si = jnp.arange(2 * ATTN_BLOCK)[None, :]
    dist = qi + ATTN_BLOCK - si
    band = (dist >= 0) & (dist < WINDOW)
    key_abs = jnp.arange(nb)[:, None, None] * ATTN_BLOCK + si[None] - ATTN_BLOCK
    mask = band[None] & (key_abs >= 0)
    scores = jnp.where(mask[None, :, None, None], scores, NEG_INF)

    sink = sinks.astype(jnp.float32).reshape(N_KV_HEADS, GQA_GROUP)
    sink_col = jnp.broadcast_to(sink[None, None, :, :, None, None], scores.shape[:-1] + (1,))
    probs = jax.nn.softmax(jnp.concatenate([scores, sink_col], axis=-1), axis=-1)[..., :-1]
    out = jnp.einsum('bnkgqs,bnskd->bnqkgd', probs.astype(vb.dtype), vb)
    return out.reshape(B, S, D_ATTN)


def setup_inputs(seed: int = 0) -> dict:
    key = jax.random.key(seed)
    ks = jax.random.split(key, 16)
    f32 = jnp.float32
    x = jax.random.normal(ks[0], (BATCH, SEQ, D_MODEL), f32)
    offset = jax.random.randint(ks[1], (BATCH, 1), 0, 1024, dtype=jnp.int32)
    positions = (offset + jnp.arange(SEQ, dtype=jnp.int32)[None, :]).astype(jnp.int32)
    w_in = jax.random.normal(ks[2], (DEPTH, D_MODEL, D_IN), f32) * D_MODEL ** -0.5
    v_ln_g = 1.0 + 0.05 * jax.random.normal(ks[3], (DEPTH, D_GMLP), f32)
    v_ln_b = 0.02 * jax.random.normal(ks[4], (DEPTH, D_GMLP), f32)
    w_spatial = jax.random.normal(ks[5], (DEPTH, N_GMLP_HEADS, CHUNK, CHUNK), f32) * CHUNK ** -0.5
    b_spatial = 1.0 + 0.1 * jax.random.normal(ks[6], (DEPTH, N_GMLP_HEADS, CHUNK), f32)
    sinks = 0.5 * jax.random.normal(ks[7], (DEPTH, N_Q_HEADS), f32)
    w_out = jax.random.normal(ks[8], (DEPTH, D_MIX, D_MODEL), f32) * (D_MIX ** -0.5) * DEEPNORM_BETA
    ln1_g = 1.0 + 0.05 * jax.random.normal(ks[9], (DEPTH, D_MODEL), f32)
    ln1_b = 0.02 * jax.random.normal(ks[10], (DEPTH, D_MODEL), f32)
    w_ff1 = jax.random.normal(ks[11], (DEPTH, D_MODEL, D_FF), f32) * D_MODEL ** -0.5
    w_ff2 = jax.random.normal(ks[12], (DEPTH, D_FF, D_MODEL), f32) * (D_FF ** -0.5) * DEEPNORM_BETA
    ln2_g = 1.0 + 0.05 * jax.random.normal(ks[13], (DEPTH, D_MODEL), f32)
    ln2_b = 0.02 * jax.random.normal(ks[14], (DEPTH, D_MODEL), f32)
    return {"x": x, "positions": positions, "w_in": w_in, "v_ln_g": v_ln_g, "v_ln_b": v_ln_b,
            "w_spatial": w_spatial, "b_spatial": b_spatial, "sinks": sinks, "w_out": w_out,
            "ln1_g": ln1_g, "ln1_b": ln1_b, "w_ff1": w_ff1, "w_ff2": w_ff2,
            "ln2_g": ln2_g, "ln2_b": ln2_b}


def reference(x, positions, w_in, v_ln_g, v_ln_b, w_spatial, b_spatial, sinks, w_out,
              ln1_g, ln1_b, w_ff1, w_ff2, ln2_g, ln2_b):
    split_at = [D_GMLP, 2 * D_GMLP, 2 * D_GMLP + D_ATTN, 2 * D_GMLP + D_ATTN + D_KV]
    for l in range(DEPTH):
        h = x @ w_in[l]
        u, v_g, q, k, v_a = jnp.split(h, split_at, axis=-1)
        a_out = gmlp_mixer(u, v_g, v_ln_g[l], v_ln_b[l], w_spatial[l], b_spatial[l])
        b_out = swa_sink_attention(q, k, v_a, positions, sinks[l])
        mix = jnp.concatenate([a_out, b_out], axis=-1) @ w_out[l]
        x = layer_norm(DEEPNORM_ALPHA * x + mix, ln1_g[l], ln1_b[l])
        ff = jnp.square(jax.nn.relu(x @ w_ff1[l])) @ w_ff2[l]
        x = layer_norm(DEEPNORM_ALPHA * x + ff, ln2_g[l], ln2_b[l])
    return x
```

```python
import functools

import jax
import jax.numpy as jnp
from jax import lax
from jax.experimental import pallas as pl
from jax.experimental.pallas import tpu as pltpu

HEAD_DIM = 64
N_GMLP_HEADS = 8
D_GMLP = N_GMLP_HEADS * HEAD_DIM
N_Q_HEADS = 8
N_KV_HEADS = 2
GQA_GROUP = N_Q_HEADS // N_KV_HEADS
D_ATTN = N_Q_HEADS * HEAD_DIM
D_KV = N_KV_HEADS * HEAD_DIM
BLK = 128
ROPE_THETA = 10000.0
LN_EPS = 1e-5
NEG_INF = -1e30
LANES = 128
HALF = HEAD_DIM // 2

MIXER_TOKENS = 256
FFN_TOKENS = 512
FFN_CHUNK = 1024
VMEM_LIMIT_BYTES = 48 * 1024 * 1024

_BF16 = jnp.bfloat16
_F32 = jnp.float32


def _dot(a, b):
    return jnp.dot(a, b, preferred_element_type=_F32)


def _dot_nt(a, b):
    return lax.dot_general(a, b, (((1,), (1,)), ((), ())), preferred_element_type=_F32)


def _layer_norm(v, g, b):
    mu = jnp.mean(v, axis=-1, keepdims=True)
    c = v - mu
    var = jnp.mean(c * c, axis=-1, keepdims=True)
    return c * lax.rsqrt(var + LN_EPS) * g + b


def _mixer_kernel(sinks_ref, x_ref, pos_ref, invf_ref, w_in_ref, vg_ref, vb_ref, wsp_ref, bsp_ref,
                  w_out_ref, g1_ref, b1_ref, o_ref, kd_buf, vw_buf, mix_buf, *, tokens, alpha):
    n_blk = tokens // BLK
    n_slab_g = D_GMLP // LANES
    n_slab_a = D_ATTN // LANES

    @pl.when(pl.program_id(1) == 0)
    def _():
        kd_buf[:, 0:BLK, :] = jnp.zeros((N_KV_HEADS, BLK, LANES), _BF16)
        vw_buf[:, 0:BLK, :] = jnp.zeros((2 * N_KV_HEADS, BLK, LANES), _BF16)

    x = x_ref[...]
    xb = x.astype(_BF16)

    lane = lax.broadcasted_iota(jnp.int32, (1, LANES), 1)
    lo = lane < HEAD_DIM

    u = jax.nn.gelu(_dot(xb, w_in_ref[:, 0:D_GMLP]))
    vgel = jax.nn.gelu(_dot(xb, w_in_ref[:, D_GMLP:2 * D_GMLP]))
    vn = _layer_norm(vgel, vg_ref[...], vb_ref[...])
    lo_g = (lax.broadcasted_iota(jnp.int32, (1, D_GMLP), 1) & HEAD_DIM) == 0
    vn_top = jnp.where(lo_g, vn, 0.0).astype(_BF16)
    vn_bot = jnp.where(lo_g, 0.0, vn).astype(_BF16)

    ti = lax.broadcasted_iota(jnp.int32, (BLK, BLK), 0)
    si = lax.broadcasted_iota(jnp.int32, (BLK, BLK), 1)
    causal = si <= ti
    for p in range(n_slab_g):
        w_pair = jnp.concatenate(
            [jnp.where(causal, wsp_ref[2 * p], 0.0), jnp.where(causal, wsp_ref[2 * p + 1], 0.0)],
            axis=1).astype(_BF16)
        bias = bsp_ref[:, p * LANES:(p + 1) * LANES]
        for c in range(n_blk):
            rows = slice(c * BLK, (c + 1) * BLK)
            cols = slice(p * LANES, (p + 1) * LANES)
            rhs = jnp.concatenate([vn_top[rows, cols], vn_bot[rows, cols]], axis=0)
            mixed = _dot(w_pair, rhs) + bias
            mix_buf[rows, cols] = (u[rows, cols] * mixed).astype(_BF16)

    pos = pos_ref[...].astype(_F32)
    ang = pos * invf_ref[...]
    cos = jnp.cos(ang)
    first_half = (lane & HALF) == 0
    sin_signed = jnp.where(first_half, -1.0, 1.0) * jnp.sin(ang)

    def rope(t):
        swapped = jnp.where(first_half, pltpu.roll(t, LANES - HALF, axis=1),
                            pltpu.roll(t, HALF, axis=1))
        return t * cos + swapped * sin_signed

    q0 = 2 * D_GMLP
    scale = HEAD_DIM ** -0.5
    q_even, q_odd = [], []
    for p in range(n_slab_a):
        qs = rope(_dot(xb, w_in_ref[:, q0 + p * LANES:q0 + (p + 1) * LANES])) * scale
        q_even.append(jnp.where(lo, qs, 0.0).astype(_BF16))
        q_odd.append(jnp.where(lo, 0.0, qs).astype(_BF16))

    k0 = q0 + D_ATTN
    kr = rope(_dot(xb, w_in_ref[:, k0:k0 + D_KV]))
    kr_sw = pltpu.roll(kr, HEAD_DIM, axis=1)
    cur = slice(BLK, BLK + tokens)
    kd_buf[0, cur, :] = jnp.where(lo, kr, kr_sw).astype(_BF16)
    kd_buf[1, cur, :] = jnp.where(lo, kr_sw, kr).astype(_BF16)
    v = _dot(xb, w_in_ref[:, k0 + D_KV:k0 + 2 * D_KV])
    v_sw = pltpu.roll(v, HEAD_DIM, axis=1)
    vw_buf[0, cur, :] = jnp.where(lo, v, 0.0).astype(_BF16)
    vw_buf[1, cur, :] = jnp.where(lo, 0.0, v_sw).astype(_BF16)
    vw_buf[2, cur, :] = jnp.where(lo, v_sw, 0.0).astype(_BF16)
    vw_buf[3, cur, :] = jnp.where(lo, 0.0, v).astype(_BF16)

    qi = lax.broadcasted_iota(jnp.int32, (BLK, 2 * BLK), 0)
    ki = lax.broadcasted_iota(jnp.int32, (BLK, 2 * BLK), 1)
    dist = qi + BLK - ki
    band = (dist >= 0) & (dist < BLK)
    first_key = jnp.where(pl.program_id(1) == 0, BLK, 0)
    band_first = band & (ki >= first_key)
    for n in range(n_blk):
        mask = band_first if n == 0 else band
        rows = slice(n * BLK, (n + 1) * BLK)
        kv_rows = slice(n * BLK, (n + 2) * BLK)
        for g in range(N_KV_HEADS):
            slabs = (2 * g, 2 * g + 1)
            q_stack = jnp.concatenate(
                [q_even[slabs[0]][rows], q_odd[slabs[0]][rows],
                 q_even[slabs[1]][rows], q_odd[slabs[1]][rows]], axis=0)
            scores = _dot_nt(q_stack, kd_buf[g, kv_rows, :])
            for j, slab in enumerate(slabs):
                acc = None
                inv_l = []
                for e in range(2):
                    head = 2 * slab + e
                    hl = 2 * j + e
                    sh = jnp.where(mask, scores[hl * BLK:(hl + 1) * BLK], NEG_INF)
                    sink = sinks_ref[head]
                    m = jnp.maximum(jnp.max(sh, axis=-1, keepdims=True), sink)
                    pr = jnp.exp(sh - m)
                    l = jnp.sum(pr, axis=-1, keepdims=True) + jnp.exp(sink - m)
                    inv_l.append(1.0 / l)
                    part = _dot(pr.astype(_BF16), vw_buf[2 * g + e, kv_rows, :])
                    acc = part if acc is None else acc + part
                out = acc * jnp.where(lo, inv_l[0], inv_l[1])
                mix_buf[rows, D_GMLP + slab * LANES:D_GMLP + (slab + 1) * LANES] = out.astype(_BF16)

    last = slice(tokens, tokens + BLK)
    kd_buf[:, 0:BLK, :] = kd_buf[:, last, :]
    vw_buf[:, 0:BLK, :] = vw_buf[:, last, :]

    y = alpha * x + _dot(mix_buf[...], w_out_ref[...])
    o_ref[...] = _layer_norm(y, g1_ref[...], b1_ref[...])


def _ffn_kernel(x_ref, w1_ref, w2_ref, g_ref, b_ref, o_ref, *, alpha, d_ff):
    x = x_ref[...]
    xb = x.astype(_BF16)
    acc = alpha * x
    for j in range(d_ff // FFN_CHUNK):
        cols = slice(j * FFN_CHUNK, (j + 1) * FFN_CHUNK)
        h = jnp.maximum(_dot(xb, w1_ref[:, cols]), 0.0)
        acc = acc + _dot((h * h).astype(_BF16), w2_ref[cols, :])
    o_ref[...] = _layer_norm(acc, g_ref[...], b_ref[...])


def _resident(shape):
    return pl.BlockSpec(shape, lambda *_: (0,) * len(shape), pipeline_mode=pl.Buffered(1))


def _mixer(x, pos, invf, sinks, w_in, vg, vb, wsp, bsp, w_out, g1, b1, *, alpha):
    batch, seq, d_model = x.shape
    tokens = MIXER_TOKENS
    assert seq % tokens == 0 and tokens % BLK == 0
    kernel = functools.partial(_mixer_kernel, tokens=tokens, alpha=alpha)
    tile = pl.BlockSpec((None, tokens, d_model), lambda b, s: (b, s, 0))
    return pl.pallas_call(
        kernel,
        grid=(batch, seq // tokens),
        in_specs=[
            pl.BlockSpec(memory_space=pltpu.SMEM),
            tile,
            pl.BlockSpec((None, tokens, 1), lambda b, s: (b, s, 0)),
            _resident(invf.shape), _resident(w_in.shape), _resident(vg.shape), _resident(vb.shape),
            _resident(wsp.shape), _resident(bsp.shape), _resident(w_out.shape),
            _resident(g1.shape), _resident(b1.shape),
        ],
        out_specs=tile,
        out_shape=jax.ShapeDtypeStruct(x.shape, x.dtype),
        scratch_shapes=[
            pltpu.VMEM((N_KV_HEADS, tokens + BLK, LANES), _BF16),
            pltpu.VMEM((2 * N_KV_HEADS, tokens + BLK, LANES), _BF16),
            pltpu.VMEM((tokens, D_GMLP + D_ATTN), _BF16),
        ],
        compiler_params=pltpu.CompilerParams(
            dimension_semantics=("arbitrary", "arbitrary"), vmem_limit_bytes=VMEM_LIMIT_BYTES),
        name="mixer",
    )(sinks, x, pos, invf, w_in, vg, vb, wsp, bsp, w_out, g1, b1)


def _ffn(x, w1, w2, g, b, *, alpha):
    n_tok, d_model = x.shape
    d_ff = w1.shape[1]
    tokens = FFN_TOKENS
    assert n_tok % tokens == 0 and d_ff % FFN_CHUNK == 0
    kernel = functools.partial(_ffn_kernel, alpha=alpha, d_ff=d_ff)
    tile = pl.BlockSpec((tokens, d_model), lambda i: (i, 0))
    return pl.pallas_call(
        kernel,
        grid=(n_tok // tokens,),
        in_specs=[tile, _resident(w1.shape), _resident(w2.shape), _resident(g.shape),
                  _resident(b.shape)],
        out_specs=tile,
        out_shape=jax.ShapeDtypeStruct(x.shape, x.dtype),
        compiler_params=pltpu.CompilerParams(
            dimension_semantics=("arbitrary",), vmem_limit_bytes=VMEM_LIMIT_BYTES),
        name="ffn",
    )(x, w1, w2, g, b)


def kernel(x, positions, w_in, v_ln_g, v_ln_b, w_spatial, b_spatial, sinks, w_out, ln1_g, ln1_b,
           w_ff1, w_ff2, ln2_g, ln2_b):
    batch, seq, d_model = x.shape
    depth = w_in.shape[0]
    alpha = (2.0 * depth) ** 0.25
    inv_freq = ROPE_THETA ** (-jnp.arange(0, HEAD_DIM, 2, dtype=_F32) / HEAD_DIM)
    invf = jnp.tile(inv_freq, LANES // HALF)[None, :]
    pos = positions.reshape(batch, seq, 1)
    row = lambda a: a[None, :]
    for l in range(depth):
        bsp = jnp.repeat(b_spatial[l].T, HEAD_DIM, axis=1)
        x = _mixer(x, pos, invf, sinks[l], w_in[l].astype(_BF16), row(v_ln_g[l]), row(v_ln_b[l]),
                   w_spatial[l], bsp, w_out[l].astype(_BF16), row(ln1_g[l]), row(ln1_b[l]),
                   alpha=alpha)
        x = _ffn(x.reshape(batch * seq, d_model), w_ff1[l].astype(_BF16), w_ff2[l].astype(_BF16),
                 row(ln2_g[l]), row(ln2_b[l]), alpha=alpha).reshape(batch, seq, d_model)
    return x
```

```python
import functools

import jax
import jax.numpy as jnp
from jax import lax
from jax.experimental import pallas as pl
from jax.experimental.pallas import tpu as pltpu

HEAD_DIM = 64
N_GMLP_HEADS = 8
D_GMLP = N_GMLP_HEADS * HEAD_DIM
N_Q_HEADS = 8
N_KV_HEADS = 2
GQA_GROUP = N_Q_HEADS // N_KV_HEADS
D_ATTN = N_Q_HEADS * HEAD_DIM
D_KV = N_KV_HEADS * HEAD_DIM
BLK = 128
ROPE_THETA = 10000.0
LN_EPS = 1e-5
NEG_INF = -1e30
LANES = 128
HALF = HEAD_DIM // 2

MIXER_TOKENS = 256
FFN_TOKENS = 512
FFN_CHUNK = 1024
VMEM_LIMIT_BYTES = 48 * 1024 * 1024

_BF16 = jnp.bfloat16
_F32 = jnp.float32


def _dot(a, b):
    return jnp.dot(a, b, preferred_element_type=_F32)


def _dot_nt(a, b):
    return lax.dot_general(a, b, (((1,), (1,)), ((), ())), preferred_element_type=_F32)


def _layer_norm(v, g, b):
    mu = jnp.mean(v, axis=-1, keepdims=True)
    c = v - mu
    var = jnp.mean(c * c, axis=-1, keepdims=True)
    return c * lax.rsqrt(var + LN_EPS) * g + b


def _mixer_kernel(sinks_ref, x_ref, pos_ref, invf_ref, w_in_ref, vg_ref, vb_ref, wsp_ref, bsp_ref,
                  w_out_ref, g1_ref, b1_ref, o_ref, kd_buf, vw_buf, mix_buf, *, tokens, alpha):
    n_blk = tokens // BLK
    n_slab_g = D_GMLP // LANES
    n_slab_a = D_ATTN // LANES

    @pl.when(pl.program_id(1) == 0)
    def _():
        kd_buf[:, 0:BLK, :] = jnp.zeros((N_KV_HEADS, BLK, LANES), _BF16)
        vw_buf[:, 0:BLK, :] = jnp.zeros((2 * N_KV_HEADS, BLK, LANES), _BF16)

    x = x_ref[...]
    xb = x.astype(_BF16)

    lane = lax.broadcasted_iota(jnp.int32, (1, LANES), 1)
    lo = lane < HEAD_DIM

    u = jax.nn.gelu(_dot(xb, w_in_ref[:, 0:D_GMLP]))
    vgel = jax.nn.gelu(_dot(xb, w_in_ref[:, D_GMLP:2 * D_GMLP]))
    vn = _layer_norm(vgel, vg_ref[...], vb_ref[...])
    lo_g = (lax.broadcasted_iota(jnp.int32, (1, D_GMLP), 1) & HEAD_DIM) == 0
    vn_top = jnp.where(lo_g, vn, 0.0).astype(_BF16)
    vn_bot = jnp.where(lo_g, 0.0, vn).astype(_BF16)

    ti = lax.broadcasted_iota(jnp.int32, (BLK, BLK), 0)
    si = lax.broadcasted_iota(jnp.int32, (BLK, BLK), 1)
    causal = si <= ti
    for p in range(n_slab_g):
        w_pair = jnp.concatenate(
            [jnp.where(causal, wsp_ref[2 * p], 0.0), jnp.where(causal, wsp_ref[2 * p + 1], 0.0)],
            axis=1).astype(_BF16)
        bias = bsp_ref[:, p * LANES:(p + 1) * LANES]
        for c in range(n_blk):
            rows = slice(c * BLK, (c + 1) * BLK)
            cols = slice(p * LANES, (p + 1) * LANES)
            rhs = jnp.concatenate([vn_top[rows, cols], vn_bot[rows, cols]], axis=0)
            mixed = _dot(w_pair, rhs) + bias
            mix_buf[rows, cols] = (u[rows, cols] * mixed).astype(_BF16)

    pos = pos_ref[...].astype(_F32)
    invf = invf_ref[...]
    n_grp = LANES // HALF
    quarter = tokens // n_grp
    grp = lane // HALF
    ang = None
    for a in range(n_grp):
        term = pos[a * quarter:(a + 1) * quarter] * jnp.where(grp == a, invf, 0.0)
        ang = term if ang is None else ang + term
    cos_packed, sin_packed = jnp.cos(ang), jnp.sin(ang)

    def spread(packed):
        parts = []
        for a in range(n_grp):
            one = jnp.where(grp == a, packed, 0.0)
            two = one + pltpu.roll(one, 2 * HALF, axis=1)
            parts.append(two + pltpu.roll(two, HALF, axis=1))
        return jnp.concatenate(parts, axis=0)

    cos = spread(cos_packed)
    first_half = (lane & HALF) == 0
    sin_signed = jnp.where(first_half, -1.0, 1.0) * spread(sin_packed)

    def rope(t):
        swapped = jnp.where(first_half, pltpu.roll(t, LANES - HALF, axis=1),
                            pltpu.roll(t, HALF, axis=1))
        return t * cos + swapped * sin_signed

    q0 = 2 * D_GMLP
    scale = HEAD_DIM ** -0.5
    q_even, q_odd = [], []
    for p in range(n_slab_a):
        qs = rope(_dot(xb, w_in_ref[:, q0 + p * LANES:q0 + (p + 1) * LANES])) * scale
        q_even.append(jnp.where(lo, qs, 0.0).astype(_BF16))
        q_odd.append(jnp.where(lo, 0.0, qs).astype(_BF16))

    k0 = q0 + D_ATTN
    kr = rope(_dot(xb, w_in_ref[:, k0:k0 + D_KV]))
    kr_sw = pltpu.roll(kr, HEAD_DIM, axis=1)
    cur = slice(BLK, BLK + tokens)
    kd_buf[0, cur, :] = jnp.where(lo, kr, kr_sw).astype(_BF16)
    kd_buf[1, cur, :] = jnp.where(lo, kr_sw, kr).astype(_BF16)
    v = _dot(xb, w_in_ref[:, k0 + D_KV:k0 + 2 * D_KV])
    v_sw = pltpu.roll(v, HEAD_DIM, axis=1)
    vw_buf[0, cur, :] = jnp.where(lo, v, 0.0).astype(_BF16)
    vw_buf[1, cur, :] = jnp.where(lo, 0.0, v_sw).astype(_BF16)
    vw_buf[2, cur, :] = jnp.where(lo, v_sw, 0.0).astype(_BF16)
    vw_buf[3, cur, :] = jnp.where(lo, 0.0, v).astype(_BF16)

    from_prev = si > ti
    no_prev = from_prev & (si < jnp.where(pl.program_id(1) == 0, BLK, 0))
    for n in range(n_blk):
        rows = slice(n * BLK, (n + 1) * BLK)
        kv_rows = slice(n * BLK, (n + 2) * BLK)
        for g in range(N_KV_HEADS):
            slabs = (2 * g, 2 * g + 1)
            q_stack = jnp.concatenate(
                [q_even[slabs[0]][rows], q_odd[slabs[0]][rows],
                 q_even[slabs[1]][rows], q_odd[slabs[1]][rows]], axis=0)
            scores = _dot_nt(q_stack, kd_buf[g, kv_rows, :])
            for j, slab in enumerate(slabs):
                acc = None
                inv_l = []
                for e in range(2):
                    head = 2 * slab + e
                    hl = 2 * j + e
                    sc = scores[hl * BLK:(hl + 1) * BLK]
                    sh = jnp.where(from_prev, sc[:, 0:BLK], sc[:, BLK:2 * BLK])
                    if n == 0:
                        sh = jnp.where(no_prev, NEG_INF, sh)
                    sink = sinks_ref[head]
                    m = jnp.maximum(jnp.max(sh, axis=-1, keepdims=True), sink)
                    pr = jnp.exp(sh - m)
                    l = jnp.sum(pr, axis=-1, keepdims=True) + jnp.exp(sink - m)
                    inv_l.append(1.0 / l)
                    pr_band = jnp.concatenate(
                        [jnp.where(from_prev, pr, 0.0), jnp.where(from_prev, 0.0, pr)], axis=1)
                    part = _dot(pr_band.astype(_BF16), vw_buf[2 * g + e, kv_rows, :])
                    acc = part if acc is None else acc + part
                out = acc * jnp.where(lo, inv_l[0], inv_l[1])
                mix_buf[rows, D_GMLP + slab * LANES:D_GMLP + (slab + 1) * LANES] = out.astype(_BF16)

    last = slice(tokens, tokens + BLK)
    kd_buf[:, 0:BLK, :] = kd_buf[:, last, :]
    vw_buf[:, 0:BLK, :] = vw_buf[:, last, :]

    y = alpha * x + _dot(mix_buf[...], w_out_ref[...])
    o_ref[...] = _layer_norm(y, g1_ref[...], b1_ref[...])


def _ffn_kernel(x_ref, w1_ref, w2_ref, g_ref, b_ref, o_ref, *, alpha, d_ff):
    x = x_ref[...]
    xb = x.astype(_BF16)
    acc = alpha * x
    for j in range(d_ff // FFN_CHUNK):
        cols = slice(j * FFN_CHUNK, (j + 1) * FFN_CHUNK)
        h = jnp.maximum(_dot(xb, w1_ref[:, cols]), 0.0)
        acc = acc + _dot((h * h).astype(_BF16), w2_ref[cols, :])
    o_ref[...] = _layer_norm(acc, g_ref[...], b_ref[...])


def _resident(shape):
    return pl.BlockSpec(shape, lambda *_: (0,) * len(shape), pipeline_mode=pl.Buffered(1))


def _mixer(x, pos, invf, sinks, w_in, vg, vb, wsp, bsp, w_out, g1, b1, *, alpha):
    batch, seq, d_model = x.shape
    tokens = MIXER_TOKENS
    assert seq % tokens == 0 and tokens % BLK == 0
    kernel = functools.partial(_mixer_kernel, tokens=tokens, alpha=alpha)
    tile = pl.BlockSpec((None, tokens, d_model), lambda b, s: (b, s, 0))
    return pl.pallas_call(
        kernel,
        grid=(batch, seq // tokens),
        in_specs=[
            pl.BlockSpec(memory_space=pltpu.SMEM),
            tile,
            pl.BlockSpec((None, tokens, 1), lambda b, s: (b, s, 0)),
            _resident(invf.shape), _resident(w_in.shape), _resident(vg.shape), _resident(vb.shape),
            _resident(wsp.shape), _resident(bsp.shape), _resident(w_out.shape),
            _resident(g1.shape), _resident(b1.shape),
        ],
        out_specs=tile,
        out_shape=jax.ShapeDtypeStruct(x.shape, x.dtype),
        scratch_shapes=[
            pltpu.VMEM((N_KV_HEADS, tokens + BLK, LANES), _BF16),
            pltpu.VMEM((2 * N_KV_HEADS, tokens + BLK, LANES), _BF16),
            pltpu.VMEM((tokens, D_GMLP + D_ATTN), _BF16),
        ],
        compiler_params=pltpu.CompilerParams(
            dimension_semantics=("arbitrary", "arbitrary"), vmem_limit_bytes=VMEM_LIMIT_BYTES),
        name="mixer",
    )(sinks, x, pos, invf, w_in, vg, vb, wsp, bsp, w_out, g1, b1)


def _ffn(x, w1, w2, g, b, *, alpha):
    n_tok, d_model = x.shape
    d_ff = w1.shape[1]
    tokens = FFN_TOKENS
    assert n_tok % tokens == 0 and d_ff % FFN_CHUNK == 0
    kernel = functools.partial(_ffn_kernel, alpha=alpha, d_ff=d_ff)
    tile = pl.BlockSpec((tokens, d_model), lambda i: (i, 0))
    return pl.pallas_call(
        kernel,
        grid=(n_tok // tokens,),
        in_specs=[tile, _resident(w1.shape), _resident(w2.shape), _resident(g.shape),
                  _resident(b.shape)],
        out_specs=tile,
        out_shape=jax.ShapeDtypeStruct(x.shape, x.dtype),
        compiler_params=pltpu.CompilerParams(
            dimension_semantics=("arbitrary",), vmem_limit_bytes=VMEM_LIMIT_BYTES),
        name="ffn",
    )(x, w1, w2, g, b)


def kernel(x, positions, w_in, v_ln_g, v_ln_b, w_spatial, b_spatial, sinks, w_out, ln1_g, ln1_b,
           w_ff1, w_ff2, ln2_g, ln2_b):
    batch, seq, d_model = x.shape
    depth = w_in.shape[0]
    alpha = (2.0 * depth) ** 0.25
    inv_freq = ROPE_THETA ** (-jnp.arange(0, HEAD_DIM, 2, dtype=_F32) / HEAD_DIM)
    invf = jnp.tile(inv_freq, LANES // HALF)[None, :]
    pos = positions.reshape(batch, seq, 1)
    row = lambda a: a[None, :]
    for l in range(depth):
        bsp = jnp.repeat(b_spatial[l].T, HEAD_DIM, axis=1)
        x = _mixer(x, pos, invf, sinks[l], w_in[l].astype(_BF16), row(v_ln_g[l]), row(v_ln_b[l]),
                   w_spatial[l], bsp, w_out[l].astype(_BF16), row(ln1_g[l]), row(ln1_b[l]),
                   alpha=alpha)
        x = _ffn(x.reshape(batch * seq, d_model), w_ff1[l].astype(_BF16), w_ff2[l].astype(_BF16),
                 row(ln2_g[l]), row(ln2_b[l]), alpha=alpha).reshape(batch, seq, d_model)
    return x
```

```python
import functools

import jax
import jax.numpy as jnp
from jax import lax
from jax.experimental import pallas as pl
from jax.experimental.pallas import tpu as pltpu

HEAD_DIM = 64
N_GMLP_HEADS = 8
D_GMLP = N_GMLP_HEADS * HEAD_DIM
N_Q_HEADS = 8
N_KV_HEADS = 2
GQA_GROUP = N_Q_HEADS // N_KV_HEADS
D_ATTN = N_Q_HEADS * HEAD_DIM
D_KV = N_KV_HEADS * HEAD_DIM
BLK = 128
ROPE_THETA = 10000.0
LN_EPS = 1e-5
NEG_INF = -1e30
LANES = 128
HALF = HEAD_DIM // 2

TILE_TOKENS = 512
FFN_CHUNK = 1024
VMEM_LIMIT_BYTES = 56 * 1024 * 1024

_BF16 = jnp.bfloat16
_F32 = jnp.float32


def _dot(a, b):
    return jnp.dot(a, b, preferred_element_type=_F32)


def _dot_nt(a, b):
    return lax.dot_general(a, b, (((1,), (1,)), ((), ())), preferred_element_type=_F32)


def _layer_norm(v, g, b):
    mu = jnp.mean(v, axis=-1, keepdims=True)
    c = v - mu
    var = jnp.mean(c * c, axis=-1, keepdims=True)
    return c * lax.rsqrt(var + LN_EPS) * g + b


def _mixer_tile(seq_start, sinks_ref, x_ref, pos_ref, invf_ref, w_in_ref, vg_ref, vb_ref, wsp_ref,
                bsp_ref, w_out_ref, g1_ref, b1_ref, x1_buf, kd_buf, vw_buf, mix_buf, *, alpha):
    tokens = x_ref.shape[0]
    n_blk = tokens // BLK
    n_slab_g = D_GMLP // LANES
    n_slab_a = D_ATTN // LANES

    x = x_ref[...]
    xb = x.astype(_BF16)

    lane = lax.broadcasted_iota(jnp.int32, (1, LANES), 1)
    lo = lane < HEAD_DIM

    u = jax.nn.gelu(_dot(xb, w_in_ref[:, 0:D_GMLP]))
    vgel = jax.nn.gelu(_dot(xb, w_in_ref[:, D_GMLP:2 * D_GMLP]))
    vn = _layer_norm(vgel, vg_ref[...], vb_ref[...])
    lo_g = (lax.broadcasted_iota(jnp.int32, (1, D_GMLP), 1) & HEAD_DIM) == 0
    vn_top = jnp.where(lo_g, vn, 0.0).astype(_BF16)
    vn_bot = jnp.where(lo_g, 0.0, vn).astype(_BF16)

    ti = lax.broadcasted_iota(jnp.int32, (BLK, BLK), 0)
    si = lax.broadcasted_iota(jnp.int32, (BLK, BLK), 1)
    causal = si <= ti
    for p in range(n_slab_g):
        w_pair = jnp.concatenate(
            [jnp.where(causal, wsp_ref[2 * p], 0.0), jnp.where(causal, wsp_ref[2 * p + 1], 0.0)],
            axis=1).astype(_BF16)
        bias = bsp_ref[:, p * LANES:(p + 1) * LANES]
        for c in range(n_blk):
            rows = slice(c * BLK, (c + 1) * BLK)
            cols = slice(p * LANES, (p + 1) * LANES)
            rhs = jnp.concatenate([vn_top[rows, cols], vn_bot[rows, cols]], axis=0)
            mixed = _dot(w_pair, rhs) + bias
            mix_buf[rows, cols] = (u[rows, cols] * mixed).astype(_BF16)

    pos = pos_ref[...].astype(_F32)
    invf = invf_ref[...]
    n_grp = LANES // HALF
    quarter = tokens // n_grp
    grp = lane // HALF
    ang = None
    for a in range(n_grp):
        term = pos[a * quarter:(a + 1) * quarter] * jnp.where(grp == a, invf, 0.0)
        ang = term if ang is None else ang + term
    cos_packed, sin_packed = jnp.cos(ang), jnp.sin(ang)

    def spread(packed):
        parts = []
        for a in range(n_grp):
            one = jnp.where(grp == a, packed, 0.0)
            two = one + pltpu.roll(one, 2 * HALF, axis=1)
            parts.append(two + pltpu.roll(two, HALF, axis=1))
        return jnp.concatenate(parts, axis=0)

    cos = spread(cos_packed)
    first_half = (lane & HALF) == 0
    sin_signed = jnp.where(first_half, -1.0, 1.0) * spread(sin_packed)

    def rope(t):
        swapped = jnp.where(first_half, pltpu.roll(t, LANES - HALF, axis=1),
                            pltpu.roll(t, HALF, axis=1))
        return t * cos + swapped * sin_signed

    q0 = 2 * D_GMLP
    scale = HEAD_DIM ** -0.5
    q_even, q_odd = [], []
    for p in range(n_slab_a):
        qs = rope(_dot(xb, w_in_ref[:, q0 + p * LANES:q0 + (p + 1) * LANES])) * scale
        q_even.append(jnp.where(lo, qs, 0.0).astype(_BF16))
        q_odd.append(jnp.where(lo, 0.0, qs).astype(_BF16))

    k0 = q0 + D_ATTN
    kr = rope(_dot(xb, w_in_ref[:, k0:k0 + D_KV]))
    kr_sw = pltpu.roll(kr, HEAD_DIM, axis=1)
    cur = slice(BLK, BLK + tokens)
    kd_buf[0, cur, :] = jnp.where(lo, kr, kr_sw).astype(_BF16)
    kd_buf[1, cur, :] = jnp.where(lo, kr_sw, kr).astype(_BF16)
    v = _dot(xb, w_in_ref[:, k0 + D_KV:k0 + 2 * D_KV])
    v_sw = pltpu.roll(v, HEAD_DIM, axis=1)
    vw_buf[0, cur, :] = jnp.where(lo, v, 0.0).astype(_BF16)
    vw_buf[1, cur, :] = jnp.where(lo, 0.0, v_sw).astype(_BF16)
    vw_buf[2, cur, :] = jnp.where(lo, v_sw, 0.0).astype(_BF16)
    vw_buf[3, cur, :] = jnp.where(lo, 0.0, v).astype(_BF16)

    from_prev = si > ti
    no_prev = from_prev & (si < jnp.where(seq_start, BLK, 0))
    for n in range(n_blk):
        rows = slice(n * BLK, (n + 1) * BLK)
        kv_rows = slice(n * BLK, (n + 2) * BLK)
        for g in range(N_KV_HEADS):
            slabs = (2 * g, 2 * g + 1)
            q_stack = jnp.concatenate(
                [q_even[slabs[0]][rows], q_odd[slabs[0]][rows],
                 q_even[slabs[1]][rows], q_odd[slabs[1]][rows]], axis=0)
            scores = _dot_nt(q_stack, kd_buf[g, kv_rows, :])
            for j, slab in enumerate(slabs):
                acc = None
                inv_l = []
                for e in range(2):
                    head = 2 * slab + e
                    hl = 2 * j + e
                    sc = scores[hl * BLK:(hl + 1) * BLK]
                    sh = jnp.where(from_prev, sc[:, 0:BLK], sc[:, BLK:2 * BLK])
                    if n == 0:
                        sh = jnp.where(no_prev, NEG_INF, sh)
                    sink = sinks_ref[head]
                    m = jnp.maximum(jnp.max(sh, axis=-1, keepdims=True), sink)
                    pr = jnp.exp(sh - m)
                    l = jnp.sum(pr, axis=-1, keepdims=True) + jnp.exp(sink - m)
                    inv_l.append(1.0 / l)
                    pr_band = jnp.concatenate(
                        [jnp.where(from_prev, pr, 0.0), jnp.where(from_prev, 0.0, pr)], axis=1)
                    part = _dot(pr_band.astype(_BF16), vw_buf[2 * g + e, kv_rows, :])
                    acc = part if acc is None else acc + part
                out = acc * jnp.where(lo, inv_l[0], inv_l[1])
                mix_buf[rows, D_GMLP + slab * LANES:D_GMLP + (slab + 1) * LANES] = out.astype(_BF16)

    last = slice(tokens, tokens + BLK)
    kd_buf[:, 0:BLK, :] = kd_buf[:, last, :]
    vw_buf[:, 0:BLK, :] = vw_buf[:, last, :]

    y = alpha * x + _dot(mix_buf[...], w_out_ref[...])
    x1_buf[...] = _layer_norm(y, g1_ref[...], b1_ref[...])


def _ffn_tile(x1_buf, w1_ref, w2_ref, g_ref, b_ref, o_ref, *, alpha):
    x = x1_buf[...]
    xb = x.astype(_BF16)
    acc = alpha * x
    for j in range(w1_ref.shape[1] // FFN_CHUNK):
        cols = slice(j * FFN_CHUNK, (j + 1) * FFN_CHUNK)
        h = jnp.maximum(_dot(xb, w1_ref[:, cols]), 0.0)
        acc = acc + _dot((h * h).astype(_BF16), w2_ref[cols, :])
    o_ref[...] = _layer_norm(acc, g_ref[...], b_ref[...])


def _layer_kernel(sinks_ref, x_ref, pos_ref, invf_ref, w_in_ref, vg_ref, vb_ref, wsp_ref, bsp_ref,
                  w_out_ref, g1_ref, b1_ref, w1_ref, w2_ref, g2_ref, b2_ref, o_ref,
                  x1_buf, kd_buf, vw_buf, mix_buf, *, alpha, n_tiles, tiles_per_seq):
    j = pl.program_id(0)
    seq_start = (j % tiles_per_seq) == 0

    @pl.when(seq_start)
    def _():
        kd_buf[:, 0:BLK, :] = jnp.zeros((N_KV_HEADS, BLK, LANES), _BF16)
        vw_buf[:, 0:BLK, :] = jnp.zeros((2 * N_KV_HEADS, BLK, LANES), _BF16)

    mixer = functools.partial(
        _mixer_tile, seq_start, sinks_ref, x_ref, pos_ref, invf_ref, w_in_ref,
        vg_ref, vb_ref, wsp_ref, bsp_ref, w_out_ref, g1_ref, b1_ref, x1_buf, kd_buf, vw_buf,
        mix_buf, alpha=alpha)
    ffn = functools.partial(_ffn_tile, x1_buf, w1_ref, w2_ref, g2_ref, b2_ref, o_ref, alpha=alpha)

    @pl.when(j == 0)
    def _():
        mixer()

    @pl.when((j > 0) & (j < n_tiles))
    def _():
        ffn()
        mixer()

    @pl.when(j == n_tiles)
    def _():
        ffn()


def _resident(shape):
    return pl.BlockSpec(shape, lambda *_: (0,) * len(shape), pipeline_mode=pl.Buffered(1))


def _layer(x, pos, invf, sinks, w_in, vg, vb, wsp, bsp, w_out, g1, b1, w1, w2, g2, b2, *, alpha,
           seq):
    n_tok, d_model = x.shape
    tokens = TILE_TOKENS
    assert seq % tokens == 0 and tokens % BLK == 0 and w1.shape[1] % FFN_CHUNK == 0
    n_tiles = n_tok // tokens
    kernel = functools.partial(_layer_kernel, alpha=alpha, n_tiles=n_tiles,
                               tiles_per_seq=seq // tokens)
    in_tile = lambda width: pl.BlockSpec(
        (tokens, width), lambda j: (jnp.minimum(j, n_tiles - 1), 0))
    out_tile = pl.BlockSpec((tokens, d_model), lambda j: (jnp.maximum(j - 1, 0), 0))
    resident = [invf, w_in, vg, vb, wsp, bsp, w_out, g1, b1, w1, w2, g2, b2]
    return pl.pallas_call(
        kernel,
        grid=(n_tiles + 1,),
        in_specs=[pl.BlockSpec(memory_space=pltpu.SMEM), in_tile(d_model), in_tile(1)]
        + [_resident(a.shape) for a in resident],
        out_specs=out_tile,
        out_shape=jax.ShapeDtypeStruct(x.shape, x.dtype),
        scratch_shapes=[
            pltpu.VMEM((tokens, d_model), _F32),
            pltpu.VMEM((N_KV_HEADS, tokens + BLK, LANES), _BF16),
            pltpu.VMEM((2 * N_KV_HEADS, tokens + BLK, LANES), _BF16),
            pltpu.VMEM((tokens, D_GMLP + D_ATTN), _BF16),
        ],
        compiler_params=pltpu.CompilerParams(
            dimension_semantics=("arbitrary",), vmem_limit_bytes=VMEM_LIMIT_BYTES),
        name="layer",
    )(sinks, x, pos, *resident)


def kernel(x, positions, w_in, v_ln_g, v_ln_b, w_spatial, b_spatial, sinks, w_out, ln1_g, ln1_b,
           w_ff1, w_ff2, ln2_g, ln2_b):
    batch, seq, d_model = x.shape
    depth = w_in.shape[0]
    alpha = (2.0 * depth) ** 0.25
    inv_freq = ROPE_THETA ** (-jnp.arange(0, HEAD_DIM, 2, dtype=_F32) / HEAD_DIM)
    invf = jnp.tile(inv_freq, LANES // HALF)[None, :]
    pos = positions.reshape(batch * seq, 1)
    x = x.reshape(batch * seq, d_model)
    row = lambda a: a[None, :]
    for l in range(depth):
        bsp = jnp.repeat(b_spatial[l].T, HEAD_DIM, axis=1)
        x = _layer(x, pos, invf, sinks[l], w_in[l].astype(_BF16), row(v_ln_g[l]), row(v_ln_b[l]),
                   w_spatial[l], bsp, w_out[l].astype(_BF16), row(ln1_g[l]), row(ln1_b[l]),
                   w_ff1[l].astype(_BF16), w_ff2[l].astype(_BF16), row(ln2_g[l]), row(ln2_b[l]),
                   alpha=alpha, seq=seq)
    return x.reshape(batch, seq, d_model)
```

```python
import functools

import jax
import jax.numpy as jnp
from jax import lax
from jax.experimental import pallas as pl
from jax.experimental.pallas import tpu as pltpu

HEAD_DIM = 64
N_GMLP_HEADS = 8
D_GMLP = N_GMLP_HEADS * HEAD_DIM
N_Q_HEADS = 8
N_KV_HEADS = 2
GQA_GROUP = N_Q_HEADS // N_KV_HEADS
D_ATTN = N_Q_HEADS * HEAD_DIM
D_KV = N_KV_HEADS * HEAD_DIM
BLK = 128
ROPE_THETA = 10000.0
LN_EPS = 1e-5
NEG_INF = -1e30
LANES = 128
HALF = HEAD_DIM // 2

TILE_TOKENS = 512
FFN_CHUNK = 1024
VMEM_LIMIT_BYTES = 56 * 1024 * 1024

_BF16 = jnp.bfloat16
_F32 = jnp.float32


def _dot(a, b):
    return jnp.dot(a, b, preferred_element_type=_F32)


def _dot_nt(a, b):
    return lax.dot_general(a, b, (((1,), (1,)), ((), ())), preferred_element_type=_F32)


def _layer_norm(v, g, b):
    mu = jnp.mean(v, axis=-1, keepdims=True)
    c = v - mu
    var = jnp.mean(c * c, axis=-1, keepdims=True)
    return c * lax.rsqrt(var + LN_EPS) * g + b


def _mixer_steps(seq_start, sinks_ref, x_ref, pos_ref, invf_ref, w_in_ref, vg_ref, vb_ref, wsp_ref,
                 bsp_ref, w_out_ref, y_buf, kd_buf, vw_buf, mix_buf, *, alpha):
    tokens = x_ref.shape[0]
    n_blk = tokens // BLK
    n_slab_g = D_GMLP // LANES
    n_slab_a = D_ATTN // LANES
    q0 = 2 * D_GMLP
    k0 = q0 + D_ATTN

    x = x_ref[...]
    xb = x.astype(_BF16)
    u_raw = _dot(xb, w_in_ref[:, 0:D_GMLP])
    vg_raw = _dot(xb, w_in_ref[:, D_GMLP:2 * D_GMLP])
    q_raw = [_dot(xb, w_in_ref[:, q0 + p * LANES:q0 + (p + 1) * LANES]) for p in range(n_slab_a)]
    k_raw = _dot(xb, w_in_ref[:, k0:k0 + D_KV])
    v = _dot(xb, w_in_ref[:, k0 + D_KV:k0 + 2 * D_KV])
    yield

    lane = lax.broadcasted_iota(jnp.int32, (1, LANES), 1)
    lo = lane < HEAD_DIM

    u = jax.nn.gelu(u_raw)
    vn = _layer_norm(jax.nn.gelu(vg_raw), vg_ref[...], vb_ref[...])
    lo_g = (lax.broadcasted_iota(jnp.int32, (1, D_GMLP), 1) & HEAD_DIM) == 0
    vn_top = jnp.where(lo_g, vn, 0.0).astype(_BF16)
    vn_bot = jnp.where(lo_g, 0.0, vn).astype(_BF16)
    ti = lax.broadcasted_iota(jnp.int32, (BLK, BLK), 0)
    si = lax.broadcasted_iota(jnp.int32, (BLK, BLK), 1)
    causal = si <= ti

    pos = pos_ref[...].astype(_F32)
    invf = invf_ref[...]
    n_grp = LANES // HALF
    quarter = tokens // n_grp
    grp = lane // HALF
    ang = None
    for a in range(n_grp):
        term = pos[a * quarter:(a + 1) * quarter] * jnp.where(grp == a, invf, 0.0)
        ang = term if ang is None else ang + term
    cos_packed, sin_packed = jnp.cos(ang), jnp.sin(ang)

    def spread(packed):
        parts = []
        for a in range(n_grp):
            one = jnp.where(grp == a, packed, 0.0)
            two = one + pltpu.roll(one, 2 * HALF, axis=1)
            parts.append(two + pltpu.roll(two, HALF, axis=1))
        return jnp.concatenate(parts, axis=0)

    cos = spread(cos_packed)
    first_half = (lane & HALF) == 0
    sin_signed = jnp.where(first_half, -1.0, 1.0) * spread(sin_packed)

    def rope(t):
        swapped = jnp.where(first_half, pltpu.roll(t, LANES - HALF, axis=1),
                            pltpu.roll(t, HALF, axis=1))
        return t * cos + swapped * sin_signed

    scale = HEAD_DIM ** -0.5
    q_even, q_odd = [], []
    for p in range(n_slab_a):
        qs = rope(q_raw[p]) * scale
        q_even.append(jnp.where(lo, qs, 0.0).astype(_BF16))
        q_odd.append(jnp.where(lo, 0.0, qs).astype(_BF16))
    kr = rope(k_raw)
    kr_sw = pltpu.roll(kr, HEAD_DIM, axis=1)
    cur = slice(BLK, BLK + tokens)
    kd_buf[0, cur, :] = jnp.where(lo, kr, kr_sw).astype(_BF16)
    kd_buf[1, cur, :] = jnp.where(lo, kr_sw, kr).astype(_BF16)
    v_sw = pltpu.roll(v, HEAD_DIM, axis=1)
    vw_buf[0, cur, :] = jnp.where(lo, v, 0.0).astype(_BF16)
    vw_buf[1, cur, :] = jnp.where(lo, 0.0, v_sw).astype(_BF16)
    vw_buf[2, cur, :] = jnp.where(lo, v_sw, 0.0).astype(_BF16)
    vw_buf[3, cur, :] = jnp.where(lo, 0.0, v).astype(_BF16)

    for p in range(n_slab_g):
        w_pair = jnp.concatenate(
            [jnp.where(causal, wsp_ref[2 * p], 0.0), jnp.where(causal, wsp_ref[2 * p + 1], 0.0)],
            axis=1).astype(_BF16)
        bias = bsp_ref[:, p * LANES:(p + 1) * LANES]
        for c in range(n_blk):
            rows = slice(c * BLK, (c + 1) * BLK)
            cols = slice(p * LANES, (p + 1) * LANES)
            rhs = jnp.concatenate([vn_top[rows, cols], vn_bot[rows, cols]], axis=0)
            mixed = _dot(w_pair, rhs) + bias
            mix_buf[rows, cols] = (u[rows, cols] * mixed).astype(_BF16)

    from_prev = si > ti
    no_prev = from_prev & (si < jnp.where(seq_start, BLK, 0))
    probs = {}
    for n in range(n_blk):
        rows = slice(n * BLK, (n + 1) * BLK)
        kv_rows = slice(n * BLK, (n + 2) * BLK)
        for g in range(N_KV_HEADS):
            heads = range(GQA_GROUP * g, GQA_GROUP * (g + 1))
            q_stack = jnp.concatenate(
                [(q_odd if h % 2 else q_even)[h // 2][rows] for h in heads], axis=0)
            scores = _dot_nt(q_stack, kd_buf[g, kv_rows, :])
            for hl, h in enumerate(heads):
                sc = scores[hl * BLK:(hl + 1) * BLK]
                sh = jnp.where(from_prev, sc[:, 0:BLK], sc[:, BLK:2 * BLK])
                if n == 0:
                    sh = jnp.where(no_prev, NEG_INF, sh)
                sink = sinks_ref[h]
                m = jnp.maximum(jnp.max(sh, axis=-1, keepdims=True), sink)
                pr = jnp.exp(sh - m)
                l = jnp.sum(pr, axis=-1, keepdims=True) + jnp.exp(sink - m)
                p_band = jnp.concatenate(
                    [jnp.where(from_prev, pr, 0.0), jnp.where(from_prev, 0.0, pr)], axis=1)
                probs[n, h] = (p_band.astype(_BF16), 1.0 / l)
    yield

    for n in range(n_blk):
        rows = slice(n * BLK, (n + 1) * BLK)
        kv_rows = slice(n * BLK, (n + 2) * BLK)
        for slab in range(n_slab_a):
            g = (2 * slab) // GQA_GROUP
            (p_e, inv_e), (p_o, inv_o) = probs[n, 2 * slab], probs[n, 2 * slab + 1]
            acc = (_dot(p_e, vw_buf[2 * g, kv_rows, :]) + _dot(p_o, vw_buf[2 * g + 1, kv_rows, :]))
            out = acc * jnp.where(lo, inv_e, inv_o)
            mix_buf[rows, D_GMLP + slab * LANES:D_GMLP + (slab + 1) * LANES] = out.astype(_BF16)
    last = slice(tokens, tokens + BLK)
    kd_buf[:, 0:BLK, :] = kd_buf[:, last, :]
    vw_buf[:, 0:BLK, :] = vw_buf[:, last, :]
    yield

    y_buf[...] = alpha * x + _dot(mix_buf[...], w_out_ref[...])


def _ffn_steps(y_buf, g1_ref, b1_ref, w1_ref, w2_ref, g2_ref, b2_ref, o_ref, *, alpha):
    x = _layer_norm(y_buf[...], g1_ref[...], b1_ref[...])
    xb = x.astype(_BF16)
    acc = alpha * x
    yield
    for j in range(w1_ref.shape[1] // FFN_CHUNK):
        cols = slice(j * FFN_CHUNK, (j + 1) * FFN_CHUNK)
        h = jnp.maximum(_dot(xb, w1_ref[:, cols]), 0.0)
        acc = acc + _dot((h * h).astype(_BF16), w2_ref[cols, :])
        yield
    o_ref[...] = _layer_norm(acc, g2_ref[...], b2_ref[...])


def _run(steps, count=None):
    if count is None:
        for _ in steps:
            pass
    else:
        for _ in range(count):
            next(steps)


def _layer_kernel(sinks_ref, x_ref, pos_ref, invf_ref, w_in_ref, vg_ref, vb_ref, wsp_ref, bsp_ref,
                  w_out_ref, g1_ref, b1_ref, w1_ref, w2_ref, g2_ref, b2_ref, o_ref,
                  y_buf, kd_buf, vw_buf, mix_buf, *, alpha, n_tiles, tiles_per_seq):
    j = pl.program_id(0)
    seq_start = (j % tiles_per_seq) == 0

    @pl.when(seq_start)
    def _():
        kd_buf[:, 0:BLK, :] = jnp.zeros((N_KV_HEADS, BLK, LANES), _BF16)
        vw_buf[:, 0:BLK, :] = jnp.zeros((2 * N_KV_HEADS, BLK, LANES), _BF16)

    mixer = functools.partial(
        _mixer_steps, seq_start, sinks_ref, x_ref, pos_ref, invf_ref, w_in_ref, vg_ref, vb_ref,
        wsp_ref, bsp_ref, w_out_ref, y_buf, kd_buf, vw_buf, mix_buf, alpha=alpha)
    ffn = functools.partial(_ffn_steps, y_buf, g1_ref, b1_ref, w1_ref, w2_ref, g2_ref, b2_ref,
                            o_ref, alpha=alpha)

    @pl.when(j == 0)
    def _():
        _run(mixer())

    @pl.when((j > 0) & (j < n_tiles))
    def _():
        f, m = ffn(), mixer()
        _run(f, 1)
        _run(m, 1)
        _run(f, 1)
        _run(m, 1)
        _run(f, 1)
        _run(m, 1)
        _run(f, 2)
        _run(m)
        _run(f)

    @pl.when(j == n_tiles)
    def _():
        _run(ffn())


def _resident(shape):
    return pl.BlockSpec(shape, lambda *_: (0,) * len(shape), pipeline_mode=pl.Buffered(1))


def _layer(x, pos, invf, sinks, w_in, vg, vb, wsp, bsp, w_out, g1, b1, w1, w2, g2, b2, *, alpha,
           seq):
    n_tok, d_model = x.shape
    tokens = TILE_TOKENS
    assert seq % tokens == 0 and tokens % BLK == 0 and w1.shape[1] % FFN_CHUNK == 0
    n_tiles = n_tok // tokens
    kernel = functools.partial(_layer_kernel, alpha=alpha, n_tiles=n_tiles,
                               tiles_per_seq=seq // tokens)
    in_tile = lambda width: pl.BlockSpec(
        (tokens, width), lambda j: (jnp.minimum(j, n_tiles - 1), 0))
    out_tile = pl.BlockSpec((tokens, d_model), lambda j: (jnp.maximum(j - 1, 0), 0))
    resident = [invf, w_in, vg, vb, wsp, bsp, w_out, g1, b1, w1, w2, g2, b2]
    return pl.pallas_call(
        kernel,
        grid=(n_tiles + 1,),
        in_specs=[pl.BlockSpec(memory_space=pltpu.SMEM), in_tile(d_model), in_tile(1)]
        + [_resident(a.shape) for a in resident],
        out_specs=out_tile,
        out_shape=jax.ShapeDtypeStruct(x.shape, x.dtype),
        scratch_shapes=[
            pltpu.VMEM((tokens, d_model), _F32),
            pltpu.VMEM((N_KV_HEADS, tokens + BLK, LANES), _BF16),
            pltpu.VMEM((2 * N_KV_HEADS, tokens + BLK, LANES), _BF16),
            pltpu.VMEM((tokens, D_GMLP + D_ATTN), _BF16),
        ],
        compiler_params=pltpu.CompilerParams(
            dimension_semantics=("arbitrary",), vmem_limit_bytes=VMEM_LIMIT_BYTES),
        name="layer",
    )(sinks, x, pos, *resident)


def kernel(x, positions, w_in, v_ln_g, v_ln_b, w_spatial, b_spatial, sinks, w_out, ln1_g, ln1_b,
           w_ff1, w_ff2, ln2_g, ln2_b):
    batch, seq, d_model = x.shape
    depth = w_in.shape[0]
    alpha = (2.0 * depth) ** 0.25
    inv_freq = ROPE_THETA ** (-jnp.arange(0, HEAD_DIM, 2, dtype=_F32) / HEAD_DIM)
    invf = jnp.tile(inv_freq, LANES // HALF)[None, :]
    pos = positions.reshape(batch * seq, 1)
    x = x.reshape(batch * seq, d_model)
    row = lambda a: a[None, :]
    for l in range(depth):
        bsp = jnp.repeat(b_spatial[l].T, HEAD_DIM, axis=1)
        x = _layer(x, pos, invf, sinks[l], w_in[l].astype(_BF16), row(v_ln_g[l]), row(v_ln_b[l]),
                   w_spatial[l], bsp, w_out[l].astype(_BF16), row(ln1_g[l]), row(ln1_b[l]),
                   w_ff1[l].astype(_BF16), w_ff2[l].astype(_BF16), row(ln2_g[l]), row(ln2_b[l]),
                   alpha=alpha, seq=seq)
    return x.reshape(batch, seq, d_model)
```

```python
import functools

import jax
import jax.numpy as jnp
from jax import lax
from jax.experimental import pallas as pl
from jax.experimental.pallas import tpu as pltpu

HEAD_DIM = 64
N_GMLP_HEADS = 8
D_GMLP = N_GMLP_HEADS * HEAD_DIM
N_Q_HEADS = 8
N_KV_HEADS = 2
GQA_GROUP = N_Q_HEADS // N_KV_HEADS
D_ATTN = N_Q_HEADS * HEAD_DIM
D_KV = N_KV_HEADS * HEAD_DIM
BLK = 128
ROPE_THETA = 10000.0
LN_EPS = 1e-5
NEG_INF = -1e30
LANES = 128
HALF = HEAD_DIM // 2

TILE_TOKENS = 512
FFN_CHUNK = 1024
VMEM_LIMIT_BYTES = 56 * 1024 * 1024

_BF16 = jnp.bfloat16
_F32 = jnp.float32


def _dot(a, b):
    return jnp.dot(a, b, preferred_element_type=_F32)


def _dot_nt(a, b):
    return lax.dot_general(a, b, (((1,), (1,)), ((), ())), preferred_element_type=_F32)


def _layer_norm(v, g, b):
    mu = jnp.mean(v, axis=-1, keepdims=True)
    c = v - mu
    var = jnp.mean(c * c, axis=-1, keepdims=True)
    return c * lax.rsqrt(var + LN_EPS) * g + b


def _mixer_steps(seq_start, sinks_ref, x_ref, pos_ref, invf_ref, w_in_ref, vg_ref, vb_ref, wsp_ref,
                 bsp_ref, w_out_ref, y_buf, kd_buf, vw_buf, mix_buf, *, alpha):
    tokens = x_ref.shape[0]
    n_blk = tokens // BLK
    n_slab_g = D_GMLP // LANES
    n_slab_a = D_ATTN // LANES
    q0 = 2 * D_GMLP
    k0 = q0 + D_ATTN

    x = x_ref[...]
    xb = x.astype(_BF16)
    u_raw = _dot(xb, w_in_ref[:, 0:D_GMLP])
    vg_raw = _dot(xb, w_in_ref[:, D_GMLP:2 * D_GMLP])
    qkv = _dot(xb, w_in_ref[:, q0:k0 + 2 * D_KV])
    q_raw = [qkv[:, p * LANES:(p + 1) * LANES] for p in range(n_slab_a)]
    k_raw = qkv[:, D_ATTN:D_ATTN + D_KV]
    v = qkv[:, D_ATTN + D_KV:D_ATTN + 2 * D_KV]
    yield

    lane = lax.broadcasted_iota(jnp.int32, (1, LANES), 1)
    lo = lane < HEAD_DIM

    u = jax.nn.gelu(u_raw)
    vn = _layer_norm(jax.nn.gelu(vg_raw), vg_ref[...], vb_ref[...])
    lo_g = (lax.broadcasted_iota(jnp.int32, (1, D_GMLP), 1) & HEAD_DIM) == 0
    vn_top = jnp.where(lo_g, vn, 0.0).astype(_BF16)
    vn_bot = jnp.where(lo_g, 0.0, vn).astype(_BF16)
    ti = lax.broadcasted_iota(jnp.int32, (BLK, BLK), 0)
    si = lax.broadcasted_iota(jnp.int32, (BLK, BLK), 1)
    causal = si <= ti

    pos = pos_ref[...].astype(_F32)
    invf = invf_ref[...]
    n_grp = LANES // HALF
    quarter = tokens // n_grp
    grp = lane // HALF
    ang = None
    for a in range(n_grp):
        term = pos[a * quarter:(a + 1) * quarter] * jnp.where(grp == a, invf, 0.0)
        ang = term if ang is None else ang + term
    cos_packed, sin_packed = jnp.cos(ang), jnp.sin(ang)

    def spread(packed):
        parts = []
        for a in range(n_grp):
            one = jnp.where(grp == a, packed, 0.0)
            two = one + pltpu.roll(one, 2 * HALF, axis=1)
            parts.append(two + pltpu.roll(two, HALF, axis=1))
        return jnp.concatenate(parts, axis=0)

    cos = spread(cos_packed)
    first_half = (lane & HALF) == 0
    sin_signed = jnp.where(first_half, -1.0, 1.0) * spread(sin_packed)

    def rope(t):
        swapped = jnp.where(first_half, pltpu.roll(t, LANES - HALF, axis=1),
                            pltpu.roll(t, HALF, axis=1))
        return t * cos + swapped * sin_signed

    scale = HEAD_DIM ** -0.5
    q_even, q_odd = [], []
    for p in range(n_slab_a):
        qs = rope(q_raw[p]) * scale
        q_even.append(jnp.where(lo, qs, 0.0).astype(_BF16))
        q_odd.append(jnp.where(lo, 0.0, qs).astype(_BF16))
    kr = rope(k_raw)
    kr_sw = pltpu.roll(kr, HEAD_DIM, axis=1)
    cur = slice(BLK, BLK + tokens)
    kd_buf[0, cur, :] = jnp.where(lo, kr, kr_sw).astype(_BF16)
    kd_buf[1, cur, :] = jnp.where(lo, kr_sw, kr).astype(_BF16)
    v_sw = pltpu.roll(v, HEAD_DIM, axis=1)
    vw_buf[0, cur, :] = jnp.where(lo, v, 0.0).astype(_BF16)
    vw_buf[1, cur, :] = jnp.where(lo, 0.0, v_sw).astype(_BF16)
    vw_buf[2, cur, :] = jnp.where(lo, v_sw, 0.0).astype(_BF16)
    vw_buf[3, cur, :] = jnp.where(lo, 0.0, v).astype(_BF16)

    for p in range(n_slab_g):
        w_pair = jnp.concatenate(
            [jnp.where(causal, wsp_ref[2 * p], 0.0), jnp.where(causal, wsp_ref[2 * p + 1], 0.0)],
            axis=1).astype(_BF16)
        bias = bsp_ref[:, p * LANES:(p + 1) * LANES]
        for c in range(n_blk):
            rows = slice(c * BLK, (c + 1) * BLK)
            cols = slice(p * LANES, (p + 1) * LANES)
            rhs = jnp.concatenate([vn_top[rows, cols], vn_bot[rows, cols]], axis=0)
            mixed = _dot(w_pair, rhs) + bias
            mix_buf[rows, cols] = (u[rows, cols] * mixed).astype(_BF16)

    from_prev = si > ti
    no_prev = from_prev & (si < jnp.where(seq_start, BLK, 0))
    probs = {}
    for n in range(n_blk):
        rows = slice(n * BLK, (n + 1) * BLK)
        kv_rows = slice(n * BLK, (n + 2) * BLK)
        for g in range(N_KV_HEADS):
            heads = range(GQA_GROUP * g, GQA_GROUP * (g + 1))
            q_stack = jnp.concatenate(
                [(q_odd if h % 2 else q_even)[h // 2][rows] for h in heads], axis=0)
            scores = _dot_nt(q_stack, kd_buf[g, kv_rows, :])
            for hl, h in enumerate(heads):
                sc = scores[hl * BLK:(hl + 1) * BLK]
                sh = jnp.where(from_prev, sc[:, 0:BLK], sc[:, BLK:2 * BLK])
                if n == 0:
                    sh = jnp.where(no_prev, NEG_INF, sh)
                sink = sinks_ref[h]
                m = jnp.maximum(jnp.max(sh, axis=-1, keepdims=True), sink)
                pr = jnp.exp(sh - m)
                l = jnp.sum(pr, axis=-1, keepdims=True) + jnp.exp(sink - m)
                p_band = jnp.concatenate(
                    [jnp.where(from_prev, pr, 0.0), jnp.where(from_prev, 0.0, pr)], axis=1)
                probs[n, h] = (p_band.astype(_BF16), 1.0 / l)
    yield

    for n in range(n_blk):
        rows = slice(n * BLK, (n + 1) * BLK)
        kv_rows = slice(n * BLK, (n + 2) * BLK)
        for slab in range(n_slab_a):
            g = (2 * slab) // GQA_GROUP
            (p_e, inv_e), (p_o, inv_o) = probs[n, 2 * slab], probs[n, 2 * slab + 1]
            acc = (_dot(p_e, vw_buf[2 * g, kv_rows, :]) + _dot(p_o, vw_buf[2 * g + 1, kv_rows, :]))
            out = acc * jnp.where(lo, inv_e, inv_o)
            mix_buf[rows, D_GMLP + slab * LANES:D_GMLP + (slab + 1) * LANES] = out.astype(_BF16)
    last = slice(tokens, tokens + BLK)
    kd_buf[:, 0:BLK, :] = kd_buf[:, last, :]
    vw_buf[:, 0:BLK, :] = vw_buf[:, last, :]
    yield

    y_buf[...] = alpha * x + _dot(mix_buf[...], w_out_ref[...])


def _ffn_steps(y_buf, g1_ref, b1_ref, w1_ref, w2_ref, g2_ref, b2_ref, o_ref, *, alpha):
    x = _layer_norm(y_buf[...], g1_ref[...], b1_ref[...])
    xb = x.astype(_BF16)
    acc = alpha * x
    yield
    for j in range(w1_ref.shape[1] // FFN_CHUNK):
        cols = slice(j * FFN_CHUNK, (j + 1) * FFN_CHUNK)
        h = jnp.maximum(_dot(xb, w1_ref[:, cols]), 0.0)
        acc = acc + _dot((h * h).astype(_BF16), w2_ref[cols, :])
        yield
    o_ref[...] = _layer_norm(acc, g2_ref[...], b2_ref[...])


def _run(steps, count=None):
    if count is None:
        for _ in steps:
            pass
    else:
        for _ in range(count):
            next(steps)


def _layer_kernel(sinks_ref, x_ref, pos_ref, invf_ref, w_in_ref, vg_ref, vb_ref, wsp_ref, bsp_ref,
                  w_out_ref, g1_ref, b1_ref, w1_ref, w2_ref, g2_ref, b2_ref, o_ref,
                  y_buf, kd_buf, vw_buf, mix_buf, *, alpha, n_tiles, tiles_per_seq):
    j = pl.program_id(0)
    seq_start = (j % tiles_per_seq) == 0

    @pl.when(seq_start)
    def _():
        kd_buf[:, 0:BLK, :] = jnp.zeros((N_KV_HEADS, BLK, LANES), _BF16)
        vw_buf[:, 0:BLK, :] = jnp.zeros((2 * N_KV_HEADS, BLK, LANES), _BF16)

    mixer = functools.partial(
        _mixer_steps, seq_start, sinks_ref, x_ref, pos_ref, invf_ref, w_in_ref, vg_ref, vb_ref,
        wsp_ref, bsp_ref, w_out_ref, y_buf, kd_buf, vw_buf, mix_buf, alpha=alpha)
    ffn = functools.partial(_ffn_steps, y_buf, g1_ref, b1_ref, w1_ref, w2_ref, g2_ref, b2_ref,
                            o_ref, alpha=alpha)

    @pl.when(j == 0)
    def _():
        _run(mixer())

    @pl.when((j > 0) & (j < n_tiles))
    def _():
        f, m = ffn(), mixer()
        _run(f, 1)
        _run(m, 1)
        _run(f, 1)
        _run(m, 1)
        _run(f, 1)
        _run(m, 1)
        _run(f, 2)
        _run(m)
        _run(f)

    @pl.when(j == n_tiles)
    def _():
        _run(ffn())


def _resident(shape):
    return pl.BlockSpec(shape, lambda *_: (0,) * len(shape), pipeline_mode=pl.Buffered(1))


def _layer(x, pos, invf, sinks, w_in, vg, vb, wsp, bsp, w_out, g1, b1, w1, w2, g2, b2, *, alpha,
           seq):
    n_tok, d_model = x.shape
    tokens = TILE_TOKENS
    assert seq % tokens == 0 and tokens % BLK == 0 and w1.shape[1] % FFN_CHUNK == 0
    n_tiles = n_tok // tokens
    kernel = functools.partial(_layer_kernel, alpha=alpha, n_tiles=n_tiles,
                               tiles_per_seq=seq // tokens)
    in_tile = lambda width: pl.BlockSpec(
        (tokens, width), lambda j: (jnp.minimum(j, n_tiles - 1), 0))
    out_tile = pl.BlockSpec((tokens, d_model), lambda j: (jnp.maximum(j - 1, 0), 0))
    resident = [invf, w_in, vg, vb, wsp, bsp, w_out, g1, b1, w1, w2, g2, b2]
    return pl.pallas_call(
        kernel,
        grid=(n_tiles + 1,),
        in_specs=[pl.BlockSpec(memory_space=pltpu.SMEM), in_tile(d_model), in_tile(1)]
        + [_resident(a.shape) for a in resident],
        out_specs=out_tile,
        out_shape=jax.ShapeDtypeStruct(x.shape, x.dtype),
        scratch_shapes=[
            pltpu.VMEM((tokens, d_model), _F32),
            pltpu.VMEM((N_KV_HEADS, tokens + BLK, LANES), _BF16),
            pltpu.VMEM((2 * N_KV_HEADS, tokens + BLK, LANES), _BF16),
            pltpu.VMEM((tokens, D_GMLP + D_ATTN), _BF16),
        ],
        compiler_params=pltpu.CompilerParams(
            dimension_semantics=("arbitrary",), vmem_limit_bytes=VMEM_LIMIT_BYTES),
        name="layer",
    )(sinks, x, pos, *resident)


def kernel(x, positions, w_in, v_ln_g, v_ln_b, w_spatial, b_spatial, sinks, w_out, ln1_g, ln1_b,
           w_ff1, w_ff2, ln2_g, ln2_b):
    batch, seq, d_model = x.shape
    depth = w_in.shape[0]
    alpha = (2.0 * depth) ** 0.25
    inv_freq = ROPE_THETA ** (-jnp.arange(0, HEAD_DIM, 2, dtype=_F32) / HEAD_DIM)
    invf = jnp.tile(inv_freq, LANES // HALF)[None, :]
    pos = positions.reshape(batch * seq, 1)
    x = x.reshape(batch * seq, d_model)
    row = lambda a: a[None, :]
    for l in range(depth):
        bsp = jnp.repeat(b_spatial[l].T, HEAD_DIM, axis=1)
        x = _layer(x, pos, invf, sinks[l], w_in[l].astype(_BF16), row(v_ln_g[l]), row(v_ln_b[l]),
                   w_spatial[l], bsp, w_out[l].astype(_BF16), row(ln1_g[l]), row(ln1_b[l]),
                   w_ff1[l].astype(_BF16), w_ff2[l].astype(_BF16), row(ln2_g[l]), row(ln2_b[l]),
                   alpha=alpha, seq=seq)
    return x.reshape(batch, seq, d_model)
```

```python
import functools

import jax
import jax.numpy as jnp
from jax import lax
from jax.experimental import pallas as pl
from jax.experimental.pallas import tpu as pltpu

HEAD_DIM = 64
N_GMLP_HEADS = 8
D_GMLP = N_GMLP_HEADS * HEAD_DIM
N_Q_HEADS = 8
N_KV_HEADS = 2
GQA_GROUP = N_Q_HEADS // N_KV_HEADS
D_ATTN = N_Q_HEADS * HEAD_DIM
D_KV = N_KV_HEADS * HEAD_DIM
BLK = 128
ROPE_THETA = 10000.0
LN_EPS = 1e-5
NEG_INF = -1e30
LANES = 128
HALF = HEAD_DIM // 2

TILE_TOKENS = 512
FFN_CHUNK = 1024
STAGE_ELEMS = 256 * 1024
VMEM_LIMIT_BYTES = 56 * 1024 * 1024

_BF16 = jnp.bfloat16
_F32 = jnp.float32


def _dot(a, b):
    return jnp.dot(a, b, preferred_element_type=_F32)


def _dot_nt(a, b):
    return lax.dot_general(a, b, (((1,), (1,)), ((), ())), preferred_element_type=_F32)


def _layer_norm(v, g, b):
    mu = jnp.mean(v, axis=-1, keepdims=True)
    c = v - mu
    var = jnp.mean(c * c, axis=-1, keepdims=True)
    return c * lax.rsqrt(var + LN_EPS) * g + b


def _mixer_steps(seq_start, sinks_ref, x_ref, pos_ref, invf_ref, w_in_ref, vg_ref, vb_ref, wsp_ref,
                 bsp_ref, w_out_ref, y_buf, kd_buf, vw_buf, mix_buf, *, alpha):
    tokens = x_ref.shape[0]
    n_blk = tokens // BLK
    n_slab_g = D_GMLP // LANES
    n_slab_a = D_ATTN // LANES
    q0 = 2 * D_GMLP
    k0 = q0 + D_ATTN

    x = x_ref[...]
    xb = x.astype(_BF16)
    u_raw = _dot(xb, w_in_ref[:, 0:D_GMLP])
    vg_raw = _dot(xb, w_in_ref[:, D_GMLP:2 * D_GMLP])
    qkv = _dot(xb, w_in_ref[:, q0:k0 + 2 * D_KV])
    q_raw = [qkv[:, p * LANES:(p + 1) * LANES] for p in range(n_slab_a)]
    k_raw = qkv[:, D_ATTN:D_ATTN + D_KV]
    v = qkv[:, D_ATTN + D_KV:D_ATTN + 2 * D_KV]
    yield

    lane = lax.broadcasted_iota(jnp.int32, (1, LANES), 1)
    lo = lane < HEAD_DIM

    u = jax.nn.gelu(u_raw)
    vn = _layer_norm(jax.nn.gelu(vg_raw), vg_ref[...], vb_ref[...])
    lo_g = (lax.broadcasted_iota(jnp.int32, (1, D_GMLP), 1) & HEAD_DIM) == 0
    vn_top = jnp.where(lo_g, vn, 0.0).astype(_BF16)
    vn_bot = jnp.where(lo_g, 0.0, vn).astype(_BF16)
    ti = lax.broadcasted_iota(jnp.int32, (BLK, BLK), 0)
    si = lax.broadcasted_iota(jnp.int32, (BLK, BLK), 1)
    causal = si <= ti

    pos = pos_ref[...].astype(_F32)
    invf = invf_ref[...]
    n_grp = LANES // HALF
    quarter = tokens // n_grp
    grp = lane // HALF
    ang = None
    for a in range(n_grp):
        term = pos[a * quarter:(a + 1) * quarter] * jnp.where(grp == a, invf, 0.0)
        ang = term if ang is None else ang + term
    cos_packed, sin_packed = jnp.cos(ang), jnp.sin(ang)

    def spread(packed):
        parts = []
        for a in range(n_grp):
            one = jnp.where(grp == a, packed, 0.0)
            two = one + pltpu.roll(one, 2 * HALF, axis=1)
            parts.append(two + pltpu.roll(two, HALF, axis=1))
        return jnp.concatenate(parts, axis=0)

    cos = spread(cos_packed)
    first_half = (lane & HALF) == 0
    sin_signed = jnp.where(first_half, -1.0, 1.0) * spread(sin_packed)

    def rope(t):
        swapped = jnp.where(first_half, pltpu.roll(t, LANES - HALF, axis=1),
                            pltpu.roll(t, HALF, axis=1))
        return t * cos + swapped * sin_signed

    scale = HEAD_DIM ** -0.5
    q_even, q_odd = [], []
    for p in range(n_slab_a):
        qs = rope(q_raw[p]) * scale
        q_even.append(jnp.where(lo, qs, 0.0).astype(_BF16))
        q_odd.append(jnp.where(lo, 0.0, qs).astype(_BF16))
    kr = rope(k_raw)
    kr_sw = pltpu.roll(kr, HEAD_DIM, axis=1)
    cur = slice(BLK, BLK + tokens)
    kd_buf[0, cur, :] = jnp.where(lo, kr, kr_sw).astype(_BF16)
    kd_buf[1, cur, :] = jnp.where(lo, kr_sw, kr).astype(_BF16)
    v_sw = pltpu.roll(v, HEAD_DIM, axis=1)
    vw_buf[0, cur, :] = jnp.where(lo, v, 0.0).astype(_BF16)
    vw_buf[1, cur, :] = jnp.where(lo, 0.0, v_sw).astype(_BF16)
    vw_buf[2, cur, :] = jnp.where(lo, v_sw, 0.0).astype(_BF16)
    vw_buf[3, cur, :] = jnp.where(lo, 0.0, v).astype(_BF16)

    for p in range(n_slab_g):
        w_pair = jnp.concatenate(
            [jnp.where(causal, wsp_ref[2 * p], 0.0), jnp.where(causal, wsp_ref[2 * p + 1], 0.0)],
            axis=1).astype(_BF16)
        bias = bsp_ref[:, p * LANES:(p + 1) * LANES]
        for c in range(n_blk):
            rows = slice(c * BLK, (c + 1) * BLK)
            cols = slice(p * LANES, (p + 1) * LANES)
            rhs = jnp.concatenate([vn_top[rows, cols], vn_bot[rows, cols]], axis=0)
            mixed = _dot(w_pair, rhs) + bias
            mix_buf[rows, cols] = (u[rows, cols] * mixed).astype(_BF16)

    from_prev = si > ti
    no_prev = from_prev & (si < jnp.where(seq_start, BLK, 0))
    probs = {}
    for n in range(n_blk):
        rows = slice(n * BLK, (n + 1) * BLK)
        kv_rows = slice(n * BLK, (n + 2) * BLK)
        for g in range(N_KV_HEADS):
            heads = range(GQA_GROUP * g, GQA_GROUP * (g + 1))
            q_stack = jnp.concatenate(
                [(q_odd if h % 2 else q_even)[h // 2][rows] for h in heads], axis=0)
            scores = _dot_nt(q_stack, kd_buf[g, kv_rows, :])
            for hl, h in enumerate(heads):
                sc = scores[hl * BLK:(hl + 1) * BLK]
                sh = jnp.where(from_prev, sc[:, 0:BLK], sc[:, BLK:2 * BLK])
                if n == 0:
                    sh = jnp.where(no_prev, NEG_INF, sh)
                sink = sinks_ref[h]
                m = jnp.maximum(jnp.max(sh, axis=-1, keepdims=True), sink)
                pr = jnp.exp(sh - m)
                l = jnp.sum(pr, axis=-1, keepdims=True) + jnp.exp(sink - m)
                p_band = jnp.concatenate(
                    [jnp.where(from_prev, pr, 0.0), jnp.where(from_prev, 0.0, pr)], axis=1)
                probs[n, h] = (p_band.astype(_BF16), 1.0 / l)
    yield

    for n in range(n_blk):
        rows = slice(n * BLK, (n + 1) * BLK)
        kv_rows = slice(n * BLK, (n + 2) * BLK)
        for slab in range(n_slab_a):
            g = (2 * slab) // GQA_GROUP
            (p_e, inv_e), (p_o, inv_o) = probs[n, 2 * slab], probs[n, 2 * slab + 1]
            acc = (_dot(p_e, vw_buf[2 * g, kv_rows, :]) + _dot(p_o, vw_buf[2 * g + 1, kv_rows, :]))
            out = acc * jnp.where(lo, inv_e, inv_o)
            mix_buf[rows, D_GMLP + slab * LANES:D_GMLP + (slab + 1) * LANES] = out.astype(_BF16)
    last = slice(tokens, tokens + BLK)
    kd_buf[:, 0:BLK, :] = kd_buf[:, last, :]
    vw_buf[:, 0:BLK, :] = vw_buf[:, last, :]
    yield

    y_buf[...] = alpha * x + _dot(mix_buf[...], w_out_ref[...])


def _ffn_steps(y_buf, g1_ref, b1_ref, w1_ref, w2_ref, g2_ref, b2_ref, o_ref, *, alpha):
    x = _layer_norm(y_buf[...], g1_ref[...], b1_ref[...])
    xb = x.astype(_BF16)
    acc = alpha * x
    yield
    for j in range(w1_ref.shape[1] // FFN_CHUNK):
        cols = slice(j * FFN_CHUNK, (j + 1) * FFN_CHUNK)
        h = jnp.maximum(_dot(xb, w1_ref[:, cols]), 0.0)
        acc = acc + _dot((h * h).astype(_BF16), w2_ref[cols, :])
        yield
    o_ref[...] = _layer_norm(acc, g2_ref[...], b2_ref[...])


def _run(steps, count=None):
    if count is None:
        for _ in steps:
            pass
    else:
        for _ in range(count):
            next(steps)


def _stage_rows(shape):
    rows = shape[0]
    while rows * shape[1] > STAGE_ELEMS:
        rows //= 2
    return rows


def _stage_weights(pairs, sem):
    stage_shapes = sorted({(_stage_rows(src.shape), src.shape[1]) for src, _ in pairs})

    def body(*stages):
        jobs = []
        for src, dst in pairs:
            rows = _stage_rows(src.shape)
            stage = stages[stage_shapes.index((rows, src.shape[1]))]
            for r0 in range(0, src.shape[0], rows):
                jobs.append((src.at[pl.ds(r0, rows), :], stage, dst, r0, rows))
        copies = [pltpu.make_async_copy(chunk, stage.at[i % 2], sem.at[i % 2])
                  for i, (chunk, stage, _, _, _) in enumerate(jobs)]
        copies[0].start()
        for i, (_, stage, dst, r0, rows) in enumerate(jobs):
            if i + 1 < len(jobs):
                copies[i + 1].start()
            copies[i].wait()
            dst[r0:r0 + rows, :] = stage[i % 2].astype(_BF16)

    pl.run_scoped(body, *[pltpu.VMEM((2,) + s, _F32) for s in stage_shapes])


def _layer_kernel(sinks_ref, x_ref, pos_ref, invf_ref, vg_ref, vb_ref, wsp_ref, bsp_ref, g1_ref,
                  b1_ref, g2_ref, b2_ref, w_in_hbm, w_out_hbm, w1_hbm, w2_hbm, o_ref,
                  y_buf, kd_buf, vw_buf, mix_buf, w_in_ref, w_out_ref, w1_ref, w2_ref, stage_sem,
                  *, alpha, n_tiles, tiles_per_seq):
    j = pl.program_id(0)
    seq_start = (j % tiles_per_seq) == 0

    @pl.when(seq_start)
    def _():
        kd_buf[:, 0:BLK, :] = jnp.zeros((N_KV_HEADS, BLK, LANES), _BF16)
        vw_buf[:, 0:BLK, :] = jnp.zeros((2 * N_KV_HEADS, BLK, LANES), _BF16)

    mixer = functools.partial(
        _mixer_steps, seq_start, sinks_ref, x_ref, pos_ref, invf_ref, w_in_ref, vg_ref, vb_ref,
        wsp_ref, bsp_ref, w_out_ref, y_buf, kd_buf, vw_buf, mix_buf, alpha=alpha)
    ffn = functools.partial(_ffn_steps, y_buf, g1_ref, b1_ref, w1_ref, w2_ref, g2_ref, b2_ref,
                            o_ref, alpha=alpha)

    @pl.when(j == 0)
    def _():
        _stage_weights([(w_in_hbm, w_in_ref), (w_out_hbm, w_out_ref), (w1_hbm, w1_ref),
                        (w2_hbm, w2_ref)], stage_sem)
        _run(mixer())

    @pl.when((j > 0) & (j < n_tiles))
    def _():
        f, m = ffn(), mixer()
        _run(f, 1)
        _run(m, 1)
        _run(f, 2)
        _run(m, 1)
        _run(f, 1)
        _run(m, 1)
        _run(f, 1)
        _run(m)
        _run(f)

    @pl.when(j == n_tiles)
    def _():
        _run(ffn())


def _resident(shape):
    return pl.BlockSpec(shape, lambda *_: (0,) * len(shape), pipeline_mode=pl.Buffered(1))


def _layer(x, pos, invf, sinks, w_in, vg, vb, wsp, bsp, w_out, g1, b1, w1, w2, g2, b2, *, alpha,
           seq):
    n_tok, d_model = x.shape
    tokens = TILE_TOKENS
    assert seq % tokens == 0 and tokens % BLK == 0 and w1.shape[1] % FFN_CHUNK == 0
    n_tiles = n_tok // tokens
    kernel = functools.partial(_layer_kernel, alpha=alpha, n_tiles=n_tiles,
                               tiles_per_seq=seq // tokens)
    in_tile = lambda width: pl.BlockSpec(
        (tokens, width), lambda j: (jnp.minimum(j, n_tiles - 1), 0))
    out_tile = pl.BlockSpec((tokens, d_model), lambda j: (jnp.maximum(j - 1, 0), 0))
    resident = [invf, vg, vb, wsp, bsp, g1, b1, g2, b2]
    weights = [w_in, w_out, w1, w2]
    return pl.pallas_call(
        kernel,
        grid=(n_tiles + 1,),
        in_specs=[pl.BlockSpec(memory_space=pltpu.SMEM), in_tile(d_model), in_tile(1)]
        + [_resident(a.shape) for a in resident]
        + [pl.BlockSpec(memory_space=pl.ANY) for _ in weights],
        out_specs=out_tile,
        out_shape=jax.ShapeDtypeStruct(x.shape, x.dtype),
        scratch_shapes=[
            pltpu.VMEM((tokens, d_model), _F32),
            pltpu.VMEM((N_KV_HEADS, tokens + BLK, LANES), _BF16),
            pltpu.VMEM((2 * N_KV_HEADS, tokens + BLK, LANES), _BF16),
            pltpu.VMEM((tokens, D_GMLP + D_ATTN), _BF16),
        ] + [pltpu.VMEM(w.shape, _BF16) for w in weights] + [pltpu.SemaphoreType.DMA((2,))],
        compiler_params=pltpu.CompilerParams(
            dimension_semantics=("arbitrary",), vmem_limit_bytes=VMEM_LIMIT_BYTES),
        name="layer",
    )(sinks, x, pos, *resident, *weights)


def kernel(x, positions, w_in, v_ln_g, v_ln_b, w_spatial, b_spatial, sinks, w_out, ln1_g, ln1_b,
           w_ff1, w_ff2, ln2_g, ln2_b):
    batch, seq, d_model = x.shape
    depth = w_in.shape[0]
    alpha = (2.0 * depth) ** 0.25
    inv_freq = ROPE_THETA ** (-jnp.arange(0, HEAD_DIM, 2, dtype=_F32) / HEAD_DIM)
    invf = jnp.tile(inv_freq, LANES // HALF)[None, :]
    pos = positions.reshape(batch * seq, 1)
    x = x.reshape(batch * seq, d_model)
    row = lambda a: a[None, :]
    for l in range(depth):
        bsp = jnp.repeat(b_spatial[l].T, HEAD_DIM, axis=1)
        x = _layer(x, pos, invf, sinks[l], w_in[l], row(v_ln_g[l]), row(v_ln_b[l]),
                   w_spatial[l], bsp, w_out[l], row(ln1_g[l]), row(ln1_b[l]),
                   w_ff1[l], w_ff2[l], row(ln2_g[l]), row(ln2_b[l]), alpha=alpha, seq=seq)
    return x.reshape(batch, seq, d_model)
```

```python
import functools

import jax
import jax.numpy as jnp
from jax import lax
from jax.experimental import pallas as pl
from jax.experimental.pallas import tpu as pltpu

HEAD_DIM = 64
N_GMLP_HEADS = 8
D_GMLP = N_GMLP_HEADS * HEAD_DIM
N_Q_HEADS = 8
N_KV_HEADS = 2
GQA_GROUP = N_Q_HEADS // N_KV_HEADS
D_ATTN = N_Q_HEADS * HEAD_DIM
D_KV = N_KV_HEADS * HEAD_DIM
BLK = 128
ROPE_THETA = 10000.0
LN_EPS = 1e-5
NEG_INF = -1e30
LANES = 128
HALF = HEAD_DIM // 2

TILE_TOKENS = 512
FFN_CHUNK = 1024
STAGE_ROWS, STAGE_COLS = 1024, 256
STAGE_SLOTS = 6
VMEM_LIMIT_BYTES = 56 * 1024 * 1024

_BF16 = jnp.bfloat16
_F32 = jnp.float32


def _dot(a, b):
    return jnp.dot(a, b, preferred_element_type=_F32)


def _dot_nt(a, b):
    return lax.dot_general(a, b, (((1,), (1,)), ((), ())), preferred_element_type=_F32)


def _layer_norm(v, g, b):
    mu = jnp.mean(v, axis=-1, keepdims=True)
    c = v - mu
    var = jnp.mean(c * c, axis=-1, keepdims=True)
    return c * lax.rsqrt(var + LN_EPS) * g + b


def _mixer_steps(seq_start, sinks_ref, x_ref, pos_ref, invf_ref, w_in_ref, vg_ref, vb_ref, wsp_ref,
                 bsp_ref, w_out_ref, y_buf, kd_buf, vw_buf, mix_buf, *, alpha):
    tokens = x_ref.shape[0]
    n_blk = tokens // BLK
    n_slab_g = D_GMLP // LANES
    n_slab_a = D_ATTN // LANES
    q0 = 2 * D_GMLP
    k0 = q0 + D_ATTN

    x = x_ref[...]
    xb = x.astype(_BF16)
    u_raw = _dot(xb, w_in_ref[:, 0:D_GMLP])
    vg_raw = _dot(xb, w_in_ref[:, D_GMLP:2 * D_GMLP])
    qkv = _dot(xb, w_in_ref[:, q0:k0 + 2 * D_KV])
    q_raw = [qkv[:, p * LANES:(p + 1) * LANES] for p in range(n_slab_a)]
    k_raw = qkv[:, D_ATTN:D_ATTN + D_KV]
    v = qkv[:, D_ATTN + D_KV:D_ATTN + 2 * D_KV]
    yield

    lane = lax.broadcasted_iota(jnp.int32, (1, LANES), 1)
    lo = lane < HEAD_DIM

    u = jax.nn.gelu(u_raw)
    vn = _layer_norm(jax.nn.gelu(vg_raw), vg_ref[...], vb_ref[...])
    lo_g = (lax.broadcasted_iota(jnp.int32, (1, D_GMLP), 1) & HEAD_DIM) == 0
    vn_top = jnp.where(lo_g, vn, 0.0).astype(_BF16)
    vn_bot = jnp.where(lo_g, 0.0, vn).astype(_BF16)
    ti = lax.broadcasted_iota(jnp.int32, (BLK, BLK), 0)
    si = lax.broadcasted_iota(jnp.int32, (BLK, BLK), 1)
    causal = si <= ti

    pos = pos_ref[...].astype(_F32)
    invf = invf_ref[...]
    n_grp = LANES // HALF
    quarter = tokens // n_grp
    grp = lane // HALF
    ang = None
    for a in range(n_grp):
        term = pos[a * quarter:(a + 1) * quarter] * jnp.where(grp == a, invf, 0.0)
        ang = term if ang is None else ang + term
    cos_packed, sin_packed = jnp.cos(ang), jnp.sin(ang)

    def spread(packed):
        parts = []
        for a in range(n_grp):
            one = jnp.where(grp == a, packed, 0.0)
            two = one + pltpu.roll(one, 2 * HALF, axis=1)
            parts.append(two + pltpu.roll(two, HALF, axis=1))
        return jnp.concatenate(parts, axis=0)

    cos = spread(cos_packed)
    first_half = (lane & HALF) == 0
    sin_signed = jnp.where(first_half, -1.0, 1.0) * spread(sin_packed)

    def rope(t):
        swapped = jnp.where(first_half, pltpu.roll(t, LANES - HALF, axis=1),
                            pltpu.roll(t, HALF, axis=1))
        return t * cos + swapped * sin_signed

    scale = HEAD_DIM ** -0.5
    q_even, q_odd = [], []
    for p in range(n_slab_a):
        qs = rope(q_raw[p]) * scale
        q_even.append(jnp.where(lo, qs, 0.0).astype(_BF16))
        q_odd.append(jnp.where(lo, 0.0, qs).astype(_BF16))
    kr = rope(k_raw)
    kr_sw = pltpu.roll(kr, HEAD_DIM, axis=1)
    cur = slice(BLK, BLK + tokens)
    kd_buf[0, cur, :] = jnp.where(lo, kr, kr_sw).astype(_BF16)
    kd_buf[1, cur, :] = jnp.where(lo, kr_sw, kr).astype(_BF16)
    v_sw = pltpu.roll(v, HEAD_DIM, axis=1)
    vw_buf[0, cur, :] = jnp.where(lo, v, 0.0).astype(_BF16)
    vw_buf[1, cur, :] = jnp.where(lo, 0.0, v_sw).astype(_BF16)
    vw_buf[2, cur, :] = jnp.where(lo, v_sw, 0.0).astype(_BF16)
    vw_buf[3, cur, :] = jnp.where(lo, 0.0, v).astype(_BF16)

    for p in range(n_slab_g):
        w_pair = jnp.concatenate(
            [jnp.where(causal, wsp_ref[2 * p], 0.0), jnp.where(causal, wsp_ref[2 * p + 1], 0.0)],
            axis=1).astype(_BF16)
        bias = bsp_ref[:, p * LANES:(p + 1) * LANES]
        for c in range(n_blk):
            rows = slice(c * BLK, (c + 1) * BLK)
            cols = slice(p * LANES, (p + 1) * LANES)
            rhs = jnp.concatenate([vn_top[rows, cols], vn_bot[rows, cols]], axis=0)
            mixed = _dot(w_pair, rhs) + bias
            mix_buf[rows, cols] = (u[rows, cols] * mixed).astype(_BF16)

    from_prev = si > ti
    no_prev = from_prev & (si < jnp.where(seq_start, BLK, 0))
    probs = {}
    for n in range(n_blk):
        rows = slice(n * BLK, (n + 1) * BLK)
        kv_rows = slice(n * BLK, (n + 2) * BLK)
        for g in range(N_KV_HEADS):
            heads = range(GQA_GROUP * g, GQA_GROUP * (g + 1))
            q_stack = jnp.concatenate(
                [(q_odd if h % 2 else q_even)[h // 2][rows] for h in heads], axis=0)
            scores = _dot_nt(q_stack, kd_buf[g, kv_rows, :])
            for hl, h in enumerate(heads):
                sc = scores[hl * BLK:(hl + 1) * BLK]
                sh = jnp.where(from_prev, sc[:, 0:BLK], sc[:, BLK:2 * BLK])
                if n == 0:
                    sh = jnp.where(no_prev, NEG_INF, sh)
                sink = sinks_ref[h]
                m = jnp.maximum(jnp.max(sh, axis=-1, keepdims=True), sink)
                pr = jnp.exp(sh - m)
                l = jnp.sum(pr, axis=-1, keepdims=True) + jnp.exp(sink - m)
                p_band = jnp.concatenate(
                    [jnp.where(from_prev, pr, 0.0), jnp.where(from_prev, 0.0, pr)], axis=1)
                probs[n, h] = (p_band.astype(_BF16), 1.0 / l)
    yield

    for n in range(n_blk):
        rows = slice(n * BLK, (n + 1) * BLK)
        kv_rows = slice(n * BLK, (n + 2) * BLK)
        for slab in range(n_slab_a):
            g = (2 * slab) // GQA_GROUP
            (p_e, inv_e), (p_o, inv_o) = probs[n, 2 * slab], probs[n, 2 * slab + 1]
            acc = (_dot(p_e, vw_buf[2 * g, kv_rows, :]) + _dot(p_o, vw_buf[2 * g + 1, kv_rows, :]))
            out = acc * jnp.where(lo, inv_e, inv_o)
            mix_buf[rows, D_GMLP + slab * LANES:D_GMLP + (slab + 1) * LANES] = out.astype(_BF16)
    last = slice(tokens, tokens + BLK)
    kd_buf[:, 0:BLK, :] = kd_buf[:, last, :]
    vw_buf[:, 0:BLK, :] = vw_buf[:, last, :]
    yield

    y_buf[...] = alpha * x + _dot(mix_buf[...], w_out_ref[...])


def _ffn_steps(y_buf, g1_ref, b1_ref, w1_ref, w2_ref, g2_ref, b2_ref, o_ref, *, alpha):
    x = _layer_norm(y_buf[...], g1_ref[...], b1_ref[...])
    xb = x.astype(_BF16)
    acc = alpha * x
    yield
    for j in range(w1_ref.shape[1] // FFN_CHUNK):
        cols = slice(j * FFN_CHUNK, (j + 1) * FFN_CHUNK)
        h = jnp.maximum(_dot(xb, w1_ref[:, cols]), 0.0)
        acc = acc + _dot((h * h).astype(_BF16), w2_ref[cols, :])
        yield
    o_ref[...] = _layer_norm(acc, g2_ref[...], b2_ref[...])


def _run(steps, count=None):
    if count is None:
        for _ in steps:
            pass
    else:
        for _ in range(count):
            next(steps)


def _stage_weights(pairs, sem):
    def body(stage):
        jobs = []
        for src, dst in pairs:
            assert src.shape[0] % STAGE_ROWS == 0 and src.shape[1] % STAGE_COLS == 0
            for r0 in range(0, src.shape[0], STAGE_ROWS):
                for c0 in range(0, src.shape[1], STAGE_COLS):
                    jobs.append((src.at[pl.ds(r0, STAGE_ROWS), pl.ds(c0, STAGE_COLS)], dst, r0, c0))
        copies = [pltpu.make_async_copy(piece, stage.at[i % STAGE_SLOTS], sem.at[i % STAGE_SLOTS])
                  for i, (piece, _, _, _) in enumerate(jobs)]
        ahead = STAGE_SLOTS - 1
        for copy in copies[:ahead]:
            copy.start()
        for i, (_, dst, r0, c0) in enumerate(jobs):
            if i + ahead < len(jobs):
                copies[i + ahead].start()
            copies[i].wait()
            dst[r0:r0 + STAGE_ROWS, c0:c0 + STAGE_COLS] = stage[i % STAGE_SLOTS].astype(_BF16)

    pl.run_scoped(body, pltpu.VMEM((STAGE_SLOTS, STAGE_ROWS, STAGE_COLS), _F32))


def _layer_kernel(sinks_ref, x_ref, pos_ref, invf_ref, vg_ref, vb_ref, wsp_ref, bsp_ref, g1_ref,
                  b1_ref, g2_ref, b2_ref, w_in_hbm, w_out_hbm, w1_hbm, w2_hbm, o_ref,
                  y_buf, kd_buf, vw_buf, mix_buf, w_in_ref, w_out_ref, w1_ref, w2_ref, stage_sem,
                  *, alpha, n_tiles, tiles_per_seq):
    j = pl.program_id(0)
    seq_start = (j % tiles_per_seq) == 0

    @pl.when(seq_start)
    def _():
        kd_buf[:, 0:BLK, :] = jnp.zeros((N_KV_HEADS, BLK, LANES), _BF16)
        vw_buf[:, 0:BLK, :] = jnp.zeros((2 * N_KV_HEADS, BLK, LANES), _BF16)

    mixer = functools.partial(
        _mixer_steps, seq_start, sinks_ref, x_ref, pos_ref, invf_ref, w_in_ref, vg_ref, vb_ref,
        wsp_ref, bsp_ref, w_out_ref, y_buf, kd_buf, vw_buf, mix_buf, alpha=alpha)
    ffn = functools.partial(_ffn_steps, y_buf, g1_ref, b1_ref, w1_ref, w2_ref, g2_ref, b2_ref,
                            o_ref, alpha=alpha)

    @pl.when(j == 0)
    def _():
        _stage_weights([(w_in_hbm, w_in_ref), (w_out_hbm, w_out_ref), (w1_hbm, w1_ref),
                        (w2_hbm, w2_ref)], stage_sem)
        _run(mixer())

    @pl.when((j > 0) & (j < n_tiles))
    def _():
        f, m = ffn(), mixer()
        _run(f, 1)
        _run(m, 1)
        _run(f, 2)
        _run(m, 1)
        _run(f, 1)
        _run(m, 1)
        _run(f, 1)
        _run(m)
        _run(f)

    @pl.when(j == n_tiles)
    def _():
        _run(ffn())


def _resident(shape):
    return pl.BlockSpec(shape, lambda *_: (0,) * len(shape), pipeline_mode=pl.Buffered(1))


def _layer(x, pos, invf, sinks, w_in, vg, vb, wsp, bsp, w_out, g1, b1, w1, w2, g2, b2, *, alpha,
           seq):
    n_tok, d_model = x.shape
    tokens = TILE_TOKENS
    assert seq % tokens == 0 and tokens % BLK == 0 and w1.shape[1] % FFN_CHUNK == 0
    n_tiles = n_tok // tokens
    kernel = functools.partial(_layer_kernel, alpha=alpha, n_tiles=n_tiles,
                               tiles_per_seq=seq // tokens)
    in_tile = lambda width: pl.BlockSpec(
        (tokens, width), lambda j: (jnp.minimum(j, n_tiles - 1), 0))
    out_tile = pl.BlockSpec((tokens, d_model), lambda j: (jnp.maximum(j - 1, 0), 0))
    resident = [invf, vg, vb, wsp, bsp, g1, b1, g2, b2]
    weights = [w_in, w_out, w1, w2]
    return pl.pallas_call(
        kernel,
        grid=(n_tiles + 1,),
        in_specs=[pl.BlockSpec(memory_space=pltpu.SMEM), in_tile(d_model), in_tile(1)]
        + [_resident(a.shape) for a in resident]
        + [pl.BlockSpec(memory_space=pl.ANY) for _ in weights],
        out_specs=out_tile,
        out_shape=jax.ShapeDtypeStruct(x.shape, x.dtype),
        scratch_shapes=[
            pltpu.VMEM((tokens, d_model), _F32),
            pltpu.VMEM((N_KV_HEADS, tokens + BLK, LANES), _BF16),
            pltpu.VMEM((2 * N_KV_HEADS, tokens + BLK, LANES), _BF16),
            pltpu.VMEM((tokens, D_GMLP + D_ATTN), _BF16),
        ] + [pltpu.VMEM(w.shape, _BF16) for w in weights] + [pltpu.SemaphoreType.DMA((STAGE_SLOTS,))],
        compiler_params=pltpu.CompilerParams(
            dimension_semantics=("arbitrary",), vmem_limit_bytes=VMEM_LIMIT_BYTES),
        name="layer",
    )(sinks, x, pos, *resident, *weights)


def kernel(x, positions, w_in, v_ln_g, v_ln_b, w_spatial, b_spatial, sinks, w_out, ln1_g, ln1_b,
           w_ff1, w_ff2, ln2_g, ln2_b):
    batch, seq, d_model = x.shape
    depth = w_in.shape[0]
    alpha = (2.0 * depth) ** 0.25
    inv_freq = ROPE_THETA ** (-jnp.arange(0, HEAD_DIM, 2, dtype=_F32) / HEAD_DIM)
    invf = jnp.tile(inv_freq, LANES // HALF)[None, :]
    pos = positions.reshape(batch * seq, 1)
    x = x.reshape(batch * seq, d_model)
    row = lambda a: a[None, :]
    for l in range(depth):
        bsp = jnp.repeat(b_spatial[l].T, HEAD_DIM, axis=1)
        x = _layer(x, pos, invf, sinks[l], w_in[l], row(v_ln_g[l]), row(v_ln_b[l]),
                   w_spatial[l], bsp, w_out[l], row(ln1_g[l]), row(ln1_b[l]),
                   w_ff1[l], w_ff2[l], row(ln2_g[l]), row(ln2_b[l]), alpha=alpha, seq=seq)
    return x.reshape(batch, seq, d_model)
```

```python
import functools

import jax
import jax.numpy as jnp
from jax import lax
from jax.experimental import pallas as pl
from jax.experimental.pallas import tpu as pltpu

HEAD_DIM = 64
N_GMLP_HEADS = 8
D_GMLP = N_GMLP_HEADS * HEAD_DIM
N_Q_HEADS = 8
N_KV_HEADS = 2
GQA_GROUP = N_Q_HEADS // N_KV_HEADS
D_ATTN = N_Q_HEADS * HEAD_DIM
D_KV = N_KV_HEADS * HEAD_DIM
BLK = 128
ROPE_THETA = 10000.0
LN_EPS = 1e-5
NEG_INF = -1e30
LANES = 128
HALF = HEAD_DIM // 2

TILE_TOKENS = 512
FFN_CHUNK = 1024
STAGE_ROWS, STAGE_COLS = 1024, 256
STAGE_SLOTS = 6
VMEM_LIMIT_BYTES = 56 * 1024 * 1024

_BF16 = jnp.bfloat16
_F32 = jnp.float32


def _dot(a, b):
    return jnp.dot(a, b, preferred_element_type=_F32)


def _dot_nt(a, b):
    return lax.dot_general(a, b, (((1,), (1,)), ((), ())), preferred_element_type=_F32)


def _layer_norm(v, g, b):
    mu = jnp.mean(v, axis=-1, keepdims=True)
    c = v - mu
    var = jnp.mean(c * c, axis=-1, keepdims=True)
    return c * lax.rsqrt(var + LN_EPS) * g + b


def _mixer_steps(seq_start, sinks_ref, x_ref, pos_ref, invf_ref, w_in_ref, vg_ref, vb_ref, wsp_ref,
                 bsp_ref, w_out_ref, y_buf, kd_buf, vw_buf, mix_buf, *, alpha):
    tokens = x_ref.shape[0]
    n_blk = tokens // BLK
    n_slab_g = D_GMLP // LANES
    n_slab_a = D_ATTN // LANES
    q0 = 2 * D_GMLP
    k0 = q0 + D_ATTN

    x = x_ref[...]
    xb = x.astype(_BF16)
    u_raw = _dot(xb, w_in_ref[:, 0:D_GMLP])
    vg_raw = _dot(xb, w_in_ref[:, D_GMLP:2 * D_GMLP])
    qkv = _dot(xb, w_in_ref[:, q0:k0 + 2 * D_KV])
    q_raw = [qkv[:, p * LANES:(p + 1) * LANES] for p in range(n_slab_a)]
    k_raw = qkv[:, D_ATTN:D_ATTN + D_KV]
    v = qkv[:, D_ATTN + D_KV:D_ATTN + 2 * D_KV]
    yield

    lane = lax.broadcasted_iota(jnp.int32, (1, LANES), 1)
    lo = lane < HEAD_DIM

    u = jax.nn.gelu(u_raw)
    vn = _layer_norm(jax.nn.gelu(vg_raw), vg_ref[...], vb_ref[...])
    lo_g = (lax.broadcasted_iota(jnp.int32, (1, D_GMLP), 1) & HEAD_DIM) == 0
    vn_top = jnp.where(lo_g, vn, 0.0).astype(_BF16)
    vn_bot = jnp.where(lo_g, 0.0, vn).astype(_BF16)
    ti = lax.broadcasted_iota(jnp.int32, (BLK, BLK), 0)
    si = lax.broadcasted_iota(jnp.int32, (BLK, BLK), 1)
    causal = si <= ti

    def column(row):
        return jnp.sum(jnp.where(si == ti, row, 0.0), axis=1, keepdims=True)

    pos = pos_ref[...].astype(_F32)
    invf = invf_ref[...]
    n_grp = LANES // HALF
    quarter = tokens // n_grp
    assert pos.shape == (n_grp, quarter) and quarter == BLK
    grp = lane // HALF
    ang = None
    for a in range(n_grp):
        term = column(pos[a:a + 1, :]) * jnp.where(grp == a, invf, 0.0)
        ang = term if ang is None else ang + term
    cos_packed, sin_packed = jnp.cos(ang), jnp.sin(ang)

    def spread(packed):
        parts = []
        for a in range(n_grp):
            one = jnp.where(grp == a, packed, 0.0)
            two = one + pltpu.roll(one, 2 * HALF, axis=1)
            parts.append(two + pltpu.roll(two, HALF, axis=1))
        return jnp.concatenate(parts, axis=0)

    cos = spread(cos_packed)
    first_half = (lane & HALF) == 0
    sin_signed = jnp.where(first_half, -1.0, 1.0) * spread(sin_packed)

    def rope(t):
        swapped = jnp.where(first_half, pltpu.roll(t, LANES - HALF, axis=1),
                            pltpu.roll(t, HALF, axis=1))
        return t * cos + swapped * sin_signed

    scale = HEAD_DIM ** -0.5
    q_even, q_odd = [], []
    for p in range(n_slab_a):
        qs = rope(q_raw[p]) * scale
        q_even.append(jnp.where(lo, qs, 0.0).astype(_BF16))
        q_odd.append(jnp.where(lo, 0.0, qs).astype(_BF16))
    kr = rope(k_raw)
    kr_sw = pltpu.roll(kr, HEAD_DIM, axis=1)
    cur = slice(BLK, BLK + tokens)
    kd_buf[0, cur, :] = jnp.where(lo, kr, kr_sw).astype(_BF16)
    kd_buf[1, cur, :] = jnp.where(lo, kr_sw, kr).astype(_BF16)
    v_sw = pltpu.roll(v, HEAD_DIM, axis=1)
    vw_buf[0, cur, :] = jnp.where(lo, v, 0.0).astype(_BF16)
    vw_buf[1, cur, :] = jnp.where(lo, 0.0, v_sw).astype(_BF16)
    vw_buf[2, cur, :] = jnp.where(lo, v_sw, 0.0).astype(_BF16)
    vw_buf[3, cur, :] = jnp.where(lo, 0.0, v).astype(_BF16)

    for p in range(n_slab_g):
        w_pair = jnp.concatenate(
            [jnp.where(causal, wsp_ref[2 * p], 0.0), jnp.where(causal, wsp_ref[2 * p + 1], 0.0)],
            axis=1).astype(_BF16)
        bias = jnp.where(lo, column(bsp_ref[2 * p:2 * p + 1, :]),
                         column(bsp_ref[2 * p + 1:2 * p + 2, :]))
        for c in range(n_blk):
            rows = slice(c * BLK, (c + 1) * BLK)
            cols = slice(p * LANES, (p + 1) * LANES)
            rhs = jnp.concatenate([vn_top[rows, cols], vn_bot[rows, cols]], axis=0)
            mixed = _dot(w_pair, rhs) + bias
            mix_buf[rows, cols] = (u[rows, cols] * mixed).astype(_BF16)

    from_prev = si > ti
    no_prev = from_prev & (si < jnp.where(seq_start, BLK, 0))
    probs = {}
    for n in range(n_blk):
        rows = slice(n * BLK, (n + 1) * BLK)
        kv_rows = slice(n * BLK, (n + 2) * BLK)
        for g in range(N_KV_HEADS):
            heads = range(GQA_GROUP * g, GQA_GROUP * (g + 1))
            q_stack = jnp.concatenate(
                [(q_odd if h % 2 else q_even)[h // 2][rows] for h in heads], axis=0)
            scores = _dot_nt(q_stack, kd_buf[g, kv_rows, :])
            for hl, h in enumerate(heads):
                sc = scores[hl * BLK:(hl + 1) * BLK]
                sh = jnp.where(from_prev, sc[:, 0:BLK], sc[:, BLK:2 * BLK])
                if n == 0:
                    sh = jnp.where(no_prev, NEG_INF, sh)
                sink = sinks_ref[h]
                m = jnp.maximum(jnp.max(sh, axis=-1, keepdims=True), sink)
                pr = jnp.exp(sh - m)
                l = jnp.sum(pr, axis=-1, keepdims=True) + jnp.exp(sink - m)
                p_band = jnp.concatenate(
                    [jnp.where(from_prev, pr, 0.0), jnp.where(from_prev, 0.0, pr)], axis=1)
                probs[n, h] = (p_band.astype(_BF16), 1.0 / l)
    yield

    for n in range(n_blk):
        rows = slice(n * BLK, (n + 1) * BLK)
        kv_rows = slice(n * BLK, (n + 2) * BLK)
        for slab in range(n_slab_a):
            g = (2 * slab) // GQA_GROUP
            (p_e, inv_e), (p_o, inv_o) = probs[n, 2 * slab], probs[n, 2 * slab + 1]
            acc = (_dot(p_e, vw_buf[2 * g, kv_rows, :]) + _dot(p_o, vw_buf[2 * g + 1, kv_rows, :]))
            out = acc * jnp.where(lo, inv_e, inv_o)
            mix_buf[rows, D_GMLP + slab * LANES:D_GMLP + (slab + 1) * LANES] = out.astype(_BF16)
    last = slice(tokens, tokens + BLK)
    kd_buf[:, 0:BLK, :] = kd_buf[:, last, :]
    vw_buf[:, 0:BLK, :] = vw_buf[:, last, :]
    yield

    y_buf[...] = alpha * x + _dot(mix_buf[...], w_out_ref[...])


def _ffn_steps(y_buf, g1_ref, b1_ref, w1_ref, w2_ref, g2_ref, b2_ref, o_ref, *, alpha):
    x = _layer_norm(y_buf[...], g1_ref[...], b1_ref[...])
    xb = x.astype(_BF16)
    acc = alpha * x
    yield
    for j in range(w1_ref.shape[1] // FFN_CHUNK):
        cols = slice(j * FFN_CHUNK, (j + 1) * FFN_CHUNK)
        h = jnp.maximum(_dot(xb, w1_ref[:, cols]), 0.0)
        acc = acc + _dot((h * h).astype(_BF16), w2_ref[cols, :])
        yield
    o_ref[...] = _layer_norm(acc, g2_ref[...], b2_ref[...])


def _run(steps, count=None):
    if count is None:
        for _ in steps:
            pass
    else:
        for _ in range(count):
            next(steps)


def _stage_weights(pairs, sem):
    def body(stage):
        jobs = []
        for src, dst in pairs:
            assert src.shape[0] % STAGE_ROWS == 0 and src.shape[1] % STAGE_COLS == 0
            for r0 in range(0, src.shape[0], STAGE_ROWS):
                for c0 in range(0, src.shape[1], STAGE_COLS):
                    jobs.append((src.at[pl.ds(r0, STAGE_ROWS), pl.ds(c0, STAGE_COLS)], dst, r0, c0))
        copies = [pltpu.make_async_copy(piece, stage.at[i % STAGE_SLOTS], sem.at[i % STAGE_SLOTS])
                  for i, (piece, _, _, _) in enumerate(jobs)]
        ahead = STAGE_SLOTS - 1
        for copy in copies[:ahead]:
            copy.start()
        for i, (_, dst, r0, c0) in enumerate(jobs):
            if i + ahead < len(jobs):
                copies[i + ahead].start()
            copies[i].wait()
            dst[r0:r0 + STAGE_ROWS, c0:c0 + STAGE_COLS] = stage[i % STAGE_SLOTS].astype(_BF16)

    pl.run_scoped(body, pltpu.VMEM((STAGE_SLOTS, STAGE_ROWS, STAGE_COLS), _F32))


def _layer_kernel(sinks_ref, x_ref, pos_ref, invf_ref, vg_ref, vb_ref, wsp_ref, bsp_ref, g1_ref,
                  b1_ref, g2_ref, b2_ref, w_in_hbm, w_out_hbm, w1_hbm, w2_hbm, o_ref,
                  y_buf, kd_buf, vw_buf, mix_buf, w_in_ref, w_out_ref, w1_ref, w2_ref, stage_sem,
                  *, alpha, n_tiles, tiles_per_seq):
    j = pl.program_id(0)
    seq_start = (j % tiles_per_seq) == 0

    @pl.when(seq_start)
    def _():
        kd_buf[:, 0:BLK, :] = jnp.zeros((N_KV_HEADS, BLK, LANES), _BF16)
        vw_buf[:, 0:BLK, :] = jnp.zeros((2 * N_KV_HEADS, BLK, LANES), _BF16)

    mixer = functools.partial(
        _mixer_steps, seq_start, sinks_ref, x_ref, pos_ref, invf_ref, w_in_ref, vg_ref, vb_ref,
        wsp_ref, bsp_ref, w_out_ref, y_buf, kd_buf, vw_buf, mix_buf, alpha=alpha)
    ffn = functools.partial(_ffn_steps, y_buf, g1_ref, b1_ref, w1_ref, w2_ref, g2_ref, b2_ref,
                            o_ref, alpha=alpha)

    @pl.when(j == 0)
    def _():
        _stage_weights([(w_in_hbm, w_in_ref), (w_out_hbm, w_out_ref), (w1_hbm, w1_ref),
                        (w2_hbm, w2_ref)], stage_sem)
        _run(mixer())

    @pl.when((j > 0) & (j < n_tiles))
    def _():
        f, m = ffn(), mixer()
        _run(f, 1)
        _run(m, 1)
        _run(f, 2)
        _run(m, 1)
        _run(f, 1)
        _run(m, 1)
        _run(f, 1)
        _run(m)
        _run(f)

    @pl.when(j == n_tiles)
    def _():
        _run(ffn())


def _resident(shape):
    return pl.BlockSpec(shape, lambda *_: (0,) * len(shape), pipeline_mode=pl.Buffered(1))


def _layer(x, pos, invf, sinks, w_in, vg, vb, wsp, bsp, w_out, g1, b1, w1, w2, g2, b2, *, alpha,
           seq):
    n_tok, d_model = x.shape
    tokens = TILE_TOKENS
    assert seq % tokens == 0 and tokens % BLK == 0 and w1.shape[1] % FFN_CHUNK == 0
    n_tiles = n_tok // tokens
    kernel = functools.partial(_layer_kernel, alpha=alpha, n_tiles=n_tiles,
                               tiles_per_seq=seq // tokens)
    in_tile = pl.BlockSpec((tokens, d_model), lambda j: (jnp.minimum(j, n_tiles - 1), 0))
    pos_tile = pl.BlockSpec((None,) + pos.shape[1:], lambda j: (jnp.minimum(j, n_tiles - 1), 0, 0))
    out_tile = pl.BlockSpec((tokens, d_model), lambda j: (jnp.maximum(j - 1, 0), 0))
    resident = [invf, vg, vb, wsp, bsp, g1, b1, g2, b2]
    weights = [w_in, w_out, w1, w2]
    return pl.pallas_call(
        kernel,
        grid=(n_tiles + 1,),
        in_specs=[pl.BlockSpec(memory_space=pltpu.SMEM), in_tile, pos_tile]
        + [_resident(a.shape) for a in resident]
        + [pl.BlockSpec(memory_space=pl.ANY) for _ in weights],
        out_specs=out_tile,
        out_shape=jax.ShapeDtypeStruct(x.shape, x.dtype),
        scratch_shapes=[
            pltpu.VMEM((tokens, d_model), _F32),
            pltpu.VMEM((N_KV_HEADS, tokens + BLK, LANES), _BF16),
            pltpu.VMEM((2 * N_KV_HEADS, tokens + BLK, LANES), _BF16),
            pltpu.VMEM((tokens, D_GMLP + D_ATTN), _BF16),
        ] + [pltpu.VMEM(w.shape, _BF16) for w in weights] + [pltpu.SemaphoreType.DMA((STAGE_SLOTS,))],
        compiler_params=pltpu.CompilerParams(
            dimension_semantics=("arbitrary",), vmem_limit_bytes=VMEM_LIMIT_BYTES),
        name="layer",
    )(sinks, x, pos, *resident, *weights)


def kernel(x, positions, w_in, v_ln_g, v_ln_b, w_spatial, b_spatial, sinks, w_out, ln1_g, ln1_b,
           w_ff1, w_ff2, ln2_g, ln2_b):
    batch, seq, d_model = x.shape
    depth = w_in.shape[0]
    alpha = (2.0 * depth) ** 0.25
    inv_freq = ROPE_THETA ** (-jnp.arange(0, HEAD_DIM, 2, dtype=_F32) / HEAD_DIM)
    invf = jnp.tile(inv_freq, LANES // HALF)[None, :]
    pos = positions.reshape(batch * seq // TILE_TOKENS, LANES // HALF, BLK)
    x = x.reshape(batch * seq, d_model)
    row = lambda a: a[None, :]
    for l in range(depth):
        x = _layer(x, pos, invf, sinks[l], w_in[l], row(v_ln_g[l]), row(v_ln_b[l]),
                   w_spatial[l], b_spatial[l], w_out[l], row(ln1_g[l]), row(ln1_b[l]),
                   w_ff1[l], w_ff2[l], row(ln2_g[l]), row(ln2_b[l]), alpha=alpha, seq=seq)
    return x.reshape(batch, seq, d_model)
```

```python
import functools

import jax
import jax.numpy as jnp
from jax import lax
from jax.experimental import pallas as pl
from jax.experimental.pallas import tpu as pltpu

HEAD_DIM = 64
N_GMLP_HEADS = 8
D_GMLP = N_GMLP_HEADS * HEAD_DIM
N_Q_HEADS = 8
N_KV_HEADS = 2
GQA_GROUP = N_Q_HEADS // N_KV_HEADS
D_ATTN = N_Q_HEADS * HEAD_DIM
D_KV = N_KV_HEADS * HEAD_DIM
BLK = 128
ROPE_THETA = 10000.0
LN_EPS = 1e-5
NEG_INF = -1e30
LANES = 128
HALF = HEAD_DIM // 2
MXU_WIDTH = 256

TILE_TOKENS = 512
FFN_CHUNK = 1024
STAGE_ROWS, STAGE_COLS = 1024, 256
STAGE_SLOTS = 6
VMEM_LIMIT_BYTES = 56 * 1024 * 1024

_BF16 = jnp.bfloat16
_F32 = jnp.float32


def _dot(a, b):
    return jnp.dot(a, b, preferred_element_type=_F32)


def _dot_nt(a, b):
    return lax.dot_general(a, b, (((1,), (1,)), ((), ())), preferred_element_type=_F32)


def _layer_norm(v, g, b):
    mu = jnp.mean(v, axis=-1, keepdims=True)
    c = v - mu
    var = jnp.mean(c * c, axis=-1, keepdims=True)
    return c * lax.rsqrt(var + LN_EPS) * g + b


def _mixer_steps(seq_start, sinks_ref, x_ref, pos_ref, invf_ref, w_in_ref, vg_ref, vb_ref, wsp_ref,
                 bsp_ref, w_out_ref, y_buf, kd_buf, vw_buf, mix_buf, *, alpha):
    tokens = x_ref.shape[0]
    n_blk = tokens // BLK
    n_slab_g = D_GMLP // LANES
    n_slab_a = D_ATTN // LANES
    q0 = 2 * D_GMLP
    k0 = q0 + D_ATTN

    x = x_ref[...]
    xb = x.astype(_BF16)
    lane = lax.broadcasted_iota(jnp.int32, (1, LANES), 1)
    lo = lane < HEAD_DIM
    ti = lax.broadcasted_iota(jnp.int32, (BLK, BLK), 0)
    si = lax.broadcasted_iota(jnp.int32, (BLK, BLK), 1)

    def column(row):
        return jnp.sum(jnp.where(si == ti, row, 0.0), axis=1, keepdims=True)

    def project(col0, width):
        parts = []
        for c in range(col0, col0 + width, MXU_WIDTH):
            parts.append(_dot(xb, w_in_ref[:, c:c + MXU_WIDTH]))
            yield
        return parts

    qkv = yield from project(q0, D_ATTN + 2 * D_KV)
    slabs = [part[:, i * LANES:(i + 1) * LANES] for part in qkv for i in range(MXU_WIDTH // LANES)]
    q_raw, k_raw, v = slabs[:n_slab_a], slabs[n_slab_a], slabs[n_slab_a + 1]

    pos = pos_ref[...].astype(_F32)
    invf = invf_ref[...]
    n_grp = LANES // HALF
    quarter = tokens // n_grp
    assert pos.shape == (n_grp, quarter) and quarter == BLK
    grp = lane // HALF
    ang = None
    for a in range(n_grp):
        term = column(pos[a:a + 1, :]) * jnp.where(grp == a, invf, 0.0)
        ang = term if ang is None else ang + term
    cos_packed, sin_packed = jnp.cos(ang), jnp.sin(ang)

    def spread(packed):
        parts = []
        for a in range(n_grp):
            one = jnp.where(grp == a, packed, 0.0)
            two = one + pltpu.roll(one, 2 * HALF, axis=1)
            parts.append(two + pltpu.roll(two, HALF, axis=1))
        return jnp.concatenate(parts, axis=0)

    cos = spread(cos_packed)
    first_half = (lane & HALF) == 0
    sin_signed = jnp.where(first_half, -1.0, 1.0) * spread(sin_packed)

    def rope(t):
        swapped = jnp.where(first_half, pltpu.roll(t, LANES - HALF, axis=1),
                            pltpu.roll(t, HALF, axis=1))
        return t * cos + swapped * sin_signed

    scale = HEAD_DIM ** -0.5
    q_even, q_odd = [], []
    for p in range(n_slab_a):
        qs = rope(q_raw[p]) * scale
        q_even.append(jnp.where(lo, qs, 0.0).astype(_BF16))
        q_odd.append(jnp.where(lo, 0.0, qs).astype(_BF16))
    kr = rope(k_raw)
    kr_sw = pltpu.roll(kr, HEAD_DIM, axis=1)
    cur = slice(BLK, BLK + tokens)
    kd_buf[0, cur, :] = jnp.where(lo, kr, kr_sw).astype(_BF16)
    kd_buf[1, cur, :] = jnp.where(lo, kr_sw, kr).astype(_BF16)
    v_sw = pltpu.roll(v, HEAD_DIM, axis=1)
    vw_buf[0, cur, :] = jnp.where(lo, v, 0.0).astype(_BF16)
    vw_buf[1, cur, :] = jnp.where(lo, 0.0, v_sw).astype(_BF16)
    vw_buf[2, cur, :] = jnp.where(lo, v_sw, 0.0).astype(_BF16)
    vw_buf[3, cur, :] = jnp.where(lo, 0.0, v).astype(_BF16)

    u = jax.nn.gelu(jnp.concatenate((yield from project(0, D_GMLP)), axis=1))
    vgel = jax.nn.gelu(jnp.concatenate((yield from project(D_GMLP, D_GMLP)), axis=1))
    vn = _layer_norm(vgel, vg_ref[...], vb_ref[...])
    lo_g = (lax.broadcasted_iota(jnp.int32, (1, D_GMLP), 1) & HEAD_DIM) == 0
    vn_top = jnp.where(lo_g, vn, 0.0).astype(_BF16)
    vn_bot = jnp.where(lo_g, 0.0, vn).astype(_BF16)

    from_prev = si > ti
    no_prev = from_prev & (si < jnp.where(seq_start, BLK, 0))
    probs = {}
    for n in range(n_blk):
        rows = slice(n * BLK, (n + 1) * BLK)
        kv_rows = slice(n * BLK, (n + 2) * BLK)
        for g in range(N_KV_HEADS):
            heads = range(GQA_GROUP * g, GQA_GROUP * (g + 1))
            q_stack = jnp.concatenate(
                [(q_odd if h % 2 else q_even)[h // 2][rows] for h in heads], axis=0)
            scores = _dot_nt(q_stack, kd_buf[g, kv_rows, :])
            for hl, h in enumerate(heads):
                sc = scores[hl * BLK:(hl + 1) * BLK]
                sh = jnp.where(from_prev, sc[:, 0:BLK], sc[:, BLK:2 * BLK])
                if n == 0:
                    sh = jnp.where(no_prev, NEG_INF, sh)
                sink = sinks_ref[h]
                m = jnp.maximum(jnp.max(sh, axis=-1, keepdims=True), sink)
                pr = jnp.exp(sh - m)
                l = jnp.sum(pr, axis=-1, keepdims=True) + jnp.exp(sink - m)
                p_band = jnp.concatenate(
                    [jnp.where(from_prev, pr, 0.0), jnp.where(from_prev, 0.0, pr)], axis=1)
                probs[n, h] = (p_band.astype(_BF16), 1.0 / l)
        yield

    causal = si <= ti
    for i in range(max(n_slab_g, n_blk)):
        if i < n_slab_g:
            p = i
            w_pair = jnp.concatenate(
                [jnp.where(causal, wsp_ref[2 * p], 0.0), jnp.where(causal, wsp_ref[2 * p + 1], 0.0)],
                axis=1).astype(_BF16)
            bias = jnp.where(lo, column(bsp_ref[2 * p:2 * p + 1, :]),
                             column(bsp_ref[2 * p + 1:2 * p + 2, :]))
            for c in range(n_blk):
                rows = slice(c * BLK, (c + 1) * BLK)
                cols = slice(p * LANES, (p + 1) * LANES)
                rhs = jnp.concatenate([vn_top[rows, cols], vn_bot[rows, cols]], axis=0)
                mixed = _dot(w_pair, rhs) + bias
                mix_buf[rows, cols] = (u[rows, cols] * mixed).astype(_BF16)
            yield
        if i < n_blk:
            n = i
            rows = slice(n * BLK, (n + 1) * BLK)
            kv_rows = slice(n * BLK, (n + 2) * BLK)
            for slab in range(n_slab_a):
                g = (2 * slab) // GQA_GROUP
                (p_e, inv_e), (p_o, inv_o) = probs[n, 2 * slab], probs[n, 2 * slab + 1]
                acc = (_dot(p_e, vw_buf[2 * g, kv_rows, :])
                       + _dot(p_o, vw_buf[2 * g + 1, kv_rows, :]))
                out = acc * jnp.where(lo, inv_e, inv_o)
                mix_buf[rows, D_GMLP + slab * LANES:D_GMLP + (slab + 1) * LANES] = out.astype(_BF16)
            yield
    last = slice(tokens, tokens + BLK)
    kd_buf[:, 0:BLK, :] = kd_buf[:, last, :]
    vw_buf[:, 0:BLK, :] = vw_buf[:, last, :]

    mix = mix_buf[...]
    for c in range(0, x.shape[1], MXU_WIDTH):
        y_buf[:, c:c + MXU_WIDTH] = alpha * x[:, c:c + MXU_WIDTH] + _dot(mix, w_out_ref[:, c:c + MXU_WIDTH])
        yield


def _ffn_steps(rows, y_buf, g1_ref, b1_ref, w1_ref, w2_ref, g2_ref, b2_ref, o_ref, *, alpha):
    x = _layer_norm(y_buf[rows, :], g1_ref[...], b1_ref[...])
    xb = x.astype(_BF16)
    acc = alpha * x
    yield
    for j in range(w1_ref.shape[1] // FFN_CHUNK):
        cols = slice(j * FFN_CHUNK, (j + 1) * FFN_CHUNK)
        h = jnp.maximum(_dot(xb, w1_ref[:, cols]), 0.0)
        yield
        acc = acc + _dot((h * h).astype(_BF16), w2_ref[cols, :])
        yield
    o_ref[rows, :] = _layer_norm(acc, g2_ref[...], b2_ref[...])


def _run(steps, count=None):
    if count is None:
        for _ in steps:
            pass
    else:
        for _ in range(count):
            next(steps)


def _stage_weights(pairs, sem):
    def body(stage):
        jobs = []
        for src, dst in pairs:
            assert src.shape[0] % STAGE_ROWS == 0 and src.shape[1] % STAGE_COLS == 0
            for r0 in range(0, src.shape[0], STAGE_ROWS):
                for c0 in range(0, src.shape[1], STAGE_COLS):
                    jobs.append((src.at[pl.ds(r0, STAGE_ROWS), pl.ds(c0, STAGE_COLS)], dst, r0, c0))
        copies = [pltpu.make_async_copy(piece, stage.at[i % STAGE_SLOTS], sem.at[i % STAGE_SLOTS])
                  for i, (piece, _, _, _) in enumerate(jobs)]
        ahead = STAGE_SLOTS - 1
        for copy in copies[:ahead]:
            copy.start()
        for i, (_, dst, r0, c0) in enumerate(jobs):
            if i + ahead < len(jobs):
                copies[i + ahead].start()
            copies[i].wait()
            dst[r0:r0 + STAGE_ROWS, c0:c0 + STAGE_COLS] = stage[i % STAGE_SLOTS].astype(_BF16)

    pl.run_scoped(body, pltpu.VMEM((STAGE_SLOTS, STAGE_ROWS, STAGE_COLS), _F32))


def _layer_kernel(sinks_ref, x_ref, pos_ref, invf_ref, vg_ref, vb_ref, wsp_ref, bsp_ref, g1_ref,
                  b1_ref, g2_ref, b2_ref, w_in_hbm, w_out_hbm, w1_hbm, w2_hbm, o_ref,
                  y_buf, kd_buf, vw_buf, mix_buf, w_in_ref, w_out_ref, w1_ref, w2_ref, stage_sem,
                  *, alpha, n_tiles, tiles_per_seq):
    j = pl.program_id(0)
    seq_start = (j % tiles_per_seq) == 0

    @pl.when(seq_start)
    def _():
        kd_buf[:, 0:BLK, :] = jnp.zeros((N_KV_HEADS, BLK, LANES), _BF16)
        vw_buf[:, 0:BLK, :] = jnp.zeros((2 * N_KV_HEADS, BLK, LANES), _BF16)

    mixer = functools.partial(
        _mixer_steps, seq_start, sinks_ref, x_ref, pos_ref, invf_ref, w_in_ref, vg_ref, vb_ref,
        wsp_ref, bsp_ref, w_out_ref, y_buf, kd_buf, vw_buf, mix_buf, alpha=alpha)
    half = x_ref.shape[0] // 2
    ffn = [functools.partial(_ffn_steps, slice(i * half, (i + 1) * half), y_buf, g1_ref, b1_ref,
                             w1_ref, w2_ref, g2_ref, b2_ref, o_ref, alpha=alpha) for i in range(2)]

    @pl.when(j == 0)
    def _():
        _stage_weights([(w_in_hbm, w_in_ref), (w_out_hbm, w_out_ref), (w1_hbm, w1_ref),
                        (w2_hbm, w2_ref)], stage_sem)
        _run(mixer())

    @pl.when(j > 0)
    def _():
        fa, fb, m = ffn[0](), ffn[1](), mixer()
        n_blk = x_ref.shape[0] // BLK
        n_proj = w_in_ref.shape[1] // MXU_WIDTH
        ffn_pieces = 2 * (w1_ref.shape[1] // FFN_CHUNK)
        mixer_pieces = [1] * (n_proj + n_blk) + [2] * n_blk + [1]
        assert len(mixer_pieces) == 2 * ffn_pieces
        _run(fa, 1)
        for count in mixer_pieces[:ffn_pieces]:
            _run(m, count)
            _run(fa, 1)
        _run(fa)
        _run(fb, 1)
        for count in mixer_pieces[ffn_pieces:]:
            _run(m, count)
            _run(fb, 1)
        _run(m)
        _run(fb)


def _resident(shape):
    return pl.BlockSpec(shape, lambda *_: (0,) * len(shape), pipeline_mode=pl.Buffered(1))


def _layer(x, pos, invf, sinks, w_in, vg, vb, wsp, bsp, w_out, g1, b1, w1, w2, g2, b2, *, alpha,
           seq):
    n_tok, d_model = x.shape
    tokens = TILE_TOKENS
    assert seq % tokens == 0 and tokens % BLK == 0 and w1.shape[1] % FFN_CHUNK == 0
    n_tiles = n_tok // tokens
    kernel = functools.partial(_layer_kernel, alpha=alpha, n_tiles=n_tiles,
                               tiles_per_seq=seq // tokens)
    in_tile = pl.BlockSpec((tokens, d_model), lambda j: (jnp.minimum(j, n_tiles - 1), 0))
    pos_tile = pl.BlockSpec((None,) + pos.shape[1:], lambda j: (jnp.minimum(j, n_tiles - 1), 0, 0))
    out_tile = pl.BlockSpec((tokens, d_model), lambda j: (jnp.maximum(j - 1, 0), 0))
    resident = [invf, vg, vb, wsp, bsp, g1, b1, g2, b2]
    weights = [w_in, w_out, w1, w2]
    return pl.pallas_call(
        kernel,
        grid=(n_tiles + 1,),
        in_specs=[pl.BlockSpec(memory_space=pltpu.SMEM), in_tile, pos_tile]
        + [_resident(a.shape) for a in resident]
        + [pl.BlockSpec(memory_space=pl.ANY) for _ in weights],
        out_specs=out_tile,
        out_shape=jax.ShapeDtypeStruct(x.shape, x.dtype),
        scratch_shapes=[
            pltpu.VMEM((tokens, d_model), _F32),
            pltpu.VMEM((N_KV_HEADS, tokens + BLK, LANES), _BF16),
            pltpu.VMEM((2 * N_KV_HEADS, tokens + BLK, LANES), _BF16),
            pltpu.VMEM((tokens, D_GMLP + D_ATTN), _BF16),
        ] + [pltpu.VMEM(w.shape, _BF16) for w in weights] + [pltpu.SemaphoreType.DMA((STAGE_SLOTS,))],
        compiler_params=pltpu.CompilerParams(
            dimension_semantics=("arbitrary",), vmem_limit_bytes=VMEM_LIMIT_BYTES),
        name="layer",
    )(sinks, x, pos, *resident, *weights)


def kernel(x, positions, w_in, v_ln_g, v_ln_b, w_spatial, b_spatial, sinks, w_out, ln1_g, ln1_b,
           w_ff1, w_ff2, ln2_g, ln2_b):
    batch, seq, d_model = x.shape
    depth = w_in.shape[0]
    alpha = (2.0 * depth) ** 0.25
    inv_freq = ROPE_THETA ** (-jnp.arange(0, HEAD_DIM, 2, dtype=_F32) / HEAD_DIM)
    invf = jnp.tile(inv_freq, LANES // HALF)[None, :]
    pos = positions.reshape(batch * seq // TILE_TOKENS, LANES // HALF, BLK)
    x = x.reshape(batch * seq, d_model)
    row = lambda a: a[None, :]
    for l in range(depth):
        x = _layer(x, pos, invf, sinks[l], w_in[l], row(v_ln_g[l]), row(v_ln_b[l]),
                   w_spatial[l], b_spatial[l], w_out[l], row(ln1_g[l]), row(ln1_b[l]),
                   w_ff1[l], w_ff2[l], row(ln2_g[l]), row(ln2_b[l]), alpha=alpha, seq=seq)
    return x.reshape(batch, seq, d_model)
```

```python
import functools

import jax
import jax.numpy as jnp
from jax import lax
from jax.experimental import pallas as pl
from jax.experimental.pallas import tpu as pltpu

HEAD_DIM = 64
N_GMLP_HEADS = 8
D_GMLP = N_GMLP_HEADS * HEAD_DIM
N_Q_HEADS = 8
N_KV_HEADS = 2
GQA_GROUP = N_Q_HEADS // N_KV_HEADS
D_ATTN = N_Q_HEADS * HEAD_DIM
D_KV = N_KV_HEADS * HEAD_DIM
BLK = 128
ROPE_THETA = 10000.0
LN_EPS = 1e-5
NEG_INF = -1e30
LANES = 128
HALF = HEAD_DIM // 2
MXU_WIDTH = 256

TILE_TOKENS = 512
FFN_CHUNK = 1024
STAGE_ROWS, STAGE_COLS = 1024, 256
STAGE_SLOTS = 6
STAGE_POINT_EVERY = 3
VMEM_LIMIT_BYTES = 56 * 1024 * 1024

_BF16 = jnp.bfloat16
_F32 = jnp.float32


def _dot(a, b):
    return jnp.dot(a, b, preferred_element_type=_F32)


def _dot_nt(a, b):
    return lax.dot_general(a, b, (((1,), (1,)), ((), ())), preferred_element_type=_F32)


def _layer_norm(v, g, b):
    mu = jnp.mean(v, axis=-1, keepdims=True)
    c = v - mu
    var = jnp.mean(c * c, axis=-1, keepdims=True)
    return c * lax.rsqrt(var + LN_EPS) * g + b


def _mixer_steps(seq_start, sinks_ref, x_ref, pos_ref, invf_ref, w_in_ref, vg_ref, vb_ref, wsp_ref,
                 bsp_ref, w_out_ref, y_buf, kd_buf, vw_buf, mix_buf, *, alpha):
    tokens = x_ref.shape[0]
    n_blk = tokens // BLK
    n_slab_g = D_GMLP // LANES
    n_slab_a = D_ATTN // LANES
    q0 = 2 * D_GMLP
    k0 = q0 + D_ATTN

    x = x_ref[...]
    xb = x.astype(_BF16)
    lane = lax.broadcasted_iota(jnp.int32, (1, LANES), 1)
    lo = lane < HEAD_DIM
    ti = lax.broadcasted_iota(jnp.int32, (BLK, BLK), 0)
    si = lax.broadcasted_iota(jnp.int32, (BLK, BLK), 1)

    def column(row):
        return jnp.sum(jnp.where(si == ti, row, 0.0), axis=1, keepdims=True)

    def project(col0, width):
        parts = []
        for c in range(col0, col0 + width, MXU_WIDTH):
            parts.append(_dot(xb, w_in_ref[:, c:c + MXU_WIDTH]))
            yield
        return parts

    qkv = yield from project(q0, D_ATTN + 2 * D_KV)
    slabs = [part[:, i * LANES:(i + 1) * LANES] for part in qkv for i in range(MXU_WIDTH // LANES)]
    q_raw, k_raw, v = slabs[:n_slab_a], slabs[n_slab_a], slabs[n_slab_a + 1]

    pos = pos_ref[...].astype(_F32)
    invf = invf_ref[...]
    n_grp = LANES // HALF
    quarter = tokens // n_grp
    assert pos.shape == (n_grp, quarter) and quarter == BLK
    grp = lane // HALF
    ang = None
    for a in range(n_grp):
        term = column(pos[a:a + 1, :]) * jnp.where(grp == a, invf, 0.0)
        ang = term if ang is None else ang + term
    cos_packed, sin_packed = jnp.cos(ang), jnp.sin(ang)

    def spread(packed):
        parts = []
        for a in range(n_grp):
            one = jnp.where(grp == a, packed, 0.0)
            two = one + pltpu.roll(one, 2 * HALF, axis=1)
            parts.append(two + pltpu.roll(two, HALF, axis=1))
        return jnp.concatenate(parts, axis=0)

    cos = spread(cos_packed)
    first_half = (lane & HALF) == 0
    sin_signed = jnp.where(first_half, -1.0, 1.0) * spread(sin_packed)

    def rope(t):
        swapped = jnp.where(first_half, pltpu.roll(t, LANES - HALF, axis=1),
                            pltpu.roll(t, HALF, axis=1))
        return t * cos + swapped * sin_signed

    scale = HEAD_DIM ** -0.5
    q_even, q_odd = [], []
    for p in range(n_slab_a):
        qs = rope(q_raw[p]) * scale
        q_even.append(jnp.where(lo, qs, 0.0).astype(_BF16))
        q_odd.append(jnp.where(lo, 0.0, qs).astype(_BF16))
    kr = rope(k_raw)
    kr_sw = pltpu.roll(kr, HEAD_DIM, axis=1)
    cur = slice(BLK, BLK + tokens)
    kd_buf[0, cur, :] = jnp.where(lo, kr, kr_sw).astype(_BF16)
    kd_buf[1, cur, :] = jnp.where(lo, kr_sw, kr).astype(_BF16)
    v_sw = pltpu.roll(v, HEAD_DIM, axis=1)
    vw_buf[0, cur, :] = jnp.where(lo, v, 0.0).astype(_BF16)
    vw_buf[1, cur, :] = jnp.where(lo, 0.0, v_sw).astype(_BF16)
    vw_buf[2, cur, :] = jnp.where(lo, v_sw, 0.0).astype(_BF16)
    vw_buf[3, cur, :] = jnp.where(lo, 0.0, v).astype(_BF16)

    u = jax.nn.gelu(jnp.concatenate((yield from project(0, D_GMLP)), axis=1))
    vgel = jax.nn.gelu(jnp.concatenate((yield from project(D_GMLP, D_GMLP)), axis=1))
    vn = _layer_norm(vgel, vg_ref[...], vb_ref[...])
    lo_g = (lax.broadcasted_iota(jnp.int32, (1, D_GMLP), 1) & HEAD_DIM) == 0
    vn_top = jnp.where(lo_g, vn, 0.0).astype(_BF16)
    vn_bot = jnp.where(lo_g, 0.0, vn).astype(_BF16)

    from_prev = si > ti
    no_prev = from_prev & (si < jnp.where(seq_start, BLK, 0))
    probs = {}
    for n in range(n_blk):
        rows = slice(n * BLK, (n + 1) * BLK)
        kv_rows = slice(n * BLK, (n + 2) * BLK)
        for g in range(N_KV_HEADS):
            heads = range(GQA_GROUP * g, GQA_GROUP * (g + 1))
            q_stack = jnp.concatenate(
                [(q_odd if h % 2 else q_even)[h // 2][rows] for h in heads], axis=0)
            scores = _dot_nt(q_stack, kd_buf[g, kv_rows, :])
            for hl, h in enumerate(heads):
                sc = scores[hl * BLK:(hl + 1) * BLK]
                sh = jnp.where(from_prev, sc[:, 0:BLK], sc[:, BLK:2 * BLK])
                if n == 0:
                    sh = jnp.where(no_prev, NEG_INF, sh)
                sink = sinks_ref[h]
                m = jnp.maximum(jnp.max(sh, axis=-1, keepdims=True), sink)
                pr = jnp.exp(sh - m)
                l = jnp.sum(pr, axis=-1, keepdims=True) + jnp.exp(sink - m)
                p_band = jnp.concatenate(
                    [jnp.where(from_prev, pr, 0.0), jnp.where(from_prev, 0.0, pr)], axis=1)
                probs[n, h] = (p_band.astype(_BF16), 1.0 / l)
        yield

    causal = si <= ti
    for i in range(max(n_slab_g, n_blk)):
        if i < n_slab_g:
            p = i
            w_pair = jnp.concatenate(
                [jnp.where(causal, wsp_ref[2 * p], 0.0), jnp.where(causal, wsp_ref[2 * p + 1], 0.0)],
                axis=1).astype(_BF16)
            bias = jnp.where(lo, column(bsp_ref[2 * p:2 * p + 1, :]),
                             column(bsp_ref[2 * p + 1:2 * p + 2, :]))
            for c in range(n_blk):
                rows = slice(c * BLK, (c + 1) * BLK)
                cols = slice(p * LANES, (p + 1) * LANES)
                rhs = jnp.concatenate([vn_top[rows, cols], vn_bot[rows, cols]], axis=0)
                mixed = _dot(w_pair, rhs) + bias
                mix_buf[rows, cols] = (u[rows, cols] * mixed).astype(_BF16)
            yield
        if i < n_blk:
            n = i
            rows = slice(n * BLK, (n + 1) * BLK)
            kv_rows = slice(n * BLK, (n + 2) * BLK)
            for slab in range(n_slab_a):
                g = (2 * slab) // GQA_GROUP
                (p_e, inv_e), (p_o, inv_o) = probs[n, 2 * slab], probs[n, 2 * slab + 1]
                acc = (_dot(p_e, vw_buf[2 * g, kv_rows, :])
                       + _dot(p_o, vw_buf[2 * g + 1, kv_rows, :]))
                out = acc * jnp.where(lo, inv_e, inv_o)
                mix_buf[rows, D_GMLP + slab * LANES:D_GMLP + (slab + 1) * LANES] = out.astype(_BF16)
            yield
    last = slice(tokens, tokens + BLK)
    kd_buf[:, 0:BLK, :] = kd_buf[:, last, :]
    vw_buf[:, 0:BLK, :] = vw_buf[:, last, :]

    mix = mix_buf[...]
    for c in range(0, x.shape[1], MXU_WIDTH):
        y_buf[:, c:c + MXU_WIDTH] = alpha * x[:, c:c + MXU_WIDTH] + _dot(mix, w_out_ref[:, c:c + MXU_WIDTH])
        yield


def _ffn_steps(rows, y_buf, g1_ref, b1_ref, w1_ref, w2_ref, g2_ref, b2_ref, o_ref, *, alpha):
    x = _layer_norm(y_buf[rows, :], g1_ref[...], b1_ref[...])
    xb = x.astype(_BF16)
    acc = alpha * x
    yield
    for j in range(w1_ref.shape[1] // FFN_CHUNK):
        cols = slice(j * FFN_CHUNK, (j + 1) * FFN_CHUNK)
        h = jnp.maximum(_dot(xb, w1_ref[:, cols]), 0.0)
        yield
        acc = acc + _dot((h * h).astype(_BF16), w2_ref[cols, :])
        yield
    o_ref[rows, :] = _layer_norm(acc, g2_ref[...], b2_ref[...])


def _run(steps, count=None):
    if count is None:
        for _ in steps:
            pass
    else:
        for _ in range(count):
            next(steps)


def _stage_jobs(pairs):
    jobs = []
    for src, dst in pairs:
        assert src.shape[0] % STAGE_ROWS == 0 and src.shape[1] % STAGE_COLS == 0
        for r0 in range(0, src.shape[0], STAGE_ROWS):
            for c0 in range(0, src.shape[1], STAGE_COLS):
                jobs.append((src.at[pl.ds(r0, STAGE_ROWS), pl.ds(c0, STAGE_COLS)], dst, r0, c0))
    return jobs


def _stage_steps(jobs, stage, sem):
    copies = [pltpu.make_async_copy(piece, stage.at[i % STAGE_SLOTS], sem.at[i % STAGE_SLOTS])
              for i, (piece, _, _, _) in enumerate(jobs)]
    ahead = STAGE_SLOTS - 1
    for copy in copies[:ahead]:
        copy.start()
    for i, (_, dst, r0, c0) in enumerate(jobs):
        if i + ahead < len(jobs):
            copies[i + ahead].start()
        copies[i].wait()
        dst[r0:r0 + STAGE_ROWS, c0:c0 + STAGE_COLS] = stage[i % STAGE_SLOTS].astype(_BF16)
        yield


def _layer_kernel(sinks_ref, x_ref, pos_ref, invf_ref, vg_ref, vb_ref, wsp_ref, bsp_ref, g1_ref,
                  b1_ref, g2_ref, b2_ref, w_in_hbm, w_out_hbm, w1_hbm, w2_hbm, o_ref,
                  y_buf, kd_buf, vw_buf, mix_buf, w_in_ref, w_out_ref, w1_ref, w2_ref, stage_sem,
                  *, alpha, n_tiles, tiles_per_seq):
    j = pl.program_id(0)
    seq_start = (j % tiles_per_seq) == 0

    @pl.when(seq_start)
    def _():
        kd_buf[:, 0:BLK, :] = jnp.zeros((N_KV_HEADS, BLK, LANES), _BF16)
        vw_buf[:, 0:BLK, :] = jnp.zeros((2 * N_KV_HEADS, BLK, LANES), _BF16)

    mixer = functools.partial(
        _mixer_steps, seq_start, sinks_ref, x_ref, pos_ref, invf_ref, w_in_ref, vg_ref, vb_ref,
        wsp_ref, bsp_ref, w_out_ref, y_buf, kd_buf, vw_buf, mix_buf, alpha=alpha)
    half = x_ref.shape[0] // 2
    ffn = [functools.partial(_ffn_steps, slice(i * half, (i + 1) * half), y_buf, g1_ref, b1_ref,
                             w1_ref, w2_ref, g2_ref, b2_ref, o_ref, alpha=alpha) for i in range(2)]

    @pl.when(j == 0)
    def _():
        def first_step(stage):
            _run(_stage_steps(_stage_jobs([(w_in_hbm, w_in_ref)]), stage, stage_sem))
            later = _stage_steps(
                _stage_jobs([(w_out_hbm, w_out_ref), (w1_hbm, w1_ref), (w2_hbm, w2_ref)]),
                stage, stage_sem)
            assert len(_stage_jobs([(w_out_hbm, w_out_ref)])) <= STAGE_SLOTS - 1
            for i, _ in enumerate(mixer()):
                if i % STAGE_POINT_EVERY == STAGE_POINT_EVERY - 1:
                    for _ in range(STAGE_SLOTS - 1):
                        next(later, None)
            _run(later)

        pl.run_scoped(first_step, pltpu.VMEM((STAGE_SLOTS, STAGE_ROWS, STAGE_COLS), _F32))

    @pl.when(j > 0)
    def _():
        fa, fb, m = ffn[0](), ffn[1](), mixer()
        n_blk = x_ref.shape[0] // BLK
        n_proj = w_in_ref.shape[1] // MXU_WIDTH
        ffn_pieces = 2 * (w1_ref.shape[1] // FFN_CHUNK)
        mixer_pieces = [1] * (n_proj + n_blk) + [2] * n_blk + [1]
        assert len(mixer_pieces) == 2 * ffn_pieces
        _run(fa, 1)
        for count in mixer_pieces[:ffn_pieces]:
            _run(m, count)
            _run(fa, 1)
        _run(fa)
        _run(fb, 1)
        for count in mixer_pieces[ffn_pieces:]:
            _run(m, count)
            _run(fb, 1)
        _run(m)
        _run(fb)


def _resident(shape):
    return pl.BlockSpec(shape, lambda *_: (0,) * len(shape), pipeline_mode=pl.Buffered(1))


def _layer(x, pos, invf, sinks, w_in, vg, vb, wsp, bsp, w_out, g1, b1, w1, w2, g2, b2, *, alpha,
           seq):
    n_tok, d_model = x.shape
    tokens = TILE_TOKENS
    assert seq % tokens == 0 and tokens % BLK == 0 and w1.shape[1] % FFN_CHUNK == 0
    n_tiles = n_tok // tokens
    kernel = functools.partial(_layer_kernel, alpha=alpha, n_tiles=n_tiles,
                               tiles_per_seq=seq // tokens)
    in_tile = pl.BlockSpec((tokens, d_model), lambda j: (jnp.minimum(j, n_tiles - 1), 0))
    pos_tile = pl.BlockSpec((None,) + pos.shape[1:], lambda j: (jnp.minimum(j, n_tiles - 1), 0, 0))
    out_tile = pl.BlockSpec((tokens, d_model), lambda j: (jnp.maximum(j - 1, 0), 0))
    resident = [invf, vg, vb, wsp, bsp, g1, b1, g2, b2]
    weights = [w_in, w_out, w1, w2]
    return pl.pallas_call(
        kernel,
        grid=(n_tiles + 1,),
        in_specs=[pl.BlockSpec(memory_space=pltpu.SMEM), in_tile, pos_tile]
        + [_resident(a.shape) for a in resident]
        + [pl.BlockSpec(memory_space=pl.ANY) for _ in weights],
        out_specs=out_tile,
        out_shape=jax.ShapeDtypeStruct(x.shape, x.dtype),
        scratch_shapes=[
            pltpu.VMEM((tokens, d_model), _F32),
            pltpu.VMEM((N_KV_HEADS, tokens + BLK, LANES), _BF16),
            pltpu.VMEM((2 * N_KV_HEADS, tokens + BLK, LANES), _BF16),
            pltpu.VMEM((tokens, D_GMLP + D_ATTN), _BF16),
        ] + [pltpu.VMEM(w.shape, _BF16) for w in weights] + [pltpu.SemaphoreType.DMA((STAGE_SLOTS,))],
        compiler_params=pltpu.CompilerParams(
            dimension_semantics=("arbitrary",), vmem_limit_bytes=VMEM_LIMIT_BYTES),
        name="layer",
    )(sinks, x, pos, *resident, *weights)


def kernel(x, positions, w_in, v_ln_g, v_ln_b, w_spatial, b_spatial, sinks, w_out, ln1_g, ln1_b,
           w_ff1, w_ff2, ln2_g, ln2_b):
    batch, seq, d_model = x.shape
    depth = w_in.shape[0]
    alpha = (2.0 * depth) ** 0.25
    inv_freq = ROPE_THETA ** (-jnp.arange(0, HEAD_DIM, 2, dtype=_F32) / HEAD_DIM)
    invf = jnp.tile(inv_freq, LANES // HALF)[None, :]
    pos = positions.reshape(batch * seq // TILE_TOKENS, LANES // HALF, BLK)
    x = x.reshape(batch * seq, d_model)
    row = lambda a: a[None, :]
    for l in range(depth):
        x = _layer(x, pos, invf, sinks[l], w_in[l], row(v_ln_g[l]), row(v_ln_b[l]),
                   w_spatial[l], b_spatial[l], w_out[l], row(ln1_g[l]), row(ln1_b[l]),
                   w_ff1[l], w_ff2[l], row(ln2_g[l]), row(ln2_b[l]), alpha=alpha, seq=seq)
    return x.reshape(batch, seq, d_model)
```

```python
import functools

import jax
import jax.numpy as jnp
from jax import lax
from jax.experimental import pallas as pl
from jax.experimental.pallas import tpu as pltpu

HEAD_DIM = 64
N_GMLP_HEADS = 8
D_GMLP = N_GMLP_HEADS * HEAD_DIM
N_Q_HEADS = 8
N_KV_HEADS = 2
GQA_GROUP = N_Q_HEADS // N_KV_HEADS
D_ATTN = N_Q_HEADS * HEAD_DIM
D_KV = N_KV_HEADS * HEAD_DIM
BLK = 128
ROPE_THETA = 10000.0
LN_EPS = 1e-5
NEG_INF = -1e30
LANES = 128
HALF = HEAD_DIM // 2
MXU_WIDTH = 256

TILE_TOKENS = 512
FFN_CHUNK = 1024
STAGE_ROWS, STAGE_COLS = 256, 1024
STAGE_SLOTS = 6
STAGE_POINT_EVERY = 3
VMEM_LIMIT_BYTES = 56 * 1024 * 1024

_BF16 = jnp.bfloat16
_F32 = jnp.float32


def _dot(a, b):
    return jnp.dot(a, b, preferred_element_type=_F32)


def _dot_nt(a, b):
    return lax.dot_general(a, b, (((1,), (1,)), ((), ())), preferred_element_type=_F32)


def _layer_norm(v, g, b):
    mu = jnp.mean(v, axis=-1, keepdims=True)
    c = v - mu
    var = jnp.mean(c * c, axis=-1, keepdims=True)
    return c * lax.rsqrt(var + LN_EPS) * g + b


def _mixer_steps(seq_start, sinks_ref, x_ref, pos_ref, invf_ref, w_in_ref, vg_ref, vb_ref, wsp_ref,
                 bsp_ref, w_out_ref, y_buf, kd_buf, vw_buf, mix_buf, *, alpha):
    tokens = x_ref.shape[0]
    n_blk = tokens // BLK
    n_slab_g = D_GMLP // LANES
    n_slab_a = D_ATTN // LANES
    q0 = 2 * D_GMLP
    k0 = q0 + D_ATTN

    x = x_ref[...]
    xb = x.astype(_BF16)
    lane = lax.broadcasted_iota(jnp.int32, (1, LANES), 1)
    lo = lane < HEAD_DIM
    ti = lax.broadcasted_iota(jnp.int32, (BLK, BLK), 0)
    si = lax.broadcasted_iota(jnp.int32, (BLK, BLK), 1)

    def column(row):
        return jnp.sum(jnp.where(si == ti, row, 0.0), axis=1, keepdims=True)

    def project(col0, width):
        parts = []
        for c in range(col0, col0 + width, MXU_WIDTH):
            parts.append(_dot(xb, w_in_ref[:, c:c + MXU_WIDTH]))
            yield
        return parts

    qkv = yield from project(q0, D_ATTN + 2 * D_KV)
    slabs = [part[:, i * LANES:(i + 1) * LANES] for part in qkv for i in range(MXU_WIDTH // LANES)]
    q_raw, k_raw, v = slabs[:n_slab_a], slabs[n_slab_a], slabs[n_slab_a + 1]

    pos = pos_ref[...].astype(_F32)
    invf = invf_ref[...]
    n_grp = LANES // HALF
    quarter = tokens // n_grp
    assert pos.shape == (n_grp, quarter) and quarter == BLK
    grp = lane // HALF
    ang = None
    for a in range(n_grp):
        term = column(pos[a:a + 1, :]) * jnp.where(grp == a, invf, 0.0)
        ang = term if ang is None else ang + term
    cos_packed, sin_packed = jnp.cos(ang), jnp.sin(ang)

    def spread(packed):
        parts = []
        for a in range(n_grp):
            one = jnp.where(grp == a, packed, 0.0)
            two = one + pltpu.roll(one, 2 * HALF, axis=1)
            parts.append(two + pltpu.roll(two, HALF, axis=1))
        return jnp.concatenate(parts, axis=0)

    cos = spread(cos_packed)
    first_half = (lane & HALF) == 0
    sin_signed = jnp.where(first_half, -1.0, 1.0) * spread(sin_packed)

    def rope(t):
        swapped = jnp.where(first_half, pltpu.roll(t, LANES - HALF, axis=1),
                            pltpu.roll(t, HALF, axis=1))
        return t * cos + swapped * sin_signed

    scale = HEAD_DIM ** -0.5
    q_even, q_odd = [], []
    for p in range(n_slab_a):
        qs = rope(q_raw[p]) * scale
        q_even.append(jnp.where(lo, qs, 0.0).astype(_BF16))
        q_odd.append(jnp.where(lo, 0.0, qs).astype(_BF16))
    kr = rope(k_raw)
    kr_sw = pltpu.roll(kr, HEAD_DIM, axis=1)
    cur = slice(BLK, BLK + tokens)
    kd_buf[0, cur, :] = jnp.where(lo, kr, kr_sw).astype(_BF16)
    kd_buf[1, cur, :] = jnp.where(lo, kr_sw, kr).astype(_BF16)
    v_sw = pltpu.roll(v, HEAD_DIM, axis=1)
    vw_buf[0, cur, :] = jnp.where(lo, v, 0.0).astype(_BF16)
    vw_buf[1, cur, :] = jnp.where(lo, 0.0, v_sw).astype(_BF16)
    vw_buf[2, cur, :] = jnp.where(lo, v_sw, 0.0).astype(_BF16)
    vw_buf[3, cur, :] = jnp.where(lo, 0.0, v).astype(_BF16)

    u = jax.nn.gelu(jnp.concatenate((yield from project(0, D_GMLP)), axis=1))
    vgel = jax.nn.gelu(jnp.concatenate((yield from project(D_GMLP, D_GMLP)), axis=1))
    vn = _layer_norm(vgel, vg_ref[...], vb_ref[...])
    lo_g = (lax.broadcasted_iota(jnp.int32, (1, D_GMLP), 1) & HEAD_DIM) == 0
    vn_top = jnp.where(lo_g, vn, 0.0).astype(_BF16)
    vn_bot = jnp.where(lo_g, 0.0, vn).astype(_BF16)

    from_prev = si > ti
    no_prev = from_prev & (si < jnp.where(seq_start, BLK, 0))
    probs = {}
    for n in range(n_blk):
        rows = slice(n * BLK, (n + 1) * BLK)
        kv_rows = slice(n * BLK, (n + 2) * BLK)
        for g in range(N_KV_HEADS):
            heads = range(GQA_GROUP * g, GQA_GROUP * (g + 1))
            q_stack = jnp.concatenate(
                [(q_odd if h % 2 else q_even)[h // 2][rows] for h in heads], axis=0)
            scores = _dot_nt(q_stack, kd_buf[g, kv_rows, :])
            for hl, h in enumerate(heads):
                sc = scores[hl * BLK:(hl + 1) * BLK]
                sh = jnp.where(from_prev, sc[:, 0:BLK], sc[:, BLK:2 * BLK])
                if n == 0:
                    sh = jnp.where(no_prev, NEG_INF, sh)
                sink = sinks_ref[h]
                m = jnp.maximum(jnp.max(sh, axis=-1, keepdims=True), sink)
                pr = jnp.exp(sh - m)
                l = jnp.sum(pr, axis=-1, keepdims=True) + jnp.exp(sink - m)
                p_band = jnp.concatenate(
                    [jnp.where(from_prev, pr, 0.0), jnp.where(from_prev, 0.0, pr)], axis=1)
                probs[n, h] = (p_band.astype(_BF16), 1.0 / l)
        yield

    causal = si <= ti
    for i in range(max(n_slab_g, n_blk)):
        if i < n_slab_g:
            p = i
            w_pair = jnp.concatenate(
                [jnp.where(causal, wsp_ref[2 * p], 0.0), jnp.where(causal, wsp_ref[2 * p + 1], 0.0)],
                axis=1).astype(_BF16)
            bias = jnp.where(lo, column(bsp_ref[2 * p:2 * p + 1, :]),
                             column(bsp_ref[2 * p + 1:2 * p + 2, :]))
            for c in range(n_blk):
                rows = slice(c * BLK, (c + 1) * BLK)
                cols = slice(p * LANES, (p + 1) * LANES)
                rhs = jnp.concatenate([vn_top[rows, cols], vn_bot[rows, cols]], axis=0)
                mixed = _dot(w_pair, rhs) + bias
                mix_buf[rows, cols] = (u[rows, cols] * mixed).astype(_BF16)
            yield
        if i < n_blk:
            n = i
            rows = slice(n * BLK, (n + 1) * BLK)
            kv_rows = slice(n * BLK, (n + 2) * BLK)
            for slab in range(n_slab_a):
                g = (2 * slab) // GQA_GROUP
                (p_e, inv_e), (p_o, inv_o) = probs[n, 2 * slab], probs[n, 2 * slab + 1]
                acc = (_dot(p_e, vw_buf[2 * g, kv_rows, :])
                       + _dot(p_o, vw_buf[2 * g + 1, kv_rows, :]))
                out = acc * jnp.where(lo, inv_e, inv_o)
                mix_buf[rows, D_GMLP + slab * LANES:D_GMLP + (slab + 1) * LANES] = out.astype(_BF16)
            yield
    last = slice(tokens, tokens + BLK)
    kd_buf[:, 0:BLK, :] = kd_buf[:, last, :]
    vw_buf[:, 0:BLK, :] = vw_buf[:, last, :]

    mix = mix_buf[...]
    for c in range(0, x.shape[1], MXU_WIDTH):
        y_buf[:, c:c + MXU_WIDTH] = alpha * x[:, c:c + MXU_WIDTH] + _dot(mix, w_out_ref[:, c:c + MXU_WIDTH])
        yield


def _ffn_steps(rows, y_buf, g1_ref, b1_ref, w1_ref, w2_ref, g2_ref, b2_ref, o_ref, *, alpha):
    x = _layer_norm(y_buf[rows, :], g1_ref[...], b1_ref[...])
    xb = x.astype(_BF16)
    acc = alpha * x
    yield
    for j in range(w1_ref.shape[1] // FFN_CHUNK):
        cols = slice(j * FFN_CHUNK, (j + 1) * FFN_CHUNK)
        h = jnp.maximum(_dot(xb, w1_ref[:, cols]), 0.0)
        yield
        acc = acc + _dot((h * h).astype(_BF16), w2_ref[cols, :])
        yield
    o_ref[rows, :] = _layer_norm(acc, g2_ref[...], b2_ref[...])


def _run(steps, count=None):
    if count is None:
        for _ in steps:
            pass
    else:
        for _ in range(count):
            next(steps)


def _stage_jobs(pairs):
    jobs = []
    for src, dst in pairs:
        assert src.shape[0] % STAGE_ROWS == 0 and src.shape[1] % LANES == 0
        for r0 in range(0, src.shape[0], STAGE_ROWS):
            for c0 in range(0, src.shape[1], STAGE_COLS):
                jobs.append((src, dst, r0, c0, min(STAGE_COLS, src.shape[1] - c0)))
    return jobs


def _stage_steps(jobs, stage, sem):
    copies = [pltpu.make_async_copy(src.at[pl.ds(r0, STAGE_ROWS), pl.ds(c0, width)],
                                    stage.at[i % STAGE_SLOTS, :, pl.ds(0, width)],
                                    sem.at[i % STAGE_SLOTS])
              for i, (src, _, r0, c0, width) in enumerate(jobs)]
    ahead = STAGE_SLOTS - 1
    for copy in copies[:ahead]:
        copy.start()
    for i, (_, dst, r0, c0, width) in enumerate(jobs):
        if i + ahead < len(jobs):
            copies[i + ahead].start()
        copies[i].wait()
        dst[r0:r0 + STAGE_ROWS, c0:c0 + width] = stage[i % STAGE_SLOTS, :, 0:width].astype(_BF16)
        yield


def _layer_kernel(sinks_ref, x_ref, pos_ref, invf_ref, vg_ref, vb_ref, wsp_ref, bsp_ref, g1_ref,
                  b1_ref, g2_ref, b2_ref, w_in_hbm, w_out_hbm, w1_hbm, w2_hbm, o_ref,
                  y_buf, kd_buf, vw_buf, mix_buf, w_in_ref, w_out_ref, w1_ref, w2_ref, stage_sem,
                  *, alpha, n_tiles, tiles_per_seq):
    j = pl.program_id(0)
    seq_start = (j % tiles_per_seq) == 0

    @pl.when(seq_start)
    def _():
        kd_buf[:, 0:BLK, :] = jnp.zeros((N_KV_HEADS, BLK, LANES), _BF16)
        vw_buf[:, 0:BLK, :] = jnp.zeros((2 * N_KV_HEADS, BLK, LANES), _BF16)

    mixer = functools.partial(
        _mixer_steps, seq_start, sinks_ref, x_ref, pos_ref, invf_ref, w_in_ref, vg_ref, vb_ref,
        wsp_ref, bsp_ref, w_out_ref, y_buf, kd_buf, vw_buf, mix_buf, alpha=alpha)
    half = x_ref.shape[0] // 2
    ffn = [functools.partial(_ffn_steps, slice(i * half, (i + 1) * half), y_buf, g1_ref, b1_ref,
                             w1_ref, w2_ref, g2_ref, b2_ref, o_ref, alpha=alpha) for i in range(2)]

    @pl.when(j == 0)
    def _():
        def first_step(stage):
            _run(_stage_steps(_stage_jobs([(w_in_hbm, w_in_ref)]), stage, stage_sem))
            later = _stage_steps(
                _stage_jobs([(w_out_hbm, w_out_ref), (w1_hbm, w1_ref), (w2_hbm, w2_ref)]),
                stage, stage_sem)
            assert len(_stage_jobs([(w_out_hbm, w_out_ref)])) <= STAGE_SLOTS - 1
            for i, _ in enumerate(mixer()):
                if i % STAGE_POINT_EVERY == STAGE_POINT_EVERY - 1:
                    for _ in range(STAGE_SLOTS - 1):
                        next(later, None)
            _run(later)

        pl.run_scoped(first_step, pltpu.VMEM((STAGE_SLOTS, STAGE_ROWS, STAGE_COLS), _F32))

    @pl.when(j > 0)
    def _():
        fa, fb, m = ffn[0](), ffn[1](), mixer()
        n_blk = x_ref.shape[0] // BLK
        n_proj = w_in_ref.shape[1] // MXU_WIDTH
        ffn_pieces = 2 * (w1_ref.shape[1] // FFN_CHUNK)
        mixer_pieces = [1] * (n_proj + n_blk) + [2] * n_blk + [1]
        assert len(mixer_pieces) == 2 * ffn_pieces
        _run(fa, 1)
        for count in mixer_pieces[:ffn_pieces]:
            _run(m, count)
            _run(fa, 1)
        _run(fa)
        _run(fb, 1)
        for count in mixer_pieces[ffn_pieces:]:
            _run(m, count)
            _run(fb, 1)
        _run(m)
        _run(fb)


def _resident(shape):
    return pl.BlockSpec(shape, lambda *_: (0,) * len(shape), pipeline_mode=pl.Buffered(1))


def _layer(x, pos, invf, sinks, w_in, vg, vb, wsp, bsp, w_out, g1, b1, w1, w2, g2, b2, *, alpha,
           seq):
    n_tok, d_model = x.shape
    tokens = TILE_TOKENS
    assert seq % tokens == 0 and tokens % BLK == 0 and w1.shape[1] % FFN_CHUNK == 0
    n_tiles = n_tok // tokens
    kernel = functools.partial(_layer_kernel, alpha=alpha, n_tiles=n_tiles,
                               tiles_per_seq=seq // tokens)
    in_tile = pl.BlockSpec((tokens, d_model), lambda j: (jnp.minimum(j, n_tiles - 1), 0))
    pos_tile = pl.BlockSpec((None,) + pos.shape[1:], lambda j: (jnp.minimum(j, n_tiles - 1), 0, 0))
    out_tile = pl.BlockSpec((tokens, d_model), lambda j: (jnp.maximum(j - 1, 0), 0))
    resident = [invf, vg, vb, wsp, bsp, g1, b1, g2, b2]
    weights = [w_in, w_out, w1, w2]
    return pl.pallas_call(
        kernel,
        grid=(n_tiles + 1,),
        in_specs=[pl.BlockSpec(memory_space=pltpu.SMEM), in_tile, pos_tile]
        + [_resident(a.shape) for a in resident]
        + [pl.BlockSpec(memory_space=pl.ANY) for _ in weights],
        out_specs=out_tile,
        out_shape=jax.ShapeDtypeStruct(x.shape, x.dtype),
        scratch_shapes=[
            pltpu.VMEM((tokens, d_model), _F32),
            pltpu.VMEM((N_KV_HEADS, tokens + BLK, LANES), _BF16),
            pltpu.VMEM((2 * N_KV_HEADS, tokens + BLK, LANES), _BF16),
            pltpu.VMEM((tokens, D_GMLP + D_ATTN), _BF16),
        ] + [pltpu.VMEM(w.shape, _BF16) for w in weights] + [pltpu.SemaphoreType.DMA((STAGE_SLOTS,))],
        compiler_params=pltpu.CompilerParams(
            dimension_semantics=("arbitrary",), vmem_limit_bytes=VMEM_LIMIT_BYTES),
        name="layer",
    )(sinks, x, pos, *resident, *weights)


def kernel(x, positions, w_in, v_ln_g, v_ln_b, w_spatial, b_spatial, sinks, w_out, ln1_g, ln1_b,
           w_ff1, w_ff2, ln2_g, ln2_b):
    batch, seq, d_model = x.shape
    depth = w_in.shape[0]
    alpha = (2.0 * depth) ** 0.25
    inv_freq = ROPE_THETA ** (-jnp.arange(0, HEAD_DIM, 2, dtype=_F32) / HEAD_DIM)
    invf = jnp.tile(inv_freq, LANES // HALF)[None, :]
    pos = positions.reshape(batch * seq // TILE_TOKENS, LANES // HALF, BLK)
    x = x.reshape(batch * seq, d_model)
    row = lambda a: a[None, :]
    for l in range(depth):
        x = _layer(x, pos, invf, sinks[l], w_in[l], row(v_ln_g[l]), row(v_ln_b[l]),
                   w_spatial[l], b_spatial[l], w_out[l], row(ln1_g[l]), row(ln1_b[l]),
                   w_ff1[l], w_ff2[l], row(ln2_g[l]), row(ln2_b[l]), alpha=alpha, seq=seq)
    return x.reshape(batch, seq, d_model)
```

```python
import functools

import jax
import jax.numpy as jnp
from jax import lax
from jax.experimental import pallas as pl
from jax.experimental.pallas import tpu as pltpu

HEAD_DIM = 64
N_GMLP_HEADS = 8
D_GMLP = N_GMLP_HEADS * HEAD_DIM
N_Q_HEADS = 8
N_KV_HEADS = 2
GQA_GROUP = N_Q_HEADS // N_KV_HEADS
D_ATTN = N_Q_HEADS * HEAD_DIM
D_KV = N_KV_HEADS * HEAD_DIM
BLK = 128
ROPE_THETA = 10000.0
LN_EPS = 1e-5
NEG_INF = -1e30
LANES = 128
HALF = HEAD_DIM // 2
MXU_WIDTH = 256

TILE_TOKENS = 512
FFN_CHUNK = 1024
STAGE_ROWS, STAGE_COLS = 256, 1024
STAGE_SLOTS = 6
STAGE_POINT_EVERY = 3
VMEM_LIMIT_BYTES = 56 * 1024 * 1024

_BF16 = jnp.bfloat16
_F32 = jnp.float32


def _dot(a, b):
    return jnp.dot(a, b, preferred_element_type=_F32)


def _dot_nt(a, b):
    return lax.dot_general(a, b, (((1,), (1,)), ((), ())), preferred_element_type=_F32)


def _layer_norm(v, g, b):
    mu = jnp.mean(v, axis=-1, keepdims=True)
    c = v - mu
    var = jnp.mean(c * c, axis=-1, keepdims=True)
    return c * lax.rsqrt(var + LN_EPS) * g + b


def _mixer_steps(seq_start, sinks_ref, x_ref, pos_ref, invf_ref, w_in_ref, vg_ref, vb_ref, wsp_ref,
                 bsp_ref, w_out_ref, g1_ref, b1_ref, x1a_buf, xb1a_buf, y_buf, kd_buf, vw_buf,
                 mix_buf, *, alpha):
    tokens = x_ref.shape[0]
    n_blk = tokens // BLK
    n_slab_g = D_GMLP // LANES
    n_slab_a = D_ATTN // LANES
    q0 = 2 * D_GMLP
    k0 = q0 + D_ATTN

    x = x_ref[...]
    xb = x.astype(_BF16)
    lane = lax.broadcasted_iota(jnp.int32, (1, LANES), 1)
    lo = lane < HEAD_DIM
    ti = lax.broadcasted_iota(jnp.int32, (BLK, BLK), 0)
    si = lax.broadcasted_iota(jnp.int32, (BLK, BLK), 1)

    def column(row):
        return jnp.sum(jnp.where(si == ti, row, 0.0), axis=1, keepdims=True)

    def project(col0, width):
        parts = []
        for c in range(col0, col0 + width, MXU_WIDTH):
            parts.append(_dot(xb, w_in_ref[:, c:c + MXU_WIDTH]))
            yield
        return parts

    qkv = yield from project(q0, D_ATTN + 2 * D_KV)
    slabs = [part[:, i * LANES:(i + 1) * LANES] for part in qkv for i in range(MXU_WIDTH // LANES)]
    q_raw, k_raw, v = slabs[:n_slab_a], slabs[n_slab_a], slabs[n_slab_a + 1]

    pos = pos_ref[...].astype(_F32)
    invf = invf_ref[...]
    n_grp = LANES // HALF
    quarter = tokens // n_grp
    assert pos.shape == (n_grp, quarter) and quarter == BLK
    grp = lane // HALF
    ang = None
    for a in range(n_grp):
        term = column(pos[a:a + 1, :]) * jnp.where(grp == a, invf, 0.0)
        ang = term if ang is None else ang + term
    cos_packed, sin_packed = jnp.cos(ang), jnp.sin(ang)

    def spread(packed):
        parts = []
        for a in range(n_grp):
            one = jnp.where(grp == a, packed, 0.0)
            two = one + pltpu.roll(one, 2 * HALF, axis=1)
            parts.append(two + pltpu.roll(two, HALF, axis=1))
        return jnp.concatenate(parts, axis=0)

    cos = spread(cos_packed)
    first_half = (lane & HALF) == 0
    sin_signed = jnp.where(first_half, -1.0, 1.0) * spread(sin_packed)

    def rope(t):
        swapped = jnp.where(first_half, pltpu.roll(t, LANES - HALF, axis=1),
                            pltpu.roll(t, HALF, axis=1))
        return t * cos + swapped * sin_signed

    scale = HEAD_DIM ** -0.5
    q_even, q_odd = [], []
    for p in range(n_slab_a):
        qs = rope(q_raw[p]) * scale
        q_even.append(jnp.where(lo, qs, 0.0).astype(_BF16))
        q_odd.append(jnp.where(lo, 0.0, qs).astype(_BF16))
    kr = rope(k_raw)
    kr_sw = pltpu.roll(kr, HEAD_DIM, axis=1)
    cur = slice(BLK, BLK + tokens)
    kd_buf[0, cur, :] = jnp.where(lo, kr, kr_sw).astype(_BF16)
    kd_buf[1, cur, :] = jnp.where(lo, kr_sw, kr).astype(_BF16)
    v_sw = pltpu.roll(v, HEAD_DIM, axis=1)
    vw_buf[0, cur, :] = jnp.where(lo, v, 0.0).astype(_BF16)
    vw_buf[1, cur, :] = jnp.where(lo, 0.0, v_sw).astype(_BF16)
    vw_buf[2, cur, :] = jnp.where(lo, v_sw, 0.0).astype(_BF16)
    vw_buf[3, cur, :] = jnp.where(lo, 0.0, v).astype(_BF16)

    u = jax.nn.gelu(jnp.concatenate((yield from project(0, D_GMLP)), axis=1))
    vgel = jax.nn.gelu(jnp.concatenate((yield from project(D_GMLP, D_GMLP)), axis=1))
    vn = _layer_norm(vgel, vg_ref[...], vb_ref[...])
    lo_g = (lax.broadcasted_iota(jnp.int32, (1, D_GMLP), 1) & HEAD_DIM) == 0
    vn_top = jnp.where(lo_g, vn, 0.0).astype(_BF16)
    vn_bot = jnp.where(lo_g, 0.0, vn).astype(_BF16)

    from_prev = si > ti
    no_prev = from_prev & (si < jnp.where(seq_start, BLK, 0))
    probs = {}
    for n in range(n_blk):
        rows = slice(n * BLK, (n + 1) * BLK)
        kv_rows = slice(n * BLK, (n + 2) * BLK)
        for g in range(N_KV_HEADS):
            heads = range(GQA_GROUP * g, GQA_GROUP * (g + 1))
            q_stack = jnp.concatenate(
                [(q_odd if h % 2 else q_even)[h // 2][rows] for h in heads], axis=0)
            scores = _dot_nt(q_stack, kd_buf[g, kv_rows, :])
            for hl, h in enumerate(heads):
                sc = scores[hl * BLK:(hl + 1) * BLK]
                sh = jnp.where(from_prev, sc[:, 0:BLK], sc[:, BLK:2 * BLK])
                if n == 0:
                    sh = jnp.where(no_prev, NEG_INF, sh)
                sink = sinks_ref[h]
                m = jnp.maximum(jnp.max(sh, axis=-1, keepdims=True), sink)
                pr = jnp.exp(sh - m)
                l = jnp.sum(pr, axis=-1, keepdims=True) + jnp.exp(sink - m)
                p_band = jnp.concatenate(
                    [jnp.where(from_prev, pr, 0.0), jnp.where(from_prev, 0.0, pr)], axis=1)
                probs[n, h] = (p_band.astype(_BF16), 1.0 / l)
        yield

    causal = si <= ti

    def gmlp_piece(p):
        w_pair = jnp.concatenate(
            [jnp.where(causal, wsp_ref[2 * p], 0.0), jnp.where(causal, wsp_ref[2 * p + 1], 0.0)],
            axis=1).astype(_BF16)
        bias = jnp.where(lo, column(bsp_ref[2 * p:2 * p + 1, :]),
                         column(bsp_ref[2 * p + 1:2 * p + 2, :]))
        for c in range(n_blk):
            rows = slice(c * BLK, (c + 1) * BLK)
            cols = slice(p * LANES, (p + 1) * LANES)
            rhs = jnp.concatenate([vn_top[rows, cols], vn_bot[rows, cols]], axis=0)
            mixed = _dot(w_pair, rhs) + bias
            mix_buf[rows, cols] = (u[rows, cols] * mixed).astype(_BF16)

    def values_piece(n):
        rows = slice(n * BLK, (n + 1) * BLK)
        kv_rows = slice(n * BLK, (n + 2) * BLK)
        for slab in range(n_slab_a):
            g = (2 * slab) // GQA_GROUP
            (p_e, inv_e), (p_o, inv_o) = probs[n, 2 * slab], probs[n, 2 * slab + 1]
            acc = _dot(p_e, vw_buf[2 * g, kv_rows, :]) + _dot(p_o, vw_buf[2 * g + 1, kv_rows, :])
            out = acc * jnp.where(lo, inv_e, inv_o)
            mix_buf[rows, D_GMLP + slab * LANES:D_GMLP + (slab + 1) * LANES] = out.astype(_BF16)

    def residual_sum(rows):
        return alpha * x[rows] + _dot(mix_buf[rows, :], w_out_ref[...])

    half_blk = n_blk // 2
    assert n_blk % 2 == 0 and n_slab_g >= half_blk
    for i in range(half_blk):
        gmlp_piece(i)
        yield
        values_piece(i)
        yield
    for p in range(half_blk, n_slab_g):
        gmlp_piece(p)
    yield
    values_piece(half_blk)
    yield
    x1_a = _layer_norm(residual_sum(slice(0, half_blk * BLK)), g1_ref[...], b1_ref[...])
    x1a_buf[...] = x1_a
    xb1a_buf[...] = x1_a.astype(_BF16)
    yield
    for n in range(half_blk + 1, n_blk):
        values_piece(n)
    last = slice(tokens, tokens + BLK)
    kd_buf[:, 0:BLK, :] = kd_buf[:, last, :]
    vw_buf[:, 0:BLK, :] = vw_buf[:, last, :]
    yield
    y_buf[...] = residual_sum(slice(half_blk * BLK, tokens))
    yield


def _ffn_steps(load_x1, rows, w1_ref, w2_ref, g2_ref, b2_ref, o_ref, *, alpha):
    x, xb = load_x1()
    acc = alpha * x
    yield
    for j in range(w1_ref.shape[1] // FFN_CHUNK):
        cols = slice(j * FFN_CHUNK, (j + 1) * FFN_CHUNK)
        h = jnp.maximum(_dot(xb, w1_ref[:, cols]), 0.0)
        yield
        acc = acc + _dot((h * h).astype(_BF16), w2_ref[cols, :])
        yield
    o_ref[rows, :] = _layer_norm(acc, g2_ref[...], b2_ref[...])


def _run(steps, count=None):
    if count is None:
        for _ in steps:
            pass
    else:
        for _ in range(count):
            next(steps)


def _stage_jobs(pairs):
    jobs = []
    for src, dst in pairs:
        assert src.shape[0] % STAGE_ROWS == 0 and src.shape[1] % LANES == 0
        for r0 in range(0, src.shape[0], STAGE_ROWS):
            for c0 in range(0, src.shape[1], STAGE_COLS):
                jobs.append((src, dst, r0, c0, min(STAGE_COLS, src.shape[1] - c0)))
    return jobs


def _stage_steps(jobs, stage, sem):
    copies = [pltpu.make_async_copy(src.at[pl.ds(r0, STAGE_ROWS), pl.ds(c0, width)],
                                    stage.at[i % STAGE_SLOTS, :, pl.ds(0, width)],
                                    sem.at[i % STAGE_SLOTS])
              for i, (src, _, r0, c0, width) in enumerate(jobs)]
    ahead = STAGE_SLOTS - 1
    for copy in copies[:ahead]:
        copy.start()
    for i, (_, dst, r0, c0, width) in enumerate(jobs):
        if i + ahead < len(jobs):
            copies[i + ahead].start()
        copies[i].wait()
        dst[r0:r0 + STAGE_ROWS, c0:c0 + width] = stage[i % STAGE_SLOTS, :, 0:width].astype(_BF16)
        yield


def _layer_kernel(sinks_ref, x_ref, pos_ref, invf_ref, vg_ref, vb_ref, wsp_ref, bsp_ref, g1_ref,
                  b1_ref, g2_ref, b2_ref, w_in_hbm, w_out_hbm, w1_hbm, w2_hbm, o_ref,
                  x1a_buf, xb1a_buf, y_buf, kd_buf, vw_buf, mix_buf, w_in_ref, w_out_ref, w1_ref,
                  w2_ref, stage_sem, *, alpha, tiles_per_seq):
    j = pl.program_id(0)
    seq_start = (j % tiles_per_seq) == 0

    @pl.when(seq_start)
    def _():
        kd_buf[:, 0:BLK, :] = jnp.zeros((N_KV_HEADS, BLK, LANES), _BF16)
        vw_buf[:, 0:BLK, :] = jnp.zeros((2 * N_KV_HEADS, BLK, LANES), _BF16)

    mixer = functools.partial(
        _mixer_steps, seq_start, sinks_ref, x_ref, pos_ref, invf_ref, w_in_ref, vg_ref, vb_ref,
        wsp_ref, bsp_ref, w_out_ref, g1_ref, b1_ref, x1a_buf, xb1a_buf, y_buf, kd_buf, vw_buf,
        mix_buf, alpha=alpha)
    half = x_ref.shape[0] // 2

    def first_half_x1():
        return x1a_buf[...], xb1a_buf[...]

    def second_half_x1():
        x1 = _layer_norm(y_buf[...], g1_ref[...], b1_ref[...])
        return x1, x1.astype(_BF16)

    loaders = [first_half_x1, second_half_x1]
    ffn = [functools.partial(_ffn_steps, loaders[i], slice(i * half, (i + 1) * half), w1_ref,
                             w2_ref, g2_ref, b2_ref, o_ref, alpha=alpha) for i in range(2)]

    @pl.when(j == 0)
    def _():
        def first_step(stage):
            _run(_stage_steps(_stage_jobs([(w_in_hbm, w_in_ref)]), stage, stage_sem))
            later = _stage_steps(
                _stage_jobs([(w_out_hbm, w_out_ref), (w1_hbm, w1_ref), (w2_hbm, w2_ref)]),
                stage, stage_sem)
            assert len(_stage_jobs([(w_out_hbm, w_out_ref)])) <= STAGE_SLOTS - 1
            for i, _ in enumerate(mixer()):
                if i % STAGE_POINT_EVERY == STAGE_POINT_EVERY - 1:
                    for _ in range(STAGE_SLOTS - 1):
                        next(later, None)
            _run(later)

        pl.run_scoped(first_step, pltpu.VMEM((STAGE_SLOTS, STAGE_ROWS, STAGE_COLS), _F32))

    @pl.when(j > 0)
    def _():
        fa, fb, m = ffn[0](), ffn[1](), mixer()
        n_blk = x_ref.shape[0] // BLK
        n_proj = w_in_ref.shape[1] // MXU_WIDTH
        ffn_pieces = 2 * (w1_ref.shape[1] // FFN_CHUNK)
        mixer_pieces = [1] * n_proj + [2] * (n_blk // 2) + [2] * (n_blk // 2) + [1] * 4
        mixer_pieces += [0] * (2 * ffn_pieces - len(mixer_pieces))
        assert len(mixer_pieces) == 2 * ffn_pieces
        _run(fa, 1)
        for count in mixer_pieces[:ffn_pieces]:
            _run(fa, 1)
            _run(m, count)
        _run(fa)
        _run(fb, 1)
        for count in mixer_pieces[ffn_pieces:]:
            _run(fb, 1)
            _run(m, count)
        _run(m)
        _run(fb)


def _resident(shape):
    return pl.BlockSpec(shape, lambda *_: (0,) * len(shape), pipeline_mode=pl.Buffered(1))


def _layer(x, pos, invf, sinks, w_in, vg, vb, wsp, bsp, w_out, g1, b1, w1, w2, g2, b2, *, alpha,
           seq):
    n_tok, d_model = x.shape
    tokens = TILE_TOKENS
    assert seq % tokens == 0 and tokens % BLK == 0 and w1.shape[1] % FFN_CHUNK == 0
    n_tiles = n_tok // tokens
    kernel = functools.partial(_layer_kernel, alpha=alpha, tiles_per_seq=seq // tokens)
    in_tile = pl.BlockSpec((tokens, d_model), lambda j: (jnp.minimum(j, n_tiles - 1), 0))
    pos_tile = pl.BlockSpec((None,) + pos.shape[1:], lambda j: (jnp.minimum(j, n_tiles - 1), 0, 0))
    out_tile = pl.BlockSpec((tokens, d_model), lambda j: (jnp.maximum(j - 1, 0), 0))
    resident = [invf, vg, vb, wsp, bsp, g1, b1, g2, b2]
    weights = [w_in, w_out, w1, w2]
    return pl.pallas_call(
        kernel,
        grid=(n_tiles + 1,),
        in_specs=[pl.BlockSpec(memory_space=pltpu.SMEM), in_tile, pos_tile]
        + [_resident(a.shape) for a in resident]
        + [pl.BlockSpec(memory_space=pl.ANY) for _ in weights],
        out_specs=out_tile,
        out_shape=jax.ShapeDtypeStruct(x.shape, x.dtype),
        scratch_shapes=[
            pltpu.VMEM((tokens // 2, d_model), _F32),
            pltpu.VMEM((tokens // 2, d_model), _BF16),
            pltpu.VMEM((tokens // 2, d_model), _F32),
            pltpu.VMEM((N_KV_HEADS, tokens + BLK, LANES), _BF16),
            pltpu.VMEM((2 * N_KV_HEADS, tokens + BLK, LANES), _BF16),
            pltpu.VMEM((tokens, D_GMLP + D_ATTN), _BF16),
        ] + [pltpu.VMEM(w.shape, _BF16) for w in weights] + [pltpu.SemaphoreType.DMA((STAGE_SLOTS,))],
        compiler_params=pltpu.CompilerParams(
            dimension_semantics=("arbitrary",), vmem_limit_bytes=VMEM_LIMIT_BYTES),
        name="layer",
    )(sinks, x, pos, *resident, *weights)


def kernel(x, positions, w_in, v_ln_g, v_ln_b, w_spatial, b_spatial, sinks, w_out, ln1_g, ln1_b,
           w_ff1, w_ff2, ln2_g, ln2_b):
    batch, seq, d_model = x.shape
    depth = w_in.shape[0]
    alpha = (2.0 * depth) ** 0.25
    inv_freq = ROPE_THETA ** (-jnp.arange(0, HEAD_DIM, 2, dtype=_F32) / HEAD_DIM)
    invf = jnp.tile(inv_freq, LANES // HALF)[None, :]
    pos = positions.reshape(batch * seq // TILE_TOKENS, LANES // HALF, BLK)
    x = x.reshape(batch * seq, d_model)
    row = lambda a: a[None, :]
    for l in range(depth):
        x = _layer(x, pos, invf, sinks[l], w_in[l], row(v_ln_g[l]), row(v_ln_b[l]),
                   w_spatial[l], b_spatial[l], w_out[l], row(ln1_g[l]), row(ln1_b[l]),
                   w_ff1[l], w_ff2[l], row(ln2_g[l]), row(ln2_b[l]), alpha=alpha, seq=seq)
    return x.reshape(batch, seq, d_model)
```

```python
import functools

import jax
import jax.numpy as jnp
from jax import lax
from jax.experimental import pallas as pl
from jax.experimental.pallas import tpu as pltpu

HEAD_DIM = 64
N_GMLP_HEADS = 8
D_GMLP = N_GMLP_HEADS * HEAD_DIM
N_Q_HEADS = 8
N_KV_HEADS = 2
GQA_GROUP = N_Q_HEADS // N_KV_HEADS
D_ATTN = N_Q_HEADS * HEAD_DIM
D_KV = N_KV_HEADS * HEAD_DIM
BLK = 128
ROPE_THETA = 10000.0
LN_EPS = 1e-5
NEG_INF = -1e30
LANES = 128
HALF = HEAD_DIM // 2
MXU_WIDTH = 256

TILE_TOKENS = 512
FFN_CHUNK = 1024
STAGE_ROWS, STAGE_COLS = 256, 1024
STAGE_SLOTS = 10
STAGE_POINT_EVERY = 5
MIXER_LEAD_PIECES = 4
VMEM_LIMIT_BYTES = 56 * 1024 * 1024

_BF16 = jnp.bfloat16
_F32 = jnp.float32


def _dot(a, b):
    return jnp.dot(a, b, preferred_element_type=_F32)


def _dot_nt(a, b):
    return lax.dot_general(a, b, (((1,), (1,)), ((), ())), preferred_element_type=_F32)


def _layer_norm(v, g, b):
    mu = jnp.mean(v, axis=-1, keepdims=True)
    c = v - mu
    var = jnp.mean(c * c, axis=-1, keepdims=True)
    return c * lax.rsqrt(var + LN_EPS) * g + b


def _mixer_steps(seq_start, sinks_ref, x_ref, pos_ref, invf_ref, w_in_ref, vg_ref, vb_ref, wsp_ref,
                 bsp_ref, w_out_ref, y_buf, kd_buf, vw_buf, mix_buf, *, alpha):
    tokens = x_ref.shape[0]
    n_blk = tokens // BLK
    n_slab_g = D_GMLP // LANES
    n_slab_a = D_ATTN // LANES
    q0 = 2 * D_GMLP
    k0 = q0 + D_ATTN

    x = x_ref[...]
    xb = x.astype(_BF16)
    lane = lax.broadcasted_iota(jnp.int32, (1, LANES), 1)
    lo = lane < HEAD_DIM
    ti = lax.broadcasted_iota(jnp.int32, (BLK, BLK), 0)
    si = lax.broadcasted_iota(jnp.int32, (BLK, BLK), 1)

    def column(row):
        return jnp.sum(jnp.where(si == ti, row, 0.0), axis=1, keepdims=True)

    def project(col0, width):
        parts = []
        for c in range(col0, col0 + width, MXU_WIDTH):
            parts.append(_dot(xb, w_in_ref[:, c:c + MXU_WIDTH]))
            yield
        return parts

    qkv = yield from project(q0, D_ATTN + 2 * D_KV)
    slabs = [part[:, i * LANES:(i + 1) * LANES] for part in qkv for i in range(MXU_WIDTH // LANES)]
    q_raw, k_raw, v = slabs[:n_slab_a], slabs[n_slab_a], slabs[n_slab_a + 1]

    pos = pos_ref[...].astype(_F32)
    invf = invf_ref[...]
    n_grp = LANES // HALF
    quarter = tokens // n_grp
    assert pos.shape == (n_grp, quarter) and quarter == BLK
    grp = lane // HALF
    ang = None
    for a in range(n_grp):
        term = column(pos[a:a + 1, :]) * jnp.where(grp == a, invf, 0.0)
        ang = term if ang is None else ang + term
    cos_packed, sin_packed = jnp.cos(ang), jnp.sin(ang)

    def spread(packed):
        parts = []
        for a in range(n_grp):
            one = jnp.where(grp == a, packed, 0.0)
            two = one + pltpu.roll(one, 2 * HALF, axis=1)
            parts.append(two + pltpu.roll(two, HALF, axis=1))
        return jnp.concatenate(parts, axis=0)

    cos = spread(cos_packed)
    first_half = (lane & HALF) == 0
    sin_signed = jnp.where(first_half, -1.0, 1.0) * spread(sin_packed)

    def rope(t):
        swapped = jnp.where(first_half, pltpu.roll(t, LANES - HALF, axis=1),
                            pltpu.roll(t, HALF, axis=1))
        return t * cos + swapped * sin_signed

    scale = HEAD_DIM ** -0.5
    q_even, q_odd = [], []
    for p in range(n_slab_a):
        qs = rope(q_raw[p]) * scale
        q_even.append(jnp.where(lo, qs, 0.0).astype(_BF16))
        q_odd.append(jnp.where(lo, 0.0, qs).astype(_BF16))
    kr = rope(k_raw)
    kr_sw = pltpu.roll(kr, HEAD_DIM, axis=1)
    cur = slice(BLK, BLK + tokens)
    kd_buf[0, cur, :] = jnp.where(lo, kr, kr_sw).astype(_BF16)
    kd_buf[1, cur, :] = jnp.where(lo, kr_sw, kr).astype(_BF16)
    v_sw = pltpu.roll(v, HEAD_DIM, axis=1)
    vw_buf[0, cur, :] = jnp.where(lo, v, 0.0).astype(_BF16)
    vw_buf[1, cur, :] = jnp.where(lo, 0.0, v_sw).astype(_BF16)
    vw_buf[2, cur, :] = jnp.where(lo, v_sw, 0.0).astype(_BF16)
    vw_buf[3, cur, :] = jnp.where(lo, 0.0, v).astype(_BF16)

    u = jax.nn.gelu(jnp.concatenate((yield from project(0, D_GMLP)), axis=1))
    vgel = jax.nn.gelu(jnp.concatenate((yield from project(D_GMLP, D_GMLP)), axis=1))
    vn = _layer_norm(vgel, vg_ref[...], vb_ref[...])
    lo_g = (lax.broadcasted_iota(jnp.int32, (1, D_GMLP), 1) & HEAD_DIM) == 0
    vn_top = jnp.where(lo_g, vn, 0.0).astype(_BF16)
    vn_bot = jnp.where(lo_g, 0.0, vn).astype(_BF16)

    from_prev = si > ti
    no_prev = from_prev & (si < jnp.where(seq_start, BLK, 0))
    probs = {}
    for n in range(n_blk):
        rows = slice(n * BLK, (n + 1) * BLK)
        kv_rows = slice(n * BLK, (n + 2) * BLK)
        for g in range(N_KV_HEADS):
            heads = range(GQA_GROUP * g, GQA_GROUP * (g + 1))
            q_stack = jnp.concatenate(
                [(q_odd if h % 2 else q_even)[h // 2][rows] for h in heads], axis=0)
            scores = _dot_nt(q_stack, kd_buf[g, kv_rows, :])
            for hl, h in enumerate(heads):
                sc = scores[hl * BLK:(hl + 1) * BLK]
                sh = jnp.where(from_prev, sc[:, 0:BLK], sc[:, BLK:2 * BLK])
                if n == 0:
                    sh = jnp.where(no_prev, NEG_INF, sh)
                sink = sinks_ref[h]
                m = jnp.maximum(jnp.max(sh, axis=-1, keepdims=True), sink)
                pr = jnp.exp(sh - m)
                l = jnp.sum(pr, axis=-1, keepdims=True) + jnp.exp(sink - m)
                p_band = jnp.concatenate(
                    [jnp.where(from_prev, pr, 0.0), jnp.where(from_prev, 0.0, pr)], axis=1)
                probs[n, h] = (p_band.astype(_BF16), 1.0 / l)
        yield

    causal = si <= ti

    def gmlp_piece(p):
        w_pair = jnp.concatenate(
            [jnp.where(causal, wsp_ref[2 * p], 0.0), jnp.where(causal, wsp_ref[2 * p + 1], 0.0)],
            axis=1).astype(_BF16)
        bias = jnp.where(lo, column(bsp_ref[2 * p:2 * p + 1, :]),
                         column(bsp_ref[2 * p + 1:2 * p + 2, :]))
        for c in range(n_blk):
            rows = slice(c * BLK, (c + 1) * BLK)
            cols = slice(p * LANES, (p + 1) * LANES)
            rhs = jnp.concatenate([vn_top[rows, cols], vn_bot[rows, cols]], axis=0)
            mixed = _dot(w_pair, rhs) + bias
            mix_buf[rows, cols] = (u[rows, cols] * mixed).astype(_BF16)

    def values_piece(n):
        rows = slice(n * BLK, (n + 1) * BLK)
        kv_rows = slice(n * BLK, (n + 2) * BLK)
        for slab in range(n_slab_a):
            g = (2 * slab) // GQA_GROUP
            (p_e, inv_e), (p_o, inv_o) = probs[n, 2 * slab], probs[n, 2 * slab + 1]
            acc = _dot(p_e, vw_buf[2 * g, kv_rows, :]) + _dot(p_o, vw_buf[2 * g + 1, kv_rows, :])
            out = acc * jnp.where(lo, inv_e, inv_o)
            mix_buf[rows, D_GMLP + slab * LANES:D_GMLP + (slab + 1) * LANES] = out.astype(_BF16)

    for i in range(max(n_slab_g, n_blk)):
        if i < n_slab_g:
            gmlp_piece(i)
            yield
        if i < n_blk:
            values_piece(i)
            yield
    last = slice(tokens, tokens + BLK)
    kd_buf[:, 0:BLK, :] = kd_buf[:, last, :]
    vw_buf[:, 0:BLK, :] = vw_buf[:, last, :]

    mix = mix_buf[...]
    for c in range(0, x.shape[1], MXU_WIDTH):
        y_buf[:, c:c + MXU_WIDTH] = alpha * x[:, c:c + MXU_WIDTH] + _dot(mix, w_out_ref[:, c:c + MXU_WIDTH])
        yield


def _ffn_steps(rows, y_buf, g1_ref, b1_ref, w1_ref, w2_ref, g2_ref, b2_ref, o_ref, *, alpha):
    x = _layer_norm(y_buf[rows, :], g1_ref[...], b1_ref[...])
    xb = x.astype(_BF16)
    acc = alpha * x
    yield
    for j in range(w1_ref.shape[1] // FFN_CHUNK):
        cols = slice(j * FFN_CHUNK, (j + 1) * FFN_CHUNK)
        h = jnp.maximum(_dot(xb, w1_ref[:, cols]), 0.0)
        yield
        acc = acc + _dot((h * h).astype(_BF16), w2_ref[cols, :])
        yield
    o_ref[rows, :] = _layer_norm(acc, g2_ref[...], b2_ref[...])


def _run(steps, count=None):
    if count is None:
        for _ in steps:
            pass
    else:
        for _ in range(count):
            next(steps)


def _stage_jobs(pairs):
    jobs = []
    for src, dst in pairs:
        assert src.shape[0] % STAGE_ROWS == 0 and src.shape[1] % LANES == 0
        for r0 in range(0, src.shape[0], STAGE_ROWS):
            for c0 in range(0, src.shape[1], STAGE_COLS):
                jobs.append((src, dst, r0, c0, min(STAGE_COLS, src.shape[1] - c0)))
    return jobs


def _stage_steps(jobs, stage, sem):
    copies = [pltpu.make_async_copy(src.at[pl.ds(r0, STAGE_ROWS), pl.ds(c0, width)],
                                    stage.at[i % STAGE_SLOTS, :, pl.ds(0, width)],
                                    sem.at[i % STAGE_SLOTS])
              for i, (src, _, r0, c0, width) in enumerate(jobs)]
    ahead = STAGE_SLOTS - 1
    for copy in copies[:ahead]:
        copy.start()
    for i, (_, dst, r0, c0, width) in enumerate(jobs):
        if i + ahead < len(jobs):
            copies[i + ahead].start()
        copies[i].wait()
        dst[r0:r0 + STAGE_ROWS, c0:c0 + width] = stage[i % STAGE_SLOTS, :, 0:width].astype(_BF16)
        yield


def _layer_kernel(sinks_ref, x_ref, pos_ref, invf_ref, vg_ref, vb_ref, wsp_ref, bsp_ref, g1_ref,
                  b1_ref, g2_ref, b2_ref, w_in_hbm, w_out_hbm, w1_hbm, w2_hbm, o_ref,
                  y_buf, kd_buf, vw_buf, mix_buf, w_in_ref, w_out_ref, w1_ref, w2_ref, stage_sem,
                  *, alpha, tiles_per_seq):
    j = pl.program_id(0)
    seq_start = (j % tiles_per_seq) == 0

    @pl.when(seq_start)
    def _():
        kd_buf[:, 0:BLK, :] = jnp.zeros((N_KV_HEADS, BLK, LANES), _BF16)
        vw_buf[:, 0:BLK, :] = jnp.zeros((2 * N_KV_HEADS, BLK, LANES), _BF16)

    mixer = functools.partial(
        _mixer_steps, seq_start, sinks_ref, x_ref, pos_ref, invf_ref, w_in_ref, vg_ref, vb_ref,
        wsp_ref, bsp_ref, w_out_ref, y_buf, kd_buf, vw_buf, mix_buf, alpha=alpha)
    half = x_ref.shape[0] // 2
    ffn = [functools.partial(_ffn_steps, slice(i * half, (i + 1) * half), y_buf, g1_ref, b1_ref,
                             w1_ref, w2_ref, g2_ref, b2_ref, o_ref, alpha=alpha) for i in range(2)]

    @pl.when(j == 0)
    def _():
        def first_step(stage):
            _run(_stage_steps(_stage_jobs([(w_in_hbm, w_in_ref)]), stage, stage_sem))
            later = _stage_steps(
                _stage_jobs([(w_out_hbm, w_out_ref), (w1_hbm, w1_ref), (w2_hbm, w2_ref)]),
                stage, stage_sem)
            assert len(_stage_jobs([(w_out_hbm, w_out_ref)])) <= STAGE_SLOTS - 1
            for i, _ in enumerate(mixer()):
                if i % STAGE_POINT_EVERY == STAGE_POINT_EVERY - 1:
                    for _ in range(STAGE_SLOTS - 1):
                        next(later, None)
            _run(later)

        pl.run_scoped(first_step, pltpu.VMEM((STAGE_SLOTS, STAGE_ROWS, STAGE_COLS), _F32))

    @pl.when(j > 0)
    def _():
        fa, fb, m = ffn[0](), ffn[1](), mixer()
        n_blk = x_ref.shape[0] // BLK
        n_proj = w_in_ref.shape[1] // MXU_WIDTH
        ffn_pieces = 2 * (w1_ref.shape[1] // FFN_CHUNK)
        mixer_pieces = [MIXER_LEAD_PIECES] + [1] * (n_proj + n_blk - MIXER_LEAD_PIECES) + [2] * n_blk
        mixer_pieces += [0] * (2 * ffn_pieces - len(mixer_pieces))
        assert len(mixer_pieces) == 2 * ffn_pieces
        _run(fa, 1)
        for count in mixer_pieces[:ffn_pieces]:
            _run(m, count)
            _run(fa, 1)
        _run(fa)
        _run(fb, 1)
        for count in mixer_pieces[ffn_pieces:]:
            _run(m, count)
            _run(fb, 1)
        _run(m)
        _run(fb)


def _resident(shape):
    return pl.BlockSpec(shape, lambda *_: (0,) * len(shape), pipeline_mode=pl.Buffered(1))


def _layer(x, pos, invf, sinks, w_in, vg, vb, wsp, bsp, w_out, g1, b1, w1, w2, g2, b2, *, alpha,
           seq):
    n_tok, d_model = x.shape
    tokens = TILE_TOKENS
    assert seq % tokens == 0 and tokens % BLK == 0 and w1.shape[1] % FFN_CHUNK == 0
    n_tiles = n_tok // tokens
    kernel = functools.partial(_layer_kernel, alpha=alpha, tiles_per_seq=seq // tokens)
    in_tile = pl.BlockSpec((tokens, d_model), lambda j: (jnp.minimum(j, n_tiles - 1), 0))
    pos_tile = pl.BlockSpec((None,) + pos.shape[1:], lambda j: (jnp.minimum(j, n_tiles - 1), 0, 0))
    out_tile = pl.BlockSpec((tokens, d_model), lambda j: (jnp.maximum(j - 1, 0), 0))
    resident = [invf, vg, vb, wsp, bsp, g1, b1, g2, b2]
    weights = [w_in, w_out, w1, w2]
    return pl.pallas_call(
        kernel,
        grid=(n_tiles + 1,),
        in_specs=[pl.BlockSpec(memory_space=pltpu.SMEM), in_tile, pos_tile]
        + [_resident(a.shape) for a in resident]
        + [pl.BlockSpec(memory_space=pl.ANY) for _ in weights],
        out_specs=out_tile,
        out_shape=jax.ShapeDtypeStruct(x.shape, x.dtype),
        scratch_shapes=[
            pltpu.VMEM((tokens, d_model), _F32),
            pltpu.VMEM((N_KV_HEADS, tokens + BLK, LANES), _BF16),
            pltpu.VMEM((2 * N_KV_HEADS, tokens + BLK, LANES), _BF16),
            pltpu.VMEM((tokens, D_GMLP + D_ATTN), _BF16),
        ] + [pltpu.VMEM(w.shape, _BF16) for w in weights] + [pltpu.SemaphoreType.DMA((STAGE_SLOTS,))],
        compiler_params=pltpu.CompilerParams(
            dimension_semantics=("arbitrary",), vmem_limit_bytes=VMEM_LIMIT_BYTES),
        name="layer",
    )(sinks, x, pos, *resident, *weights)


def kernel(x, positions, w_in, v_ln_g, v_ln_b, w_spatial, b_spatial, sinks, w_out, ln1_g, ln1_b,
           w_ff1, w_ff2, ln2_g, ln2_b):
    batch, seq, d_model = x.shape
    depth = w_in.shape[0]
    alpha = (2.0 * depth) ** 0.25
    inv_freq = ROPE_THETA ** (-jnp.arange(0, HEAD_DIM, 2, dtype=_F32) / HEAD_DIM)
    invf = jnp.tile(inv_freq, LANES // HALF)[None, :]
    pos = positions.reshape(batch * seq // TILE_TOKENS, LANES // HALF, BLK)
    x = x.reshape(batch * seq, d_model)
    row = lambda a: a[None, :]
    for l in range(depth):
        x = _layer(x, pos, invf, sinks[l], w_in[l], row(v_ln_g[l]), row(v_ln_b[l]),
                   w_spatial[l], b_spatial[l], w_out[l], row(ln1_g[l]), row(ln1_b[l]),
                   w_ff1[l], w_ff2[l], row(ln2_g[l]), row(ln2_b[l]), alpha=alpha, seq=seq)
    return x.reshape(batch, seq, d_model)
```

```python
import functools

import jax
import jax.numpy as jnp
from jax import lax
from jax.experimental import pallas as pl
from jax.experimental.pallas import tpu as pltpu

HEAD_DIM = 64
N_GMLP_HEADS = 8
D_GMLP = N_GMLP_HEADS * HEAD_DIM
N_Q_HEADS = 8
N_KV_HEADS = 2
GQA_GROUP = N_Q_HEADS // N_KV_HEADS
D_ATTN = N_Q_HEADS * HEAD_DIM
D_KV = N_KV_HEADS * HEAD_DIM
BLK = 128
ROPE_THETA = 10000.0
LN_EPS = 1e-5
NEG_INF = -1e30
LANES = 128
HALF = HEAD_DIM // 2
MXU_WIDTH = 256

TILE_TOKENS = 512
FFN_CHUNK = 1024
STAGE_ROWS, STAGE_COLS = 256, 1024
STAGE_SLOTS = 6
STAGE_POINT_EVERY = 3
MIXER_LEAD_PIECES = 4
VMEM_LIMIT_BYTES = 56 * 1024 * 1024

_BF16 = jnp.bfloat16
_F32 = jnp.float32


def _dot(a, b):
    return jnp.dot(a, b, preferred_element_type=_F32)


def _dot_nt(a, b):
    return lax.dot_general(a, b, (((1,), (1,)), ((), ())), preferred_element_type=_F32)


def _layer_norm(v, g, b):
    mu = jnp.mean(v, axis=-1, keepdims=True)
    c = v - mu
    var = jnp.mean(c * c, axis=-1, keepdims=True)
    return c * lax.rsqrt(var + LN_EPS) * g + b


def _mixer_steps(seq_start, sinks_ref, x_ref, pos_ref, invf_ref, w_in_ref, vg_ref, vb_ref, wsp_ref,
                 bsp_ref, w_out_ref, y_buf, kd_buf, vw_buf, mix_buf, *, alpha):
    tokens = x_ref.shape[0]
    n_blk = tokens // BLK
    n_slab_g = D_GMLP // LANES
    n_slab_a = D_ATTN // LANES
    q0 = 2 * D_GMLP
    k0 = q0 + D_ATTN

    x = x_ref[...]
    xb = x.astype(_BF16)
    lane = lax.broadcasted_iota(jnp.int32, (1, LANES), 1)
    lo = lane < HEAD_DIM
    ti = lax.broadcasted_iota(jnp.int32, (BLK, BLK), 0)
    si = lax.broadcasted_iota(jnp.int32, (BLK, BLK), 1)

    def column(row):
        return jnp.sum(jnp.where(si == ti, row, 0.0), axis=1, keepdims=True)

    def project(col0, width):
        parts = []
        for c in range(col0, col0 + width, MXU_WIDTH):
            parts.append(_dot(xb, w_in_ref[:, c:c + MXU_WIDTH]))
            yield
        return parts

    qkv = yield from project(q0, D_ATTN + 2 * D_KV)
    slabs = [part[:, i * LANES:(i + 1) * LANES] for part in qkv for i in range(MXU_WIDTH // LANES)]
    q_raw, k_raw, v = slabs[:n_slab_a], slabs[n_slab_a], slabs[n_slab_a + 1]

    pos = pos_ref[...].astype(_F32)
    invf = invf_ref[...]
    n_grp = LANES // HALF
    quarter = tokens // n_grp
    assert pos.shape == (n_grp, quarter) and quarter == BLK
    grp = lane // HALF
    ang = None
    for a in range(n_grp):
        term = column(pos[a:a + 1, :]) * jnp.where(grp == a, invf, 0.0)
        ang = term if ang is None else ang + term
    cos_packed, sin_packed = jnp.cos(ang), jnp.sin(ang)

    def spread(packed):
        parts = []
        for a in range(n_grp):
            one = jnp.where(grp == a, packed, 0.0)
            two = one + pltpu.roll(one, 2 * HALF, axis=1)
            parts.append(two + pltpu.roll(two, HALF, axis=1))
        return jnp.concatenate(parts, axis=0)

    cos = spread(cos_packed)
    first_half = (lane & HALF) == 0
    sin_signed = jnp.where(first_half, -1.0, 1.0) * spread(sin_packed)

    def rope(t):
        swapped = jnp.where(first_half, pltpu.roll(t, LANES - HALF, axis=1),
                            pltpu.roll(t, HALF, axis=1))
        return t * cos + swapped * sin_signed

    scale = HEAD_DIM ** -0.5
    q_even, q_odd = [], []
    for p in range(n_slab_a):
        qs = rope(q_raw[p]) * scale
        q_even.append(jnp.where(lo, qs, 0.0).astype(_BF16))
        q_odd.append(jnp.where(lo, 0.0, qs).astype(_BF16))
    kr = rope(k_raw)
    kr_sw = pltpu.roll(kr, HEAD_DIM, axis=1)
    cur = slice(BLK, BLK + tokens)
    kd_buf[0, cur, :] = jnp.where(lo, kr, kr_sw).astype(_BF16)
    kd_buf[1, cur, :] = jnp.where(lo, kr_sw, kr).astype(_BF16)
    v_sw = pltpu.roll(v, HEAD_DIM, axis=1)
    vw_buf[0, cur, :] = jnp.where(lo, v, 0.0).astype(_BF16)
    vw_buf[1, cur, :] = jnp.where(lo, 0.0, v_sw).astype(_BF16)
    vw_buf[2, cur, :] = jnp.where(lo, v_sw, 0.0).astype(_BF16)
    vw_buf[3, cur, :] = jnp.where(lo, 0.0, v).astype(_BF16)

    u = jax.nn.gelu(jnp.concatenate((yield from project(0, D_GMLP)), axis=1))
    vgel = jax.nn.gelu(jnp.concatenate((yield from project(D_GMLP, D_GMLP)), axis=1))
    vn = _layer_norm(vgel, vg_ref[...], vb_ref[...])
    lo_g = (lax.broadcasted_iota(jnp.int32, (1, D_GMLP), 1) & HEAD_DIM) == 0
    vn_top = jnp.where(lo_g, vn, 0.0).astype(_BF16)
    vn_bot = jnp.where(lo_g, 0.0, vn).astype(_BF16)

    from_prev = si > ti
    no_prev = from_prev & (si < jnp.where(seq_start, BLK, 0))
    probs = {}
    for n in range(n_blk):
        rows = slice(n * BLK, (n + 1) * BLK)
        kv_rows = slice(n * BLK, (n + 2) * BLK)
        for g in range(N_KV_HEADS):
            heads = range(GQA_GROUP * g, GQA_GROUP * (g + 1))
            q_stack = jnp.concatenate(
                [(q_odd if h % 2 else q_even)[h // 2][rows] for h in heads], axis=0)
            scores = _dot_nt(q_stack, kd_buf[g, kv_rows, :])
            for hl, h in enumerate(heads):
                sc = scores[hl * BLK:(hl + 1) * BLK]
                sh = jnp.where(from_prev, sc[:, 0:BLK], sc[:, BLK:2 * BLK])
                if n == 0:
                    sh = jnp.where(no_prev, NEG_INF, sh)
                sink = sinks_ref[h]
                m = jnp.maximum(jnp.max(sh, axis=-1, keepdims=True), sink)
                pr = jnp.exp(sh - m)
                l = jnp.sum(pr, axis=-1, keepdims=True) + jnp.exp(sink - m)
                p_band = jnp.concatenate(
                    [jnp.where(from_prev, pr, 0.0), jnp.where(from_prev, 0.0, pr)], axis=1)
                probs[n, h] = (p_band.astype(_BF16), 1.0 / l)
        yield

    causal = si <= ti

    def gmlp_piece(p):
        w_pair = jnp.concatenate(
            [jnp.where(causal, wsp_ref[2 * p], 0.0), jnp.where(causal, wsp_ref[2 * p + 1], 0.0)],
            axis=1).astype(_BF16)
        bias = jnp.where(lo, column(bsp_ref[2 * p:2 * p + 1, :]),
                         column(bsp_ref[2 * p + 1:2 * p + 2, :]))
        for c in range(n_blk):
            rows = slice(c * BLK, (c + 1) * BLK)
            cols = slice(p * LANES, (p + 1) * LANES)
            rhs = jnp.concatenate([vn_top[rows, cols], vn_bot[rows, cols]], axis=0)
            mixed = _dot(w_pair, rhs) + bias
            mix_buf[rows, cols] = (u[rows, cols] * mixed).astype(_BF16)

    def values_piece(n):
        rows = slice(n * BLK, (n + 1) * BLK)
        kv_rows = slice(n * BLK, (n + 2) * BLK)
        for slab in range(n_slab_a):
            g = (2 * slab) // GQA_GROUP
            (p_e, inv_e), (p_o, inv_o) = probs[n, 2 * slab], probs[n, 2 * slab + 1]
            acc = _dot(p_e, vw_buf[2 * g, kv_rows, :]) + _dot(p_o, vw_buf[2 * g + 1, kv_rows, :])
            out = acc * jnp.where(lo, inv_e, inv_o)
            mix_buf[rows, D_GMLP + slab * LANES:D_GMLP + (slab + 1) * LANES] = out.astype(_BF16)

    for i in range(max(n_slab_g, n_blk)):
        if i < n_slab_g:
            gmlp_piece(i)
            yield
        if i < n_blk:
            values_piece(i)
            yield
    last = slice(tokens, tokens + BLK)
    kd_buf[:, 0:BLK, :] = kd_buf[:, last, :]
    vw_buf[:, 0:BLK, :] = vw_buf[:, last, :]

    mix = mix_buf[...]
    for c in range(0, x.shape[1], MXU_WIDTH):
        y_buf[:, c:c + MXU_WIDTH] = alpha * x[:, c:c + MXU_WIDTH] + _dot(mix, w_out_ref[:, c:c + MXU_WIDTH])
        yield


def _ffn_steps(rows, y_buf, g1_ref, b1_ref, w1_ref, w2_ref, g2_ref, b2_ref, o_ref, *, alpha):
    x = _layer_norm(y_buf[rows, :], g1_ref[...], b1_ref[...])
    xb = x.astype(_BF16)
    acc = alpha * x
    yield
    for j in range(w1_ref.shape[1] // FFN_CHUNK):
        cols = slice(j * FFN_CHUNK, (j + 1) * FFN_CHUNK)
        h = jnp.maximum(_dot(xb, w1_ref[:, cols]), 0.0)
        yield
        acc = acc + _dot((h * h).astype(_BF16), w2_ref[cols, :])
        yield
    o_ref[rows, :] = _layer_norm(acc, g2_ref[...], b2_ref[...])


def _run(steps, count=None):
    if count is None:
        for _ in steps:
            pass
    else:
        for _ in range(count):
            next(steps)


def _stage_jobs(pairs):
    jobs = []
    for src, dst in pairs:
        assert src.shape[0] % STAGE_ROWS == 0 and src.shape[1] % LANES == 0
        for r0 in range(0, src.shape[0], STAGE_ROWS):
            for c0 in range(0, src.shape[1], STAGE_COLS):
                jobs.append((src, dst, r0, c0, min(STAGE_COLS, src.shape[1] - c0)))
    return jobs


def _stage_steps(jobs, stage, sem):
    copies = [pltpu.make_async_copy(src.at[pl.ds(r0, STAGE_ROWS), pl.ds(c0, width)],
                                    stage.at[i % STAGE_SLOTS, :, pl.ds(0, width)],
                                    sem.at[i % STAGE_SLOTS])
              for i, (src, _, r0, c0, width) in enumerate(jobs)]
    ahead = STAGE_SLOTS - 1
    for copy in copies[:ahead]:
        copy.start()
    for i, (_, dst, r0, c0, width) in enumerate(jobs):
        if i + ahead < len(jobs):
            copies[i + ahead].start()
        copies[i].wait()
        dst[r0:r0 + STAGE_ROWS, c0:c0 + width] = stage[i % STAGE_SLOTS, :, 0:width].astype(_BF16)
        yield


def _layer_kernel(sinks_ref, x_ref, pos_ref, invf_ref, vg_ref, vb_ref, wsp_ref, bsp_ref, g1_ref,
                  b1_ref, g2_ref, b2_ref, w_in_hbm, w_out_hbm, w1_hbm, w2_hbm, o_ref,
                  y_buf, kd_buf, vw_buf, mix_buf, w_in_ref, w_out_ref, w1_ref, w2_ref, stage_sem,
                  *, alpha, tiles_per_seq):
    j = pl.program_id(0)
    seq_start = (j % tiles_per_seq) == 0

    @pl.when(seq_start)
    def _():
        kd_buf[:, 0:BLK, :] = jnp.zeros((N_KV_HEADS, BLK, LANES), _BF16)
        vw_buf[:, 0:BLK, :] = jnp.zeros((2 * N_KV_HEADS, BLK, LANES), _BF16)

    mixer = functools.partial(
        _mixer_steps, seq_start, sinks_ref, x_ref, pos_ref, invf_ref, w_in_ref, vg_ref, vb_ref,
        wsp_ref, bsp_ref, w_out_ref, y_buf, kd_buf, vw_buf, mix_buf, alpha=alpha)
    half = x_ref.shape[0] // 2
    ffn = [functools.partial(_ffn_steps, slice(i * half, (i + 1) * half), y_buf, g1_ref, b1_ref,
                             w1_ref, w2_ref, g2_ref, b2_ref, o_ref, alpha=alpha) for i in range(2)]

    @pl.when(j == 0)
    def _():
        def first_step(stage):
            _run(_stage_steps(_stage_jobs([(w_in_hbm, w_in_ref)]), stage, stage_sem))
            later = _stage_steps(
                _stage_jobs([(w_out_hbm, w_out_ref), (w1_hbm, w1_ref), (w2_hbm, w2_ref)]),
                stage, stage_sem)
            assert len(_stage_jobs([(w_out_hbm, w_out_ref)])) <= STAGE_SLOTS - 1
            for i, _ in enumerate(mixer()):
                if i % STAGE_POINT_EVERY == STAGE_POINT_EVERY - 1:
                    for _ in range(STAGE_SLOTS - 1):
                        next(later, None)
            _run(later)

        pl.run_scoped(first_step, pltpu.VMEM((STAGE_SLOTS, STAGE_ROWS, STAGE_COLS), _F32))

    @pl.when(j > 0)
    def _():
        fa, fb, m = ffn[0](), ffn[1](), mixer()
        n_blk = x_ref.shape[0] // BLK
        n_proj = w_in_ref.shape[1] // MXU_WIDTH
        ffn_pieces = 2 * (w1_ref.shape[1] // FFN_CHUNK)
        mixer_pieces = [MIXER_LEAD_PIECES] + [1] * (n_proj + n_blk - MIXER_LEAD_PIECES) + [2] * n_blk
        mixer_pieces += [0] * (2 * ffn_pieces - len(mixer_pieces))
        assert len(mixer_pieces) == 2 * ffn_pieces
        _run(fa, 1)
        for count in mixer_pieces[:ffn_pieces]:
            _run(m, count)
            _run(fa, 1)
        _run(fa)
        _run(fb, 1)
        for count in mixer_pieces[ffn_pieces:]:
            _run(m, count)
            _run(fb, 1)
        _run(m)
        _run(fb)


def _resident(shape):
    return pl.BlockSpec(shape, lambda *_: (0,) * len(shape), pipeline_mode=pl.Buffered(1))


def _layer(x, pos, invf, sinks, w_in, vg, vb, wsp, bsp, w_out, g1, b1, w1, w2, g2, b2, *, alpha,
           seq):
    n_tok, d_model = x.shape
    tokens = TILE_TOKENS
    assert seq % tokens == 0 and tokens % BLK == 0 and w1.shape[1] % FFN_CHUNK == 0
    n_tiles = n_tok // tokens
    kernel = functools.partial(_layer_kernel, alpha=alpha, tiles_per_seq=seq // tokens)
    in_tile = pl.BlockSpec((tokens, d_model), lambda j: (jnp.minimum(j, n_tiles - 1), 0))
    pos_tile = pl.BlockSpec((None,) + pos.shape[1:], lambda j: (jnp.minimum(j, n_tiles - 1), 0, 0))
    out_tile = pl.BlockSpec((tokens, d_model), lambda j: (jnp.maximum(j - 1, 0), 0))
    resident = [invf, vg, vb, wsp, bsp, g1, b1, g2, b2]
    weights = [w_in, w_out, w1, w2]
    return pl.pallas_call(
        kernel,
        grid=(n_tiles + 1,),
        in_specs=[pl.BlockSpec(memory_space=pltpu.SMEM), in_tile, pos_tile]
        + [_resident(a.shape) for a in resident]
        + [pl.BlockSpec(memory_space=pl.ANY) for _ in weights],
        out_specs=out_tile,
        out_shape=jax.ShapeDtypeStruct(x.shape, x.dtype),
        scratch_shapes=[
            pltpu.VMEM((tokens, d_model), _F32),
            pltpu.VMEM((N_KV_HEADS, tokens + BLK, LANES), _BF16),
            pltpu.VMEM((2 * N_KV_HEADS, tokens + BLK, LANES), _BF16),
            pltpu.VMEM((tokens, D_GMLP + D_ATTN), _BF16),
        ] + [pltpu.VMEM(w.shape, _BF16) for w in weights] + [pltpu.SemaphoreType.DMA((STAGE_SLOTS,))],
        compiler_params=pltpu.CompilerParams(
            dimension_semantics=("arbitrary",), vmem_limit_bytes=VMEM_LIMIT_BYTES),
        name="layer",
    )(sinks, x, pos, *resident, *weights)


def kernel(x, positions, w_in, v_ln_g, v_ln_b, w_spatial, b_spatial, sinks, w_out, ln1_g, ln1_b,
           w_ff1, w_ff2, ln2_g, ln2_b):
    batch, seq, d_model = x.shape
    depth = w_in.shape[0]
    alpha = (2.0 * depth) ** 0.25
    inv_freq = ROPE_THETA ** (-jnp.arange(0, HEAD_DIM, 2, dtype=_F32) / HEAD_DIM)
    invf = jnp.tile(inv_freq, LANES // HALF)[None, :]
    pos = positions.reshape(batch * seq // TILE_TOKENS, LANES // HALF, BLK)
    x = x.reshape(batch * seq, d_model)
    row = lambda a: a[None, :]
    for l in range(depth):
        x = _layer(x, pos, invf, sinks[l], w_in[l], row(v_ln_g[l]), row(v_ln_b[l]),
                   w_spatial[l], b_spatial[l], w_out[l], row(ln1_g[l]), row(ln1_b[l]),
                   w_ff1[l], w_ff2[l], row(ln2_g[l]), row(ln2_b[l]), alpha=alpha, seq=seq)
    return x.reshape(batch, seq, d_model)
```

```python
import functools

import jax
import jax.numpy as jnp
from jax import lax
from jax.experimental import pallas as pl
from jax.experimental.pallas import tpu as pltpu

HEAD_DIM = 64
N_GMLP_HEADS = 8
D_GMLP = N_GMLP_HEADS * HEAD_DIM
N_Q_HEADS = 8
N_KV_HEADS = 2
GQA_GROUP = N_Q_HEADS // N_KV_HEADS
D_ATTN = N_Q_HEADS * HEAD_DIM
D_KV = N_KV_HEADS * HEAD_DIM
BLK = 128
ROPE_THETA = 10000.0
LN_EPS = 1e-5
NEG_INF = -1e30
LANES = 128
HALF = HEAD_DIM // 2
MXU_WIDTH = 256

TILE_TOKENS = 512
FFN_CHUNK = 1024
STAGE_ROWS, STAGE_COLS = 256, 1024
STAGE_SLOTS = 6
STAGE_POINT_EVERY = 3
VMEM_LIMIT_BYTES = 56 * 1024 * 1024

_BF16 = jnp.bfloat16
_F32 = jnp.float32


def _dot(a, b):
    return jnp.dot(a, b, preferred_element_type=_F32)


def _dot_nt(a, b):
    return lax.dot_general(a, b, (((1,), (1,)), ((), ())), preferred_element_type=_F32)


def _layer_norm(v, g, b):
    mu = jnp.mean(v, axis=-1, keepdims=True)
    c = v - mu
    var = jnp.mean(c * c, axis=-1, keepdims=True)
    return c * lax.rsqrt(var + LN_EPS) * g + b


def _mixer_steps(seq_start, sinks_ref, x_ref, pos_ref, invf_ref, w_in_ref, vg_ref, vb_ref, wsp_ref,
                 bsp_ref, w_out_ref, y_buf, kd_buf, vw_buf, mix_buf, *, alpha):
    tokens = x_ref.shape[0]
    n_blk = tokens // BLK
    n_slab_g = D_GMLP // LANES
    n_slab_a = D_ATTN // LANES
    q0 = 2 * D_GMLP
    k0 = q0 + D_ATTN

    x = x_ref[...]
    xb = x.astype(_BF16)
    lane = lax.broadcasted_iota(jnp.int32, (1, LANES), 1)
    lo = lane < HEAD_DIM
    ti = lax.broadcasted_iota(jnp.int32, (BLK, BLK), 0)
    si = lax.broadcasted_iota(jnp.int32, (BLK, BLK), 1)

    def column(row):
        return jnp.sum(jnp.where(si == ti, row, 0.0), axis=1, keepdims=True)

    def project(col0, width):
        parts = []
        for c in range(col0, col0 + width, MXU_WIDTH):
            parts.append(_dot(xb, w_in_ref[:, c:c + MXU_WIDTH]))
            yield
        return parts

    qkv = yield from project(q0, D_ATTN + 2 * D_KV)
    slabs = [part[:, i * LANES:(i + 1) * LANES] for part in qkv for i in range(MXU_WIDTH // LANES)]
    q_raw, k_raw, v = slabs[:n_slab_a], slabs[n_slab_a], slabs[n_slab_a + 1]

    pos = pos_ref[...].astype(_F32)
    invf = invf_ref[...]
    n_grp = LANES // HALF
    quarter = tokens // n_grp
    assert pos.shape == (n_grp, quarter) and quarter == BLK
    grp = lane // HALF
    ang = None
    for a in range(n_grp):
        term = column(pos[a:a + 1, :]) * jnp.where(grp == a, invf, 0.0)
        ang = term if ang is None else ang + term
    cos_packed, sin_packed = jnp.cos(ang), jnp.sin(ang)

    def spread(packed):
        parts = []
        for a in range(n_grp):
            one = jnp.where(grp == a, packed, 0.0)
            two = one + pltpu.roll(one, 2 * HALF, axis=1)
            parts.append(two + pltpu.roll(two, HALF, axis=1))
        return jnp.concatenate(parts, axis=0)

    cos = spread(cos_packed)
    first_half = (lane & HALF) == 0
    sin_signed = jnp.where(first_half, -1.0, 1.0) * spread(sin_packed)

    def rope(t):
        swapped = jnp.where(first_half, pltpu.roll(t, LANES - HALF, axis=1),
                            pltpu.roll(t, HALF, axis=1))
        return t * cos + swapped * sin_signed

    scale = HEAD_DIM ** -0.5
    q_even, q_odd = [], []
    for p in range(n_slab_a):
        qs = rope(q_raw[p]) * scale
        q_even.append(jnp.where(lo, qs, 0.0).astype(_BF16))
        q_odd.append(jnp.where(lo, 0.0, qs).astype(_BF16))
    kr = rope(k_raw)
    kr_sw = pltpu.roll(kr, HEAD_DIM, axis=1)
    cur = slice(BLK, BLK + tokens)
    kd_buf[0, cur, :] = jnp.where(lo, kr, kr_sw).astype(_BF16)
    kd_buf[1, cur, :] = jnp.where(lo, kr_sw, kr).astype(_BF16)
    v_sw = pltpu.roll(v, HEAD_DIM, axis=1)
    vw_buf[0, cur, :] = jnp.where(lo, v, 0.0).astype(_BF16)
    vw_buf[1, cur, :] = jnp.where(lo, 0.0, v_sw).astype(_BF16)
    vw_buf[2, cur, :] = jnp.where(lo, v_sw, 0.0).astype(_BF16)
    vw_buf[3, cur, :] = jnp.where(lo, 0.0, v).astype(_BF16)

    u = jax.nn.gelu(jnp.concatenate((yield from project(0, D_GMLP)), axis=1))
    vgel = jax.nn.gelu(jnp.concatenate((yield from project(D_GMLP, D_GMLP)), axis=1))
    vn = _layer_norm(vgel, vg_ref[...], vb_ref[...])
    lo_g = (lax.broadcasted_iota(jnp.int32, (1, D_GMLP), 1) & HEAD_DIM) == 0
    vn_top = jnp.where(lo_g, vn, 0.0).astype(_BF16)
    vn_bot = jnp.where(lo_g, 0.0, vn).astype(_BF16)

    from_prev = si > ti
    no_prev = from_prev & (si < jnp.where(seq_start, BLK, 0))
    probs = {}
    for n in range(n_blk):
        rows = slice(n * BLK, (n + 1) * BLK)
        kv_rows = slice(n * BLK, (n + 2) * BLK)
        for g in range(N_KV_HEADS):
            heads = range(GQA_GROUP * g, GQA_GROUP * (g + 1))
            q_stack = jnp.concatenate(
                [(q_odd if h % 2 else q_even)[h // 2][rows] for h in heads], axis=0)
            scores = _dot_nt(q_stack, kd_buf[g, kv_rows, :])
            for hl, h in enumerate(heads):
                sc = scores[hl * BLK:(hl + 1) * BLK]
                sh = jnp.where(from_prev, sc[:, 0:BLK], sc[:, BLK:2 * BLK])
                if n == 0:
                    sh = jnp.where(no_prev, NEG_INF, sh)
                sink = sinks_ref[h]
                m = jnp.maximum(jnp.max(sh, axis=-1, keepdims=True), sink)
                pr = jnp.exp(sh - m)
                l = jnp.sum(pr, axis=-1, keepdims=True) + jnp.exp(sink - m)
                p_band = jnp.concatenate(
                    [jnp.where(from_prev, pr, 0.0), jnp.where(from_prev, 0.0, pr)], axis=1)
                probs[n, h] = (p_band.astype(_BF16), 1.0 / l)
        yield

    causal = si <= ti

    def gmlp_piece(p):
        w_pair = jnp.concatenate(
            [jnp.where(causal, wsp_ref[2 * p], 0.0), jnp.where(causal, wsp_ref[2 * p + 1], 0.0)],
            axis=1).astype(_BF16)
        bias = jnp.where(lo, column(bsp_ref[2 * p:2 * p + 1, :]),
                         column(bsp_ref[2 * p + 1:2 * p + 2, :]))
        for c in range(n_blk):
            rows = slice(c * BLK, (c + 1) * BLK)
            cols = slice(p * LANES, (p + 1) * LANES)
            rhs = jnp.concatenate([vn_top[rows, cols], vn_bot[rows, cols]], axis=0)
            mixed = _dot(w_pair, rhs) + bias
            mix_buf[rows, cols] = (u[rows, cols] * mixed).astype(_BF16)

    def values_piece(n):
        rows = slice(n * BLK, (n + 1) * BLK)
        kv_rows = slice(n * BLK, (n + 2) * BLK)
        for slab in range(n_slab_a):
            g = (2 * slab) // GQA_GROUP
            (p_e, inv_e), (p_o, inv_o) = probs[n, 2 * slab], probs[n, 2 * slab + 1]
            acc = _dot(p_e, vw_buf[2 * g, kv_rows, :]) + _dot(p_o, vw_buf[2 * g + 1, kv_rows, :])
            out = acc * jnp.where(lo, inv_e, inv_o)
            mix_buf[rows, D_GMLP + slab * LANES:D_GMLP + (slab + 1) * LANES] = out.astype(_BF16)

    for i in range(max(n_slab_g, n_blk)):
        if i < n_slab_g:
            gmlp_piece(i)
            yield
        if i < n_blk:
            values_piece(i)
            yield
    last = slice(tokens, tokens + BLK)
    kd_buf[:, 0:BLK, :] = kd_buf[:, last, :]
    vw_buf[:, 0:BLK, :] = vw_buf[:, last, :]

    mix = mix_buf[...]
    for c in range(0, x.shape[1], MXU_WIDTH):
        y_buf[:, c:c + MXU_WIDTH] = alpha * x[:, c:c + MXU_WIDTH] + _dot(mix, w_out_ref[:, c:c + MXU_WIDTH])
        yield


def _ffn_steps(rows, y_buf, g1_ref, b1_ref, w1_ref, w2_ref, g2_ref, b2_ref, o_ref, *, alpha):
    x = _layer_norm(y_buf[rows, :], g1_ref[...], b1_ref[...])
    xb = x.astype(_BF16)
    yield
    squares = []
    for c in range(0, w1_ref.shape[1], FFN_CHUNK):
        h = jnp.maximum(_dot(xb, w1_ref[:, c:c + FFN_CHUNK]).astype(_BF16), 0.0)
        squares.append(h * h)
        yield
    h2 = jnp.concatenate(squares, axis=1)
    outs = []
    for c in range(0, x.shape[1], MXU_WIDTH):
        outs.append(alpha * x[:, c:c + MXU_WIDTH] + _dot(h2, w2_ref[:, c:c + MXU_WIDTH]))
        yield
    o_ref[rows, :] = _layer_norm(jnp.concatenate(outs, axis=1), g2_ref[...], b2_ref[...])


def _run(steps, count=None):
    if count is None:
        for _ in steps:
            pass
    else:
        for _ in range(count):
            next(steps)


def _stage_jobs(pairs):
    jobs = []
    for src, dst in pairs:
        assert src.shape[0] % STAGE_ROWS == 0 and src.shape[1] % LANES == 0
        for r0 in range(0, src.shape[0], STAGE_ROWS):
            for c0 in range(0, src.shape[1], STAGE_COLS):
                jobs.append((src, dst, r0, c0, min(STAGE_COLS, src.shape[1] - c0)))
    return jobs


def _stage_steps(jobs, stage, sem):
    copies = [pltpu.make_async_copy(src.at[pl.ds(r0, STAGE_ROWS), pl.ds(c0, width)],
                                    stage.at[i % STAGE_SLOTS, :, pl.ds(0, width)],
                                    sem.at[i % STAGE_SLOTS])
              for i, (src, _, r0, c0, width) in enumerate(jobs)]
    ahead = STAGE_SLOTS - 1
    for copy in copies[:ahead]:
        copy.start()
    for i, (_, dst, r0, c0, width) in enumerate(jobs):
        if i + ahead < len(jobs):
            copies[i + ahead].start()
        copies[i].wait()
        dst[r0:r0 + STAGE_ROWS, c0:c0 + width] = stage[i % STAGE_SLOTS, :, 0:width].astype(_BF16)
        yield


def _layer_kernel(sinks_ref, x_ref, pos_ref, invf_ref, vg_ref, vb_ref, wsp_ref, bsp_ref, g1_ref,
                  b1_ref, g2_ref, b2_ref, w_in_hbm, w_out_hbm, w1_hbm, w2_hbm, o_ref,
                  y_buf, kd_buf, vw_buf, mix_buf, w_in_ref, w_out_ref, w1_ref, w2_ref, stage_sem,
                  *, alpha, tiles_per_seq):
    j = pl.program_id(0)
    seq_start = (j % tiles_per_seq) == 0

    @pl.when(seq_start)
    def _():
        kd_buf[:, 0:BLK, :] = jnp.zeros((N_KV_HEADS, BLK, LANES), _BF16)
        vw_buf[:, 0:BLK, :] = jnp.zeros((2 * N_KV_HEADS, BLK, LANES), _BF16)

    mixer = functools.partial(
        _mixer_steps, seq_start, sinks_ref, x_ref, pos_ref, invf_ref, w_in_ref, vg_ref, vb_ref,
        wsp_ref, bsp_ref, w_out_ref, y_buf, kd_buf, vw_buf, mix_buf, alpha=alpha)
    half = x_ref.shape[0] // 2
    ffn = [functools.partial(_ffn_steps, slice(i * half, (i + 1) * half), y_buf, g1_ref, b1_ref,
                             w1_ref, w2_ref, g2_ref, b2_ref, o_ref, alpha=alpha) for i in range(2)]

    @pl.when(j == 0)
    def _():
        def first_step(stage):
            _run(_stage_steps(_stage_jobs([(w_in_hbm, w_in_ref)]), stage, stage_sem))
            later = _stage_steps(
                _stage_jobs([(w_out_hbm, w_out_ref), (w1_hbm, w1_ref), (w2_hbm, w2_ref)]),
                stage, stage_sem)
            assert len(_stage_jobs([(w_out_hbm, w_out_ref)])) <= STAGE_SLOTS - 1
            for i, _ in enumerate(mixer()):
                if i % STAGE_POINT_EVERY == STAGE_POINT_EVERY - 1:
                    for _ in range(STAGE_SLOTS - 1):
                        next(later, None)
            _run(later)

        pl.run_scoped(first_step, pltpu.VMEM((STAGE_SLOTS, STAGE_ROWS, STAGE_COLS), _F32))

    @pl.when(j > 0)
    def _():
        fa, fb, m = ffn[0](), ffn[1](), mixer()
        n_blk = x_ref.shape[0] // BLK
        n_proj = w_in_ref.shape[1] // MXU_WIDTH
        ffn_pieces = w1_ref.shape[1] // FFN_CHUNK + w2_ref.shape[1] // MXU_WIDTH
        mixer_pieces = [1] * (n_proj + n_blk) + [2] * n_blk + [1]
        assert len(mixer_pieces) == 2 * ffn_pieces
        _run(fa, 1)
        for count in mixer_pieces[:ffn_pieces]:
            _run(m, count)
            _run(fa, 1)
        _run(fa)
        _run(fb, 1)
        for count in mixer_pieces[ffn_pieces:]:
            _run(m, count)
            _run(fb, 1)
        _run(m)
        _run(fb)


def _resident(shape):
    return pl.BlockSpec(shape, lambda *_: (0,) * len(shape), pipeline_mode=pl.Buffered(1))


def _layer(x, pos, invf, sinks, w_in, vg, vb, wsp, bsp, w_out, g1, b1, w1, w2, g2, b2, *, alpha,
           seq):
    n_tok, d_model = x.shape
    tokens = TILE_TOKENS
    assert seq % tokens == 0 and tokens % BLK == 0 and w1.shape[1] % FFN_CHUNK == 0
    n_tiles = n_tok // tokens
    kernel = functools.partial(_layer_kernel, alpha=alpha, tiles_per_seq=seq // tokens)
    in_tile = pl.BlockSpec((tokens, d_model), lambda j: (jnp.minimum(j, n_tiles - 1), 0))
    pos_tile = pl.BlockSpec((None,) + pos.shape[1:], lambda j: (jnp.minimum(j, n_tiles - 1), 0, 0))
    out_tile = pl.BlockSpec((tokens, d_model), lambda j: (jnp.maximum(j - 1, 0), 0))
    resident = [invf, vg, vb, wsp, bsp, g1, b1, g2, b2]
    weights = [w_in, w_out, w1, w2]
    return pl.pallas_call(
        kernel,
        grid=(n_tiles + 1,),
        in_specs=[pl.BlockSpec(memory_space=pltpu.SMEM), in_tile, pos_tile]
        + [_resident(a.shape) for a in resident]
        + [pl.BlockSpec(memory_space=pl.ANY) for _ in weights],
        out_specs=out_tile,
        out_shape=jax.ShapeDtypeStruct(x.shape, x.dtype),
        scratch_shapes=[
            pltpu.VMEM((tokens, d_model), _F32),
            pltpu.VMEM((N_KV_HEADS, tokens + BLK, LANES), _BF16),
            pltpu.VMEM((2 * N_KV_HEADS, tokens + BLK, LANES), _BF16),
            pltpu.VMEM((tokens, D_GMLP + D_ATTN), _BF16),
        ] + [pltpu.VMEM(w.shape, _BF16) for w in weights] + [pltpu.SemaphoreType.DMA((STAGE_SLOTS,))],
        compiler_params=pltpu.CompilerParams(
            dimension_semantics=("arbitrary",), vmem_limit_bytes=VMEM_LIMIT_BYTES),
        name="layer",
    )(sinks, x, pos, *resident, *weights)


def kernel(x, positions, w_in, v_ln_g, v_ln_b, w_spatial, b_spatial, sinks, w_out, ln1_g, ln1_b,
           w_ff1, w_ff2, ln2_g, ln2_b):
    batch, seq, d_model = x.shape
    depth = w_in.shape[0]
    alpha = (2.0 * depth) ** 0.25
    inv_freq = ROPE_THETA ** (-jnp.arange(0, HEAD_DIM, 2, dtype=_F32) / HEAD_DIM)
    invf = jnp.tile(inv_freq, LANES // HALF)[None, :]
    pos = positions.reshape(batch * seq // TILE_TOKENS, LANES // HALF, BLK)
    x = x.reshape(batch * seq, d_model)
    row = lambda a: a[None, :]
    for l in range(depth):
        x = _layer(x, pos, invf, sinks[l], w_in[l], row(v_ln_g[l]), row(v_ln_b[l]),
                   w_spatial[l], b_spatial[l], w_out[l], row(ln1_g[l]), row(ln1_b[l]),
                   w_ff1[l], w_ff2[l], row(ln2_g[l]), row(ln2_b[l]), alpha=alpha, seq=seq)
    return x.reshape(batch, seq, d_model)
```

```python
import functools

import jax
import jax.numpy as jnp
from jax import lax
from jax.experimental import pallas as pl
from jax.experimental.pallas import tpu as pltpu

HEAD_DIM = 64
N_GMLP_HEADS = 8
D_GMLP = N_GMLP_HEADS * HEAD_DIM
N_Q_HEADS = 8
N_KV_HEADS = 2
GQA_GROUP = N_Q_HEADS // N_KV_HEADS
D_ATTN = N_Q_HEADS * HEAD_DIM
D_KV = N_KV_HEADS * HEAD_DIM
BLK = 128
ROPE_THETA = 10000.0
LN_EPS = 1e-5
NEG_INF = -1e30
LANES = 128
HALF = HEAD_DIM // 2
MXU_WIDTH = 256

TILE_TOKENS = 512
FFN_CHUNK = 1024
STAGE_ROWS, STAGE_COLS = 256, 1024
STAGE_SLOTS = 6
STAGE_POINT_EVERY = 3
VMEM_LIMIT_BYTES = 56 * 1024 * 1024

_BF16 = jnp.bfloat16
_F32 = jnp.float32


def _dot(a, b):
    return jnp.dot(a, b, preferred_element_type=_F32)


def _dot_nt(a, b):
    return lax.dot_general(a, b, (((1,), (1,)), ((), ())), preferred_element_type=_F32)


def _layer_norm(v, g, b):
    mu = jnp.mean(v, axis=-1, keepdims=True)
    c = v - mu
    var = jnp.mean(c * c, axis=-1, keepdims=True)
    return c * lax.rsqrt(var + LN_EPS) * g + b


def _mixer_steps(seq_start, sinks_ref, x_ref, pos_ref, invf_ref, w_in_ref, vg_ref, vb_ref, wsp_ref,
                 bsp_ref, w_out_ref, y_buf, kd_buf, vw_buf, mix_buf, *, alpha):
    tokens = x_ref.shape[0]
    n_blk = tokens // BLK
    n_slab_g = D_GMLP // LANES
    n_slab_a = D_ATTN // LANES
    q0 = 2 * D_GMLP
    k0 = q0 + D_ATTN

    x = x_ref[...]
    xb = x.astype(_BF16)
    lane = lax.broadcasted_iota(jnp.int32, (1, LANES), 1)
    lo = lane < HEAD_DIM
    ti = lax.broadcasted_iota(jnp.int32, (BLK, BLK), 0)
    si = lax.broadcasted_iota(jnp.int32, (BLK, BLK), 1)

    def column(row):
        return jnp.sum(jnp.where(si == ti, row, 0.0), axis=1, keepdims=True)

    def project(col0, width):
        parts = []
        for c in range(col0, col0 + width, MXU_WIDTH):
            parts.append(_dot(xb, w_in_ref[:, c:c + MXU_WIDTH]))
            yield
        return parts

    qkv = yield from project(q0, D_ATTN + 2 * D_KV)
    slabs = [part[:, i * LANES:(i + 1) * LANES] for part in qkv for i in range(MXU_WIDTH // LANES)]
    q_raw, k_raw, v = slabs[:n_slab_a], slabs[n_slab_a], slabs[n_slab_a + 1]

    pos = pos_ref[...].astype(_F32)
    invf = invf_ref[...]
    n_grp = LANES // HALF
    quarter = tokens // n_grp
    assert pos.shape == (n_grp, quarter) and quarter == BLK
    grp = lane // HALF
    ang = None
    for a in range(n_grp):
        term = column(pos[a:a + 1, :]) * jnp.where(grp == a, invf, 0.0)
        ang = term if ang is None else ang + term
    cos_packed, sin_packed = jnp.cos(ang), jnp.sin(ang)

    def spread(packed):
        parts = []
        for a in range(n_grp):
            one = jnp.where(grp == a, packed, 0.0)
            two = one + pltpu.roll(one, 2 * HALF, axis=1)
            parts.append(two + pltpu.roll(two, HALF, axis=1))
        return jnp.concatenate(parts, axis=0)

    cos = spread(cos_packed)
    first_half = (lane & HALF) == 0
    sin_signed = jnp.where(first_half, -1.0, 1.0) * spread(sin_packed)

    def rope(t):
        swapped = jnp.where(first_half, pltpu.roll(t, LANES - HALF, axis=1),
                            pltpu.roll(t, HALF, axis=1))
        return t * cos + swapped * sin_signed

    scale = HEAD_DIM ** -0.5
    q_even, q_odd = [], []
    for p in range(n_slab_a):
        qs = rope(q_raw[p]) * scale
        q_even.append(jnp.where(lo, qs, 0.0).astype(_BF16))
        q_odd.append(jnp.where(lo, 0.0, qs).astype(_BF16))
    kr = rope(k_raw)
    kr_sw = pltpu.roll(kr, HEAD_DIM, axis=1)
    cur = slice(BLK, BLK + tokens)
    kd_buf[0, cur, :] = jnp.where(lo, kr, kr_sw).astype(_BF16)
    kd_buf[1, cur, :] = jnp.where(lo, kr_sw, kr).astype(_BF16)
    v_sw = pltpu.roll(v, HEAD_DIM, axis=1)
    vw_buf[0, cur, :] = jnp.where(lo, v, 0.0).astype(_BF16)
    vw_buf[1, cur, :] = jnp.where(lo, 0.0, v_sw).astype(_BF16)
    vw_buf[2, cur, :] = jnp.where(lo, v_sw, 0.0).astype(_BF16)
    vw_buf[3, cur, :] = jnp.where(lo, 0.0, v).astype(_BF16)

    u = jax.nn.gelu(jnp.concatenate((yield from project(0, D_GMLP)), axis=1))
    vgel = jax.nn.gelu(jnp.concatenate((yield from project(D_GMLP, D_GMLP)), axis=1))
    vn = _layer_norm(vgel, vg_ref[...], vb_ref[...])
    lo_g = (lax.broadcasted_iota(jnp.int32, (1, D_GMLP), 1) & HEAD_DIM) == 0
    vn_top = jnp.where(lo_g, vn, 0.0).astype(_BF16)
    vn_bot = jnp.where(lo_g, 0.0, vn).astype(_BF16)

    from_prev = si > ti
    no_prev = from_prev & (si < jnp.where(seq_start, BLK, 0))
    probs = {}
    for n in range(n_blk):
        rows = slice(n * BLK, (n + 1) * BLK)
        kv_rows = slice(n * BLK, (n + 2) * BLK)
        for g in range(N_KV_HEADS):
            heads = range(GQA_GROUP * g, GQA_GROUP * (g + 1))
            q_stack = jnp.concatenate(
                [(q_odd if h % 2 else q_even)[h // 2][rows] for h in heads], axis=0)
            scores = _dot_nt(q_stack, kd_buf[g, kv_rows, :])
            for hl, h in enumerate(heads):
                sc = scores[hl * BLK:(hl + 1) * BLK]
                sh = jnp.where(from_prev, sc[:, 0:BLK], sc[:, BLK:2 * BLK])
                if n == 0:
                    sh = jnp.where(no_prev, NEG_INF, sh)
                sink = sinks_ref[h]
                m = jnp.maximum(jnp.max(sh, axis=-1, keepdims=True), sink)
                pr = jnp.exp(sh - m)
                l = jnp.sum(pr, axis=-1, keepdims=True) + jnp.exp(sink - m)
                p_band = jnp.concatenate(
                    [jnp.where(from_prev, pr, 0.0), jnp.where(from_prev, 0.0, pr)], axis=1)
                probs[n, h] = (p_band.astype(_BF16), 1.0 / l)
        yield

    causal = si <= ti

    def gmlp_piece(p):
        w_pair = jnp.concatenate(
            [jnp.where(causal, wsp_ref[2 * p], 0.0), jnp.where(causal, wsp_ref[2 * p + 1], 0.0)],
            axis=1).astype(_BF16)
        bias = jnp.where(lo, column(bsp_ref[2 * p:2 * p + 1, :]),
                         column(bsp_ref[2 * p + 1:2 * p + 2, :]))
        for c in range(n_blk):
            rows = slice(c * BLK, (c + 1) * BLK)
            cols = slice(p * LANES, (p + 1) * LANES)
            rhs = jnp.concatenate([vn_top[rows, cols], vn_bot[rows, cols]], axis=0)
            mixed = _dot(w_pair, rhs) + bias
            mix_buf[rows, cols] = (u[rows, cols] * mixed).astype(_BF16)

    def values_piece(n):
        rows = slice(n * BLK, (n + 1) * BLK)
        kv_rows = slice(n * BLK, (n + 2) * BLK)
        for slab in range(n_slab_a):
            g = (2 * slab) // GQA_GROUP
            (p_e, inv_e), (p_o, inv_o) = probs[n, 2 * slab], probs[n, 2 * slab + 1]
            acc = _dot(p_e, vw_buf[2 * g, kv_rows, :]) + _dot(p_o, vw_buf[2 * g + 1, kv_rows, :])
            out = acc * jnp.where(lo, inv_e, inv_o)
            mix_buf[rows, D_GMLP + slab * LANES:D_GMLP + (slab + 1) * LANES] = out.astype(_BF16)

    for i in range(max(n_slab_g, n_blk)):
        if i < n_slab_g:
            gmlp_piece(i)
            yield
        if i < n_blk:
            values_piece(i)
            yield
    last = slice(tokens, tokens + BLK)
    kd_buf[:, 0:BLK, :] = kd_buf[:, last, :]
    vw_buf[:, 0:BLK, :] = vw_buf[:, last, :]

    mix = mix_buf[...]
    for c in range(0, x.shape[1], MXU_WIDTH):
        y_buf[:, c:c + MXU_WIDTH] = alpha * x[:, c:c + MXU_WIDTH] + _dot(mix, w_out_ref[:, c:c + MXU_WIDTH])
        yield


def _ffn_steps(rows, y_buf, g1_ref, b1_ref, w1_ref, w2_ref, g2_ref, b2_ref, o_ref, *, alpha):
    x = _layer_norm(y_buf[rows, :], g1_ref[...], b1_ref[...])
    xb = x.astype(_BF16)
    acc = alpha * x
    yield
    for j in range(w1_ref.shape[1] // FFN_CHUNK):
        cols = slice(j * FFN_CHUNK, (j + 1) * FFN_CHUNK)
        h = jnp.maximum(_dot(xb, w1_ref[:, cols]), 0.0)
        yield
        acc = acc + _dot((h * h).astype(_BF16), w2_ref[cols, :])
        yield
    o_ref[rows, :] = _layer_norm(acc, g2_ref[...], b2_ref[...])


def _run(steps, count=None):
    if count is None:
        for _ in steps:
            pass
    else:
        for _ in range(count):
            next(steps)


def _stage_jobs(pairs):
    jobs = []
    for src, dst in pairs:
        assert src.shape[0] % STAGE_ROWS == 0 and src.shape[1] % LANES == 0
        for r0 in range(0, src.shape[0], STAGE_ROWS):
            for c0 in range(0, src.shape[1], STAGE_COLS):
                jobs.append((src, dst, r0, c0, min(STAGE_COLS, src.shape[1] - c0)))
    return jobs


def _stage_steps(jobs, stage, sem):
    copies = [pltpu.make_async_copy(src.at[pl.ds(r0, STAGE_ROWS), pl.ds(c0, width)],
                                    stage.at[i % STAGE_SLOTS, :, pl.ds(0, width)],
                                    sem.at[i % STAGE_SLOTS])
              for i, (src, _, r0, c0, width) in enumerate(jobs)]
    ahead = STAGE_SLOTS - 1
    for copy in copies[:ahead]:
        copy.start()
    for i, (_, dst, r0, c0, width) in enumerate(jobs):
        if i + ahead < len(jobs):
            copies[i + ahead].start()
        copies[i].wait()
        dst[r0:r0 + STAGE_ROWS, c0:c0 + width] = stage[i % STAGE_SLOTS, :, 0:width].astype(_BF16)
        yield


def _layer_kernel(sinks_ref, x_ref, pos_ref, invf_ref, vg_ref, vb_ref, wsp_ref, bsp_ref, g1_ref,
                  b1_ref, g2_ref, b2_ref, w_in_hbm, w_out_hbm, w1_hbm, w2_hbm, o_ref,
                  y_buf, kd_buf, vw_buf, mix_buf, w_in_ref, w_out_ref, w1_ref, w2_ref, stage_sem,
                  *, alpha, tiles_per_seq):
    j = pl.program_id(0)
    seq_start = (j % tiles_per_seq) == 0

    @pl.when(seq_start)
    def _():
        kd_buf[:, 0:BLK, :] = jnp.zeros((N_KV_HEADS, BLK, LANES), _BF16)
        vw_buf[:, 0:BLK, :] = jnp.zeros((2 * N_KV_HEADS, BLK, LANES), _BF16)

    mixer = functools.partial(
        _mixer_steps, seq_start, sinks_ref, x_ref, pos_ref, invf_ref, w_in_ref, vg_ref, vb_ref,
        wsp_ref, bsp_ref, w_out_ref, y_buf, kd_buf, vw_buf, mix_buf, alpha=alpha)
    half = x_ref.shape[0] // 2
    ffn = [functools.partial(_ffn_steps, slice(i * half, (i + 1) * half), y_buf, g1_ref, b1_ref,
                             w1_ref, w2_ref, g2_ref, b2_ref, o_ref, alpha=alpha) for i in range(2)]

    @pl.when(j == 0)
    def _():
        def first_step(stage):
            _run(_stage_steps(_stage_jobs([(w_in_hbm, w_in_ref)]), stage, stage_sem))
            later = _stage_steps(
                _stage_jobs([(w_out_hbm, w_out_ref), (w1_hbm, w1_ref), (w2_hbm, w2_ref)]),
                stage, stage_sem)
            assert len(_stage_jobs([(w_out_hbm, w_out_ref)])) <= STAGE_SLOTS - 1
            for i, _ in enumerate(mixer()):
                if i % STAGE_POINT_EVERY == STAGE_POINT_EVERY - 1:
                    for _ in range(STAGE_SLOTS - 1):
                        next(later, None)
            _run(later)

        pl.run_scoped(first_step, pltpu.VMEM((STAGE_SLOTS, STAGE_ROWS, STAGE_COLS), _F32))

    @pl.when(j > 0)
    def _():
        fa, fb, m = ffn[0](), ffn[1](), mixer()
        n_blk = x_ref.shape[0] // BLK
        n_proj = w_in_ref.shape[1] // MXU_WIDTH
        ffn_pieces = 2 * (w1_ref.shape[1] // FFN_CHUNK)
        mixer_pieces = [1] * (n_proj + n_blk) + [2] * n_blk + [1]
        assert len(mixer_pieces) == 2 * ffn_pieces
        _run(fa, 1)
        for count in mixer_pieces[:ffn_pieces]:
            _run(m, count)
            _run(fa, 1)
        _run(fa)
        _run(fb, 1)
        for count in mixer_pieces[ffn_pieces:]:
            _run(m, count)
            _run(fb, 1)
        _run(m)
        _run(fb)


def _resident(shape):
    return pl.BlockSpec(shape, lambda *_: (0,) * len(shape), pipeline_mode=pl.Buffered(1))


def _layer(x, pos, invf, sinks, w_in, vg, vb, wsp, bsp, w_out, g1, b1, w1, w2, g2, b2, *, alpha,
           seq):
    n_tok, d_model = x.shape
    tokens = TILE_TOKENS
    assert seq % tokens == 0 and tokens % BLK == 0 and w1.shape[1] % FFN_CHUNK == 0
    n_tiles = n_tok // tokens
    kernel = functools.partial(_layer_kernel, alpha=alpha, tiles_per_seq=seq // tokens)
    in_tile = pl.BlockSpec((tokens, d_model), lambda j: (jnp.minimum(j, n_tiles - 1), 0))
    pos_tile = pl.BlockSpec((None,) + pos.shape[1:], lambda j: (jnp.minimum(j, n_tiles - 1), 0, 0))
    out_tile = pl.BlockSpec((tokens, d_model), lambda j: (jnp.maximum(j - 1, 0), 0))
    resident = [invf, vg, vb, wsp, bsp, g1, b1, g2, b2]
    weights = [w_in, w_out, w1, w2]
    return pl.pallas_call(
        kernel,
        grid=(n_tiles + 1,),
        in_specs=[pl.BlockSpec(memory_space=pltpu.SMEM), in_tile, pos_tile]
        + [_resident(a.shape) for a in resident]
        + [pl.BlockSpec(memory_space=pl.ANY) for _ in weights],
        out_specs=out_tile,
        out_shape=jax.ShapeDtypeStruct(x.shape, x.dtype),
        scratch_shapes=[
            pltpu.VMEM((tokens, d_model), _F32),
            pltpu.VMEM((N_KV_HEADS, tokens + BLK, LANES), _BF16),
            pltpu.VMEM((2 * N_KV_HEADS, tokens + BLK, LANES), _BF16),
            pltpu.VMEM((tokens, D_GMLP + D_ATTN), _BF16),
        ] + [pltpu.VMEM(w.shape, _BF16) for w in weights] + [pltpu.SemaphoreType.DMA((STAGE_SLOTS,))],
        compiler_params=pltpu.CompilerParams(
            dimension_semantics=("arbitrary",), vmem_limit_bytes=VMEM_LIMIT_BYTES),
        name="layer",
    )(sinks, x, pos, *resident, *weights)


def kernel(x, positions, w_in, v_ln_g, v_ln_b, w_spatial, b_spatial, sinks, w_out, ln1_g, ln1_b,
           w_ff1, w_ff2, ln2_g, ln2_b):
    batch, seq, d_model = x.shape
    depth = w_in.shape[0]
    alpha = (2.0 * depth) ** 0.25
    inv_freq = ROPE_THETA ** (-jnp.arange(0, HEAD_DIM, 2, dtype=_F32) / HEAD_DIM)
    invf = jnp.tile(inv_freq, LANES // HALF)[None, :]
    pos = positions.reshape(batch * seq // TILE_TOKENS, LANES // HALF, BLK)
    x = x.reshape(batch * seq, d_model)
    row = lambda a: a[None, :]
    for l in range(depth):
        x = _layer(x, pos, invf, sinks[l], w_in[l], row(v_ln_g[l]), row(v_ln_b[l]),
                   w_spatial[l], b_spatial[l], w_out[l], row(ln1_g[l]), row(ln1_b[l]),
                   w_ff1[l], w_ff2[l], row(ln2_g[l]), row(ln2_b[l]), alpha=alpha, seq=seq)
    return x.reshape(batch, seq, d_model)
```

```python
import functools

import jax
import jax.numpy as jnp
from jax import lax
from jax.experimental import pallas as pl
from jax.experimental.pallas import tpu as pltpu

HEAD_DIM = 64
N_GMLP_HEADS = 8
D_GMLP = N_GMLP_HEADS * HEAD_DIM
N_Q_HEADS = 8
N_KV_HEADS = 2
GQA_GROUP = N_Q_HEADS // N_KV_HEADS
D_ATTN = N_Q_HEADS * HEAD_DIM
D_KV = N_KV_HEADS * HEAD_DIM
BLK = 128
ROPE_THETA = 10000.0
LN_EPS = 1e-5
NEG_INF = -1e30
LANES = 128
HALF = HEAD_DIM // 2
MXU_WIDTH = 256

TILE_TOKENS = 512
FFN_CHUNK = 1024
STAGE_ROWS, STAGE_COLS = 256, 1024
STAGE_SLOTS = 6
STAGE_POINT_EVERY = 3
VMEM_LIMIT_BYTES = 56 * 1024 * 1024

_BF16 = jnp.bfloat16
_F32 = jnp.float32


def _dot(a, b):
    return jnp.dot(a, b, preferred_element_type=_F32)


def _dot_nt(a, b):
    return lax.dot_general(a, b, (((1,), (1,)), ((), ())), preferred_element_type=_F32)


def _layer_norm(v, g, b):
    mu = jnp.mean(v, axis=-1, keepdims=True)
    c = v - mu
    var = jnp.mean(c * c, axis=-1, keepdims=True)
    return c * lax.rsqrt(var + LN_EPS) * g + b


def _mixer_steps(seq_start, sinks_ref, x_ref, pos_ref, invf_ref, w_in_ref, vg_ref, vb_ref, wsp_ref,
                 bsp_ref, w_out_ref, y_buf, kd_buf, vt_buf, mix_buf, *, alpha):
    tokens = x_ref.shape[0]
    n_blk = tokens // BLK
    n_slab_g = D_GMLP // LANES
    n_slab_a = D_ATTN // LANES
    q0 = 2 * D_GMLP
    k0 = q0 + D_ATTN

    x = x_ref[...]
    xb = x.astype(_BF16)
    lane = lax.broadcasted_iota(jnp.int32, (1, LANES), 1)
    lo = lane < HEAD_DIM
    ti = lax.broadcasted_iota(jnp.int32, (BLK, BLK), 0)
    si = lax.broadcasted_iota(jnp.int32, (BLK, BLK), 1)

    def column(row):
        return jnp.sum(jnp.where(si == ti, row, 0.0), axis=1, keepdims=True)

    def project(col0, width):
        parts = []
        for c in range(col0, col0 + width, MXU_WIDTH):
            parts.append(_dot(xb, w_in_ref[:, c:c + MXU_WIDTH]))
            yield
        return parts

    qkv = yield from project(q0, D_ATTN + 2 * D_KV)
    slabs = [part[:, i * LANES:(i + 1) * LANES] for part in qkv for i in range(MXU_WIDTH // LANES)]
    q_raw, k_raw, v = slabs[:n_slab_a], slabs[n_slab_a], slabs[n_slab_a + 1]

    pos = pos_ref[...].astype(_F32)
    invf = invf_ref[...]
    n_grp = LANES // HALF
    quarter = tokens // n_grp
    assert pos.shape == (n_grp, quarter) and quarter == BLK
    grp = lane // HALF
    ang = None
    for a in range(n_grp):
        term = column(pos[a:a + 1, :]) * jnp.where(grp == a, invf, 0.0)
        ang = term if ang is None else ang + term
    cos_packed, sin_packed = jnp.cos(ang), jnp.sin(ang)

    def spread(packed):
        parts = []
        for a in range(n_grp):
            one = jnp.where(grp == a, packed, 0.0)
            two = one + pltpu.roll(one, 2 * HALF, axis=1)
            parts.append(two + pltpu.roll(two, HALF, axis=1))
        return jnp.concatenate(parts, axis=0)

    cos = spread(cos_packed)
    first_half = (lane & HALF) == 0
    sin_signed = jnp.where(first_half, -1.0, 1.0) * spread(sin_packed)

    def rope(t):
        swapped = jnp.where(first_half, pltpu.roll(t, LANES - HALF, axis=1),
                            pltpu.roll(t, HALF, axis=1))
        return t * cos + swapped * sin_signed

    scale = HEAD_DIM ** -0.5
    q_even, q_odd = [], []
    for p in range(n_slab_a):
        qs = rope(q_raw[p]) * scale
        q_even.append(jnp.where(lo, qs, 0.0).astype(_BF16))
        q_odd.append(jnp.where(lo, 0.0, qs).astype(_BF16))
    kr = rope(k_raw)
    kr_sw = pltpu.roll(kr, HEAD_DIM, axis=1)
    cur = slice(BLK, BLK + tokens)
    kd_buf[0, cur, :] = jnp.where(lo, kr, kr_sw).astype(_BF16)
    kd_buf[1, cur, :] = jnp.where(lo, kr_sw, kr).astype(_BF16)
    vt_buf[:, cur] = v.T.astype(_BF16)

    u = jax.nn.gelu(jnp.concatenate((yield from project(0, D_GMLP)), axis=1))
    vgel = jax.nn.gelu(jnp.concatenate((yield from project(D_GMLP, D_GMLP)), axis=1))
    vn = _layer_norm(vgel, vg_ref[...], vb_ref[...])
    lo_g = (lax.broadcasted_iota(jnp.int32, (1, D_GMLP), 1) & HEAD_DIM) == 0
    vn_top = jnp.where(lo_g, vn, 0.0).astype(_BF16)
    vn_bot = jnp.where(lo_g, 0.0, vn).astype(_BF16)

    from_prev = ti > si
    no_prev = from_prev & (ti < jnp.where(seq_start, BLK, 0))
    probs = {}
    for n in range(n_blk):
        rows = slice(n * BLK, (n + 1) * BLK)
        kv_rows = slice(n * BLK, (n + 2) * BLK)
        for g in range(N_KV_HEADS):
            heads = range(GQA_GROUP * g, GQA_GROUP * (g + 1))
            q_stack = jnp.concatenate(
                [(q_odd if h % 2 else q_even)[h // 2][rows] for h in heads], axis=0)
            scores = _dot_nt(kd_buf[g, kv_rows, :], q_stack)
            p_parts, inv_parts = [], []
            for hl, h in enumerate(heads):
                cols = slice(hl * BLK, (hl + 1) * BLK)
                sh = jnp.where(from_prev, scores[0:BLK, cols], scores[BLK:2 * BLK, cols])
                if n == 0:
                    sh = jnp.where(no_prev, NEG_INF, sh)
                sink = sinks_ref[h]
                m = jnp.maximum(jnp.max(sh, axis=0, keepdims=True), sink)
                pr = jnp.exp(sh - m)
                l = jnp.sum(pr, axis=0, keepdims=True) + jnp.exp(sink - m)
                p_parts.append(jnp.concatenate(
                    [jnp.where(from_prev, pr, 0.0), jnp.where(from_prev, 0.0, pr)], axis=0))
                inv_parts.append(1.0 / l)
            probs[n, g] = (jnp.concatenate(p_parts, axis=1).astype(_BF16),
                           jnp.concatenate(inv_parts, axis=1))
        yield

    causal = si <= ti

    def gmlp_piece(p):
        w_pair = jnp.concatenate(
            [jnp.where(causal, wsp_ref[2 * p], 0.0), jnp.where(causal, wsp_ref[2 * p + 1], 0.0)],
            axis=1).astype(_BF16)
        bias = jnp.where(lo, column(bsp_ref[2 * p:2 * p + 1, :]),
                         column(bsp_ref[2 * p + 1:2 * p + 2, :]))
        for c in range(n_blk):
            rows = slice(c * BLK, (c + 1) * BLK)
            cols = slice(p * LANES, (p + 1) * LANES)
            rhs = jnp.concatenate([vn_top[rows, cols], vn_bot[rows, cols]], axis=0)
            mixed = _dot(w_pair, rhs) + bias
            mix_buf[rows, cols] = (u[rows, cols] * mixed).astype(_BF16)

    def values_piece(n):
        rows = slice(n * BLK, (n + 1) * BLK)
        kv_cols = slice(n * BLK, (n + 2) * BLK)
        for g in range(N_KV_HEADS):
            p_all, inv_all = probs[n, g]
            v_t = vt_buf[g * HEAD_DIM:(g + 1) * HEAD_DIM, kv_cols]
            out_t = _dot(v_t, p_all) * inv_all
            for i in range(GQA_GROUP // 2):
                slab = g * (GQA_GROUP // 2) + i
                pair = jnp.concatenate([out_t[:, (2 * i) * BLK:(2 * i + 1) * BLK],
                                        out_t[:, (2 * i + 1) * BLK:(2 * i + 2) * BLK]], axis=0)
                mix_buf[rows, D_GMLP + slab * LANES:D_GMLP + (slab + 1) * LANES] = (
                    pair.T.astype(_BF16))

    for i in range(max(n_slab_g, n_blk)):
        if i < n_slab_g:
            gmlp_piece(i)
            yield
        if i < n_blk:
            values_piece(i)
            yield
    last = slice(tokens, tokens + BLK)
    kd_buf[:, 0:BLK, :] = kd_buf[:, last, :]
    vt_buf[:, 0:BLK] = vt_buf[:, last]

    mix = mix_buf[...]
    for c in range(0, x.shape[1], MXU_WIDTH):
        y_buf[:, c:c + MXU_WIDTH] = alpha * x[:, c:c + MXU_WIDTH] + _dot(mix, w_out_ref[:, c:c + MXU_WIDTH])
        yield


def _ffn_steps(rows, y_buf, g1_ref, b1_ref, w1_ref, w2_ref, g2_ref, b2_ref, o_ref, *, alpha):
    x = _layer_norm(y_buf[rows, :], g1_ref[...], b1_ref[...])
    xb = x.astype(_BF16)
    acc = alpha * x
    yield
    for j in range(w1_ref.shape[1] // FFN_CHUNK):
        cols = slice(j * FFN_CHUNK, (j + 1) * FFN_CHUNK)
        h = jnp.maximum(_dot(xb, w1_ref[:, cols]), 0.0)
        yield
        acc = acc + _dot((h * h).astype(_BF16), w2_ref[cols, :])
        yield
    o_ref[rows, :] = _layer_norm(acc, g2_ref[...], b2_ref[...])


def _run(steps, count=None):
    if count is None:
        for _ in steps:
            pass
    else:
        for _ in range(count):
            next(steps)


def _stage_jobs(pairs):
    jobs = []
    for src, dst in pairs:
        assert src.shape[0] % STAGE_ROWS == 0 and src.shape[1] % LANES == 0
        for r0 in range(0, src.shape[0], STAGE_ROWS):
            for c0 in range(0, src.shape[1], STAGE_COLS):
                jobs.append((src, dst, r0, c0, min(STAGE_COLS, src.shape[1] - c0)))
    return jobs


def _stage_steps(jobs, stage, sem):
    copies = [pltpu.make_async_copy(src.at[pl.ds(r0, STAGE_ROWS), pl.ds(c0, width)],
                                    stage.at[i % STAGE_SLOTS, :, pl.ds(0, width)],
                                    sem.at[i % STAGE_SLOTS])
              for i, (src, _, r0, c0, width) in enumerate(jobs)]
    ahead = STAGE_SLOTS - 1
    for copy in copies[:ahead]:
        copy.start()
    for i, (_, dst, r0, c0, width) in enumerate(jobs):
        if i + ahead < len(jobs):
            copies[i + ahead].start()
        copies[i].wait()
        dst[r0:r0 + STAGE_ROWS, c0:c0 + width] = stage[i % STAGE_SLOTS, :, 0:width].astype(_BF16)
        yield


def _layer_kernel(sinks_ref, x_ref, pos_ref, invf_ref, vg_ref, vb_ref, wsp_ref, bsp_ref, g1_ref,
                  b1_ref, g2_ref, b2_ref, w_in_hbm, w_out_hbm, w1_hbm, w2_hbm, o_ref,
                  y_buf, kd_buf, vt_buf, mix_buf, w_in_ref, w_out_ref, w1_ref, w2_ref, stage_sem,
                  *, alpha, tiles_per_seq):
    j = pl.program_id(0)
    seq_start = (j % tiles_per_seq) == 0

    @pl.when(seq_start)
    def _():
        kd_buf[:, 0:BLK, :] = jnp.zeros((N_KV_HEADS, BLK, LANES), _BF16)
        vt_buf[:, 0:BLK] = jnp.zeros((D_KV, BLK), _BF16)

    mixer = functools.partial(
        _mixer_steps, seq_start, sinks_ref, x_ref, pos_ref, invf_ref, w_in_ref, vg_ref, vb_ref,
        wsp_ref, bsp_ref, w_out_ref, y_buf, kd_buf, vt_buf, mix_buf, alpha=alpha)
    half = x_ref.shape[0] // 2
    ffn = [functools.partial(_ffn_steps, slice(i * half, (i + 1) * half), y_buf, g1_ref, b1_ref,
                             w1_ref, w2_ref, g2_ref, b2_ref, o_ref, alpha=alpha) for i in range(2)]

    @pl.when(j == 0)
    def _():
        def first_step(stage):
            _run(_stage_steps(_stage_jobs([(w_in_hbm, w_in_ref)]), stage, stage_sem))
            later = _stage_steps(
                _stage_jobs([(w_out_hbm, w_out_ref), (w1_hbm, w1_ref), (w2_hbm, w2_ref)]),
                stage, stage_sem)
            assert len(_stage_jobs([(w_out_hbm, w_out_ref)])) <= STAGE_SLOTS - 1
            for i, _ in enumerate(mixer()):
                if i % STAGE_POINT_EVERY == STAGE_POINT_EVERY - 1:
                    for _ in range(STAGE_SLOTS - 1):
                        next(later, None)
            _run(later)

        pl.run_scoped(first_step, pltpu.VMEM((STAGE_SLOTS, STAGE_ROWS, STAGE_COLS), _F32))

    @pl.when(j > 0)
    def _():
        fa, fb, m = ffn[0](), ffn[1](), mixer()
        n_blk = x_ref.shape[0] // BLK
        n_proj = w_in_ref.shape[1] // MXU_WIDTH
        ffn_pieces = 2 * (w1_ref.shape[1] // FFN_CHUNK)
        mixer_pieces = [1] * (n_proj + n_blk) + [2] * n_blk + [1]
        assert len(mixer_pieces) == 2 * ffn_pieces
        _run(fa, 1)
        for count in mixer_pieces[:ffn_pieces]:
            _run(m, count)
            _run(fa, 1)
        _run(fa)
        _run(fb, 1)
        for count in mixer_pieces[ffn_pieces:]:
            _run(m, count)
            _run(fb, 1)
        _run(m)
        _run(fb)


def _resident(shape):
    return pl.BlockSpec(shape, lambda *_: (0,) * len(shape), pipeline_mode=pl.Buffered(1))


def _layer(x, pos, invf, sinks, w_in, vg, vb, wsp, bsp, w_out, g1, b1, w1, w2, g2, b2, *, alpha,
           seq):
    n_tok, d_model = x.shape
    tokens = TILE_TOKENS
    assert seq % tokens == 0 and tokens % BLK == 0 and w1.shape[1] % FFN_CHUNK == 0
    n_tiles = n_tok // tokens
    kernel = functools.partial(_layer_kernel, alpha=alpha, tiles_per_seq=seq // tokens)
    in_tile = pl.BlockSpec((tokens, d_model), lambda j: (jnp.minimum(j, n_tiles - 1), 0))
    pos_tile = pl.BlockSpec((None,) + pos.shape[1:], lambda j: (jnp.minimum(j, n_tiles - 1), 0, 0))
    out_tile = pl.BlockSpec((tokens, d_model), lambda j: (jnp.maximum(j - 1, 0), 0))
    resident = [invf, vg, vb, wsp, bsp, g1, b1, g2, b2]
    weights = [w_in, w_out, w1, w2]
    return pl.pallas_call(
        kernel,
        grid=(n_tiles + 1,),
        in_specs=[pl.BlockSpec(memory_space=pltpu.SMEM), in_tile, pos_tile]
        + [_resident(a.shape) for a in resident]
        + [pl.BlockSpec(memory_space=pl.ANY) for _ in weights],
        out_specs=out_tile,
        out_shape=jax.ShapeDtypeStruct(x.shape, x.dtype),
        scratch_shapes=[
            pltpu.VMEM((tokens, d_model), _F32),
            pltpu.VMEM((N_KV_HEADS, tokens + BLK, LANES), _BF16),
            pltpu.VMEM((D_KV, tokens + BLK), _BF16),
            pltpu.VMEM((tokens, D_GMLP + D_ATTN), _BF16),
        ] + [pltpu.VMEM(w.shape, _BF16) for w in weights] + [pltpu.SemaphoreType.DMA((STAGE_SLOTS,))],
        compiler_params=pltpu.CompilerParams(
            dimension_semantics=("arbitrary",), vmem_limit_bytes=VMEM_LIMIT_BYTES),
        name="layer",
    )(sinks, x, pos, *resident, *weights)


def kernel(x, positions, w_in, v_ln_g, v_ln_b, w_spatial, b_spatial, sinks, w_out, ln1_g, ln1_b,
           w_ff1, w_ff2, ln2_g, ln2_b):
    batch, seq, d_model = x.shape
    depth = w_in.shape[0]
    alpha = (2.0 * depth) ** 0.25
    inv_freq = ROPE_THETA ** (-jnp.arange(0, HEAD_DIM, 2, dtype=_F32) / HEAD_DIM)
    invf = jnp.tile(inv_freq, LANES // HALF)[None, :]
    pos = positions.reshape(batch * seq // TILE_TOKENS, LANES // HALF, BLK)
    x = x.reshape(batch * seq, d_model)
    row = lambda a: a[None, :]
    for l in range(depth):
        x = _layer(x, pos, invf, sinks[l], w_in[l], row(v_ln_g[l]), row(v_ln_b[l]),
                   w_spatial[l], b_spatial[l], w_out[l], row(ln1_g[l]), row(ln1_b[l]),
                   w_ff1[l], w_ff2[l], row(ln2_g[l]), row(ln2_b[l]), alpha=alpha, seq=seq)
    return x.reshape(batch, seq, d_model)
```

```python
import functools

import jax
import jax.numpy as jnp
from jax import lax
from jax.experimental import pallas as pl
from jax.experimental.pallas import tpu as pltpu

HEAD_DIM = 64
N_GMLP_HEADS = 8
D_GMLP = N_GMLP_HEADS * HEAD_DIM
N_Q_HEADS = 8
N_KV_HEADS = 2
GQA_GROUP = N_Q_HEADS // N_KV_HEADS
D_ATTN = N_Q_HEADS * HEAD_DIM
D_KV = N_KV_HEADS * HEAD_DIM
BLK = 128
ROPE_THETA = 10000.0
LN_EPS = 1e-5
NEG_INF = -1e30
LANES = 128
HALF = HEAD_DIM // 2
MXU_WIDTH = 256

TILE_TOKENS = 512
FFN_CHUNK = 1024
STAGE_ROWS, STAGE_COLS = 256, 1024
STAGE_SLOTS = 6
STAGE_POINT_EVERY = 3
VMEM_LIMIT_BYTES = 56 * 1024 * 1024

_BF16 = jnp.bfloat16
_F32 = jnp.float32


def _dot(a, b):
    return jnp.dot(a, b, preferred_element_type=_F32)


def _dot_nt(a, b):
    return lax.dot_general(a, b, (((1,), (1,)), ((), ())), preferred_element_type=_F32)


def _layer_norm(v, g, b):
    mu = jnp.mean(v, axis=-1, keepdims=True)
    c = v - mu
    var = jnp.mean(c * c, axis=-1, keepdims=True)
    return c * lax.rsqrt(var + LN_EPS) * g + b


def _mixer_steps(seq_start, sinks_ref, x_ref, pos_ref, invf_ref, w_in_ref, vg_ref, vb_ref, wsp_ref,
                 bsp_ref, w_out_ref, y_buf, kd_buf, vt_buf, mix_buf, *, alpha):
    tokens = x_ref.shape[0]
    n_blk = tokens // BLK
    n_slab_g = D_GMLP // LANES
    n_slab_a = D_ATTN // LANES
    q0 = 2 * D_GMLP
    k0 = q0 + D_ATTN

    x = x_ref[...]
    xb = x.astype(_BF16)
    lane = lax.broadcasted_iota(jnp.int32, (1, LANES), 1)
    lo = lane < HEAD_DIM
    ti = lax.broadcasted_iota(jnp.int32, (BLK, BLK), 0)
    si = lax.broadcasted_iota(jnp.int32, (BLK, BLK), 1)

    def column(row):
        return jnp.sum(jnp.where(si == ti, row, 0.0), axis=1, keepdims=True)

    def project(col0, width):
        parts = []
        for c in range(col0, col0 + width, MXU_WIDTH):
            parts.append(_dot(xb, w_in_ref[:, c:c + MXU_WIDTH]))
            yield
        return parts

    qkv = yield from project(q0, D_ATTN + 2 * D_KV)
    slabs = [part[:, i * LANES:(i + 1) * LANES] for part in qkv for i in range(MXU_WIDTH // LANES)]
    q_raw, k_raw, v = slabs[:n_slab_a], slabs[n_slab_a], slabs[n_slab_a + 1]

    pos = pos_ref[...].astype(_F32)
    invf = invf_ref[...]
    n_grp = LANES // HALF
    quarter = tokens // n_grp
    assert pos.shape == (n_grp, quarter) and quarter == BLK
    grp = lane // HALF
    ang = None
    for a in range(n_grp):
        term = column(pos[a:a + 1, :]) * jnp.where(grp == a, invf, 0.0)
        ang = term if ang is None else ang + term
    cos_packed, sin_packed = jnp.cos(ang), jnp.sin(ang)

    def spread(packed):
        parts = []
        for a in range(n_grp):
            one = jnp.where(grp == a, packed, 0.0)
            two = one + pltpu.roll(one, 2 * HALF, axis=1)
            parts.append(two + pltpu.roll(two, HALF, axis=1))
        return jnp.concatenate(parts, axis=0)

    cos = spread(cos_packed)
    first_half = (lane & HALF) == 0
    sin_signed = jnp.where(first_half, -1.0, 1.0) * spread(sin_packed)

    def rope(t):
        swapped = jnp.where(first_half, pltpu.roll(t, LANES - HALF, axis=1),
                            pltpu.roll(t, HALF, axis=1))
        return t * cos + swapped * sin_signed

    scale = HEAD_DIM ** -0.5
    q_even, q_odd = [], []
    for p in range(n_slab_a):
        qs = rope(q_raw[p]) * scale
        q_even.append(jnp.where(lo, qs, 0.0).astype(_BF16))
        q_odd.append(jnp.where(lo, 0.0, qs).astype(_BF16))
    kr = rope(k_raw)
    kr_sw = pltpu.roll(kr, HEAD_DIM, axis=1)
    cur = slice(BLK, BLK + tokens)
    kd_buf[0, cur, :] = jnp.where(lo, kr, kr_sw).astype(_BF16)
    kd_buf[1, cur, :] = jnp.where(lo, kr_sw, kr).astype(_BF16)
    vt_buf[:, cur] = v.T.astype(_BF16)

    u = jax.nn.gelu(jnp.concatenate((yield from project(0, D_GMLP)), axis=1))
    vgel = jax.nn.gelu(jnp.concatenate((yield from project(D_GMLP, D_GMLP)), axis=1))
    vn = _layer_norm(vgel, vg_ref[...], vb_ref[...])

    from_prev = ti > si
    no_prev = from_prev & (ti < jnp.where(seq_start, BLK, 0))
    probs = {}
    for n in range(n_blk):
        rows = slice(n * BLK, (n + 1) * BLK)
        kv_rows = slice(n * BLK, (n + 2) * BLK)
        for g in range(N_KV_HEADS):
            heads = range(GQA_GROUP * g, GQA_GROUP * (g + 1))
            q_stack = jnp.concatenate(
                [(q_odd if h % 2 else q_even)[h // 2][rows] for h in heads], axis=0)
            scores = _dot_nt(kd_buf[g, kv_rows, :], q_stack)
            p_parts, inv_parts = [], []
            for hl, h in enumerate(heads):
                cols = slice(hl * BLK, (hl + 1) * BLK)
                sh = jnp.where(from_prev, scores[0:BLK, cols], scores[BLK:2 * BLK, cols])
                if n == 0:
                    sh = jnp.where(no_prev, NEG_INF, sh)
                sink = sinks_ref[h]
                m = jnp.maximum(jnp.max(sh, axis=0, keepdims=True), sink)
                pr = jnp.exp(sh - m)
                l = jnp.sum(pr, axis=0, keepdims=True) + jnp.exp(sink - m)
                p_parts.append(jnp.concatenate(
                    [jnp.where(from_prev, pr, 0.0), jnp.where(from_prev, 0.0, pr)], axis=0))
                inv_parts.append(1.0 / l)
            probs[n, g] = (jnp.concatenate(p_parts, axis=1).astype(_BF16),
                           jnp.concatenate(inv_parts, axis=1))
        yield

    causal = si <= ti

    def gmlp_piece(p):
        cols = slice(p * LANES, (p + 1) * LANES)
        zero = jnp.zeros((BLK, BLK), _F32)
        w_diag = jnp.concatenate(
            [jnp.concatenate([jnp.where(causal, wsp_ref[2 * p], 0.0), zero], axis=1),
             jnp.concatenate([zero, jnp.where(causal, wsp_ref[2 * p + 1], 0.0)], axis=1)],
            axis=0).astype(_BF16)
        lhs = []
        for c in range(n_blk):
            v_t = vn[c * BLK:(c + 1) * BLK, cols].T
            lhs.append(jnp.concatenate([v_t[0:HEAD_DIM], v_t[HEAD_DIM:]], axis=1))
        mixed_t = _dot_nt(jnp.concatenate(lhs, axis=0).astype(_BF16), w_diag)
        bias = jnp.where(lo, column(bsp_ref[2 * p:2 * p + 1, :]),
                         column(bsp_ref[2 * p + 1:2 * p + 2, :]))
        for c in range(n_blk):
            rows = slice(c * BLK, (c + 1) * BLK)
            blk = mixed_t[c * HEAD_DIM:(c + 1) * HEAD_DIM]
            mixed = jnp.concatenate([blk[:, 0:BLK], blk[:, BLK:]], axis=0).T + bias
            mix_buf[rows, cols] = (u[rows, cols] * mixed).astype(_BF16)

    def values_piece(n):
        rows = slice(n * BLK, (n + 1) * BLK)
        kv_cols = slice(n * BLK, (n + 2) * BLK)
        for g in range(N_KV_HEADS):
            p_all, inv_all = probs[n, g]
            v_t = vt_buf[g * HEAD_DIM:(g + 1) * HEAD_DIM, kv_cols]
            out_t = _dot(v_t, p_all) * inv_all
            for i in range(GQA_GROUP // 2):
                slab = g * (GQA_GROUP // 2) + i
                pair = jnp.concatenate([out_t[:, (2 * i) * BLK:(2 * i + 1) * BLK],
                                        out_t[:, (2 * i + 1) * BLK:(2 * i + 2) * BLK]], axis=0)
                mix_buf[rows, D_GMLP + slab * LANES:D_GMLP + (slab + 1) * LANES] = (
                    pair.T.astype(_BF16))

    for i in range(max(n_slab_g, n_blk)):
        if i < n_slab_g:
            gmlp_piece(i)
            yield
        if i < n_blk:
            values_piece(i)
            yield
    last = slice(tokens, tokens + BLK)
    kd_buf[:, 0:BLK, :] = kd_buf[:, last, :]
    vt_buf[:, 0:BLK] = vt_buf[:, last]

    mix = mix_buf[...]
    for c in range(0, x.shape[1], MXU_WIDTH):
        y_buf[:, c:c + MXU_WIDTH] = alpha * x[:, c:c + MXU_WIDTH] + _dot(mix, w_out_ref[:, c:c + MXU_WIDTH])
        yield


def _ffn_steps(rows, y_buf, g1_ref, b1_ref, w1_ref, w2_ref, g2_ref, b2_ref, o_ref, *, alpha):
    x = _layer_norm(y_buf[rows, :], g1_ref[...], b1_ref[...])
    xb = x.astype(_BF16)
    acc = alpha * x
    yield
    for j in range(w1_ref.shape[1] // FFN_CHUNK):
        cols = slice(j * FFN_CHUNK, (j + 1) * FFN_CHUNK)
        h = jnp.maximum(_dot(xb, w1_ref[:, cols]), 0.0)
        yield
        acc = acc + _dot((h * h).astype(_BF16), w2_ref[cols, :])
        yield
    o_ref[rows, :] = _layer_norm(acc, g2_ref[...], b2_ref[...])


def _run(steps, count=None):
    if count is None:
        for _ in steps:
            pass
    else:
        for _ in range(count):
            next(steps)


def _stage_jobs(pairs):
    jobs = []
    for src, dst in pairs:
        assert src.shape[0] % STAGE_ROWS == 0 and src.shape[1] % LANES == 0
        for r0 in range(0, src.shape[0], STAGE_ROWS):
            for c0 in range(0, src.shape[1], STAGE_COLS):
                jobs.append((src, dst, r0, c0, min(STAGE_COLS, src.shape[1] - c0)))
    return jobs


def _stage_steps(jobs, stage, sem):
    copies = [pltpu.make_async_copy(src.at[pl.ds(r0, STAGE_ROWS), pl.ds(c0, width)],
                                    stage.at[i % STAGE_SLOTS, :, pl.ds(0, width)],
                                    sem.at[i % STAGE_SLOTS])
              for i, (src, _, r0, c0, width) in enumerate(jobs)]
    ahead = STAGE_SLOTS - 1
    for copy in copies[:ahead]:
        copy.start()
    for i, (_, dst, r0, c0, width) in enumerate(jobs):
        if i + ahead < len(jobs):
            copies[i + ahead].start()
        copies[i].wait()
        dst[r0:r0 + STAGE_ROWS, c0:c0 + width] = stage[i % STAGE_SLOTS, :, 0:width].astype(_BF16)
        yield


def _layer_kernel(sinks_ref, x_ref, pos_ref, invf_ref, vg_ref, vb_ref, wsp_ref, bsp_ref, g1_ref,
                  b1_ref, g2_ref, b2_ref, w_in_hbm, w_out_hbm, w1_hbm, w2_hbm, o_ref,
                  y_buf, kd_buf, vt_buf, mix_buf, w_in_ref, w_out_ref, w1_ref, w2_ref, stage_sem,
                  *, alpha, tiles_per_seq):
    j = pl.program_id(0)
    seq_start = (j % tiles_per_seq) == 0

    @pl.when(seq_start)
    def _():
        kd_buf[:, 0:BLK, :] = jnp.zeros((N_KV_HEADS, BLK, LANES), _BF16)
        vt_buf[:, 0:BLK] = jnp.zeros((D_KV, BLK), _BF16)

    mixer = functools.partial(
        _mixer_steps, seq_start, sinks_ref, x_ref, pos_ref, invf_ref, w_in_ref, vg_ref, vb_ref,
        wsp_ref, bsp_ref, w_out_ref, y_buf, kd_buf, vt_buf, mix_buf, alpha=alpha)
    half = x_ref.shape[0] // 2
    ffn = [functools.partial(_ffn_steps, slice(i * half, (i + 1) * half), y_buf, g1_ref, b1_ref,
                             w1_ref, w2_ref, g2_ref, b2_ref, o_ref, alpha=alpha) for i in range(2)]

    @pl.when(j == 0)
    def _():
        def first_step(stage):
            _run(_stage_steps(_stage_jobs([(w_in_hbm, w_in_ref)]), stage, stage_sem))
            later = _stage_steps(
                _stage_jobs([(w_out_hbm, w_out_ref), (w1_hbm, w1_ref), (w2_hbm, w2_ref)]),
                stage, stage_sem)
            assert len(_stage_jobs([(w_out_hbm, w_out_ref)])) <= STAGE_SLOTS - 1
            for i, _ in enumerate(mixer()):
                if i % STAGE_POINT_EVERY == STAGE_POINT_EVERY - 1:
                    for _ in range(STAGE_SLOTS - 1):
                        next(later, None)
            _run(later)

        pl.run_scoped(first_step, pltpu.VMEM((STAGE_SLOTS, STAGE_ROWS, STAGE_COLS), _F32))

    @pl.when(j > 0)
    def _():
        fa, fb, m = ffn[0](), ffn[1](), mixer()
        n_blk = x_ref.shape[0] // BLK
        n_proj = w_in_ref.shape[1] // MXU_WIDTH
        ffn_pieces = 2 * (w1_ref.shape[1] // FFN_CHUNK)
        mixer_pieces = [1] * (n_proj + n_blk) + [2] * n_blk + [1]
        assert len(mixer_pieces) == 2 * ffn_pieces
        _run(fa, 1)
        for count in mixer_pieces[:ffn_pieces]:
            _run(m, count)
            _run(fa, 1)
        _run(fa)
        _run(fb, 1)
        for count in mixer_pieces[ffn_pieces:]:
            _run(m, count)
            _run(fb, 1)
        _run(m)
        _run(fb)


def _resident(shape):
    return pl.BlockSpec(shape, lambda *_: (0,) * len(shape), pipeline_mode=pl.Buffered(1))


def _layer(x, pos, invf, sinks, w_in, vg, vb, wsp, bsp, w_out, g1, b1, w1, w2, g2, b2, *, alpha,
           seq):
    n_tok, d_model = x.shape
    tokens = TILE_TOKENS
    assert seq % tokens == 0 and tokens % BLK == 0 and w1.shape[1] % FFN_CHUNK == 0
    n_tiles = n_tok // tokens
    kernel = functools.partial(_layer_kernel, alpha=alpha, tiles_per_seq=seq // tokens)
    in_tile = pl.BlockSpec((tokens, d_model), lambda j: (jnp.minimum(j, n_tiles - 1), 0))
    pos_tile = pl.BlockSpec((None,) + pos.shape[1:], lambda j: (jnp.minimum(j, n_tiles - 1), 0, 0))
    out_tile = pl.BlockSpec((tokens, d_model), lambda j: (jnp.maximum(j - 1, 0), 0))
    resident = [invf, vg, vb, wsp, bsp, g1, b1, g2, b2]
    weights = [w_in, w_out, w1, w2]
    return pl.pallas_call(
        kernel,
        grid=(n_tiles + 1,),
        in_specs=[pl.BlockSpec(memory_space=pltpu.SMEM), in_tile, pos_tile]
        + [_resident(a.shape) for a in resident]
        + [pl.BlockSpec(memory_space=pl.ANY) for _ in weights],
        out_specs=out_tile,
        out_shape=jax.ShapeDtypeStruct(x.shape, x.dtype),
        scratch_shapes=[
            pltpu.VMEM((tokens, d_model), _F32),
            pltpu.VMEM((N_KV_HEADS, tokens + BLK, LANES), _BF16),
            pltpu.VMEM((D_KV, tokens + BLK), _BF16),
            pltpu.VMEM((tokens, D_GMLP + D_ATTN), _BF16),
        ] + [pltpu.VMEM(w.shape, _BF16) for w in weights] + [pltpu.SemaphoreType.DMA((STAGE_SLOTS,))],
        compiler_params=pltpu.CompilerParams(
            dimension_semantics=("arbitrary",), vmem_limit_bytes=VMEM_LIMIT_BYTES),
        name="layer",
    )(sinks, x, pos, *resident, *weights)


def kernel(x, positions, w_in, v_ln_g, v_ln_b, w_spatial, b_spatial, sinks, w_out, ln1_g, ln1_b,
           w_ff1, w_ff2, ln2_g, ln2_b):
    batch, seq, d_model = x.shape
    depth = w_in.shape[0]
    alpha = (2.0 * depth) ** 0.25
    inv_freq = ROPE_THETA ** (-jnp.arange(0, HEAD_DIM, 2, dtype=_F32) / HEAD_DIM)
    invf = jnp.tile(inv_freq, LANES // HALF)[None, :]
    pos = positions.reshape(batch * seq // TILE_TOKENS, LANES // HALF, BLK)
    x = x.reshape(batch * seq, d_model)
    row = lambda a: a[None, :]
    for l in range(depth):
        x = _layer(x, pos, invf, sinks[l], w_in[l], row(v_ln_g[l]), row(v_ln_b[l]),
                   w_spatial[l], b_spatial[l], w_out[l], row(ln1_g[l]), row(ln1_b[l]),
                   w_ff1[l], w_ff2[l], row(ln2_g[l]), row(ln2_b[l]), alpha=alpha, seq=seq)
    return x.reshape(batch, seq, d_model)
```

```python
import functools

import jax
import jax.numpy as jnp
from jax import lax
from jax.experimental import pallas as pl
from jax.experimental.pallas import tpu as pltpu

HEAD_DIM = 64
N_GMLP_HEADS = 8
D_GMLP = N_GMLP_HEADS * HEAD_DIM
N_Q_HEADS = 8
N_KV_HEADS = 2
GQA_GROUP = N_Q_HEADS // N_KV_HEADS
D_ATTN = N_Q_HEADS * HEAD_DIM
D_KV = N_KV_HEADS * HEAD_DIM
BLK = 128
ROPE_THETA = 10000.0
LN_EPS = 1e-5
NEG_INF = -1e30
LANES = 128
HALF = HEAD_DIM // 2
MXU_WIDTH = 256

TILE_TOKENS = 512
FFN_CHUNK = 1024
STAGE_ROWS, STAGE_COLS = 256, 1024
STAGE_SLOTS = 6
STAGE_POINT_EVERY = 3
VMEM_LIMIT_BYTES = 56 * 1024 * 1024

_BF16 = jnp.bfloat16
_F32 = jnp.float32


def _dot(a, b):
    return jnp.dot(a, b, preferred_element_type=_F32)


def _dot_nt(a, b):
    return lax.dot_general(a, b, (((1,), (1,)), ((), ())), preferred_element_type=_F32)


def _layer_norm(v, g, b):
    mu = jnp.mean(v, axis=-1, keepdims=True)
    c = v - mu
    var = jnp.mean(c * c, axis=-1, keepdims=True)
    return c * lax.rsqrt(var + LN_EPS) * g + b


def _mixer_steps(seq_start, sinks_ref, x_ref, pos_row, invf_ref, w_in_ref, vg_ref, vb_ref, wsp_ref,
                 bsp_ref, w_out_ref, y_buf, kd_buf, vt_buf, mix_buf, *, alpha):
    tokens = x_ref.shape[0]
    n_blk = tokens // BLK
    n_slab_g = D_GMLP // LANES
    n_slab_a = D_ATTN // LANES
    q0 = 2 * D_GMLP
    k0 = q0 + D_ATTN

    x = x_ref[...]
    xb = x.astype(_BF16)
    lane = lax.broadcasted_iota(jnp.int32, (1, LANES), 1)
    lo = lane < HEAD_DIM
    ti = lax.broadcasted_iota(jnp.int32, (BLK, BLK), 0)
    si = lax.broadcasted_iota(jnp.int32, (BLK, BLK), 1)

    def column(row):
        return jnp.sum(jnp.where(si == ti, row, 0.0), axis=1, keepdims=True)

    def project(col0, width):
        parts = []
        for c in range(col0, col0 + width, MXU_WIDTH):
            parts.append(_dot(xb, w_in_ref[:, c:c + MXU_WIDTH]))
            yield
        return parts

    qkv = yield from project(q0, D_ATTN + 2 * D_KV)
    slabs = [part[:, i * LANES:(i + 1) * LANES] for part in qkv for i in range(MXU_WIDTH // LANES)]
    q_raw, k_raw, v = slabs[:n_slab_a], slabs[n_slab_a], slabs[n_slab_a + 1]

    pos = pos_row.astype(_F32)
    invf = invf_ref[...]
    n_grp = LANES // HALF
    quarter = tokens // n_grp
    assert pos.shape == (1, tokens) and quarter == BLK
    grp = lane // HALF
    ang = None
    for a in range(n_grp):
        term = (column(pos[:, a * quarter:(a + 1) * quarter])
                * jnp.where(grp == a, invf, 0.0))
        ang = term if ang is None else ang + term
    cos_packed, sin_packed = jnp.cos(ang), jnp.sin(ang)

    def spread(packed):
        parts = []
        for a in range(n_grp):
            one = jnp.where(grp == a, packed, 0.0)
            two = one + pltpu.roll(one, 2 * HALF, axis=1)
            parts.append(two + pltpu.roll(two, HALF, axis=1))
        return jnp.concatenate(parts, axis=0)

    cos = spread(cos_packed)
    first_half = (lane & HALF) == 0
    sin_signed = jnp.where(first_half, -1.0, 1.0) * spread(sin_packed)

    def rope(t):
        swapped = jnp.where(first_half, pltpu.roll(t, LANES - HALF, axis=1),
                            pltpu.roll(t, HALF, axis=1))
        return t * cos + swapped * sin_signed

    scale = HEAD_DIM ** -0.5
    q_even, q_odd = [], []
    for p in range(n_slab_a):
        qs = rope(q_raw[p]) * scale
        q_even.append(jnp.where(lo, qs, 0.0).astype(_BF16))
        q_odd.append(jnp.where(lo, 0.0, qs).astype(_BF16))
    kr = rope(k_raw)
    kr_sw = pltpu.roll(kr, HEAD_DIM, axis=1)
    cur = slice(BLK, BLK + tokens)
    kd_buf[0, cur, :] = jnp.where(lo, kr, kr_sw).astype(_BF16)
    kd_buf[1, cur, :] = jnp.where(lo, kr_sw, kr).astype(_BF16)
    vt_buf[:, cur] = v.T.astype(_BF16)

    u = jax.nn.gelu(jnp.concatenate((yield from project(0, D_GMLP)), axis=1))
    vgel = jax.nn.gelu(jnp.concatenate((yield from project(D_GMLP, D_GMLP)), axis=1))
    vn = _layer_norm(vgel, vg_ref[...], vb_ref[...])
    lo_g = (lax.broadcasted_iota(jnp.int32, (1, D_GMLP), 1) & HEAD_DIM) == 0
    vn_top = jnp.where(lo_g, vn, 0.0).astype(_BF16)
    vn_bot = jnp.where(lo_g, 0.0, vn).astype(_BF16)

    from_prev = ti > si
    no_prev = from_prev & (ti < jnp.where(seq_start, BLK, 0))
    probs = {}
    for n in range(n_blk):
        rows = slice(n * BLK, (n + 1) * BLK)
        kv_rows = slice(n * BLK, (n + 2) * BLK)
        for g in range(N_KV_HEADS):
            heads = range(GQA_GROUP * g, GQA_GROUP * (g + 1))
            q_stack = jnp.concatenate(
                [(q_odd if h % 2 else q_even)[h // 2][rows] for h in heads], axis=0)
            scores = _dot_nt(kd_buf[g, kv_rows, :], q_stack)
            p_parts, inv_parts = [], []
            for hl, h in enumerate(heads):
                cols = slice(hl * BLK, (hl + 1) * BLK)
                sh = jnp.where(from_prev, scores[0:BLK, cols], scores[BLK:2 * BLK, cols])
                if n == 0:
                    sh = jnp.where(no_prev, NEG_INF, sh)
                sink = sinks_ref[h]
                m = jnp.maximum(jnp.max(sh, axis=0, keepdims=True), sink)
                pr = jnp.exp(sh - m)
                l = jnp.sum(pr, axis=0, keepdims=True) + jnp.exp(sink - m)
                p_parts.append(jnp.concatenate(
                    [jnp.where(from_prev, pr, 0.0), jnp.where(from_prev, 0.0, pr)], axis=0))
                inv_parts.append(1.0 / l)
            probs[n, g] = (jnp.concatenate(p_parts, axis=1).astype(_BF16),
                           jnp.concatenate(inv_parts, axis=1))
        yield

    causal = si <= ti

    def gmlp_piece(p):
        w_pair = jnp.concatenate(
            [jnp.where(causal, wsp_ref[2 * p], 0.0), jnp.where(causal, wsp_ref[2 * p + 1], 0.0)],
            axis=1).astype(_BF16)
        bias = jnp.where(lo, column(bsp_ref[2 * p:2 * p + 1, :]),
                         column(bsp_ref[2 * p + 1:2 * p + 2, :]))
        for c in range(n_blk):
            rows = slice(c * BLK, (c + 1) * BLK)
            cols = slice(p * LANES, (p + 1) * LANES)
            rhs = jnp.concatenate([vn_top[rows, cols], vn_bot[rows, cols]], axis=0)
            mixed = _dot(w_pair, rhs) + bias
            mix_buf[rows, cols] = (u[rows, cols] * mixed).astype(_BF16)

    def values_piece(n):
        rows = slice(n * BLK, (n + 1) * BLK)
        kv_cols = slice(n * BLK, (n + 2) * BLK)
        for g in range(N_KV_HEADS):
            p_all, inv_all = probs[n, g]
            v_t = vt_buf[g * HEAD_DIM:(g + 1) * HEAD_DIM, kv_cols]
            out_t = _dot(v_t, p_all) * inv_all
            for i in range(GQA_GROUP // 2):
                slab = g * (GQA_GROUP // 2) + i
                pair = jnp.concatenate([out_t[:, (2 * i) * BLK:(2 * i + 1) * BLK],
                                        out_t[:, (2 * i + 1) * BLK:(2 * i + 2) * BLK]], axis=0)
                mix_buf[rows, D_GMLP + slab * LANES:D_GMLP + (slab + 1) * LANES] = (
                    pair.T.astype(_BF16))

    for i in range(max(n_slab_g, n_blk)):
        if i < n_slab_g:
            gmlp_piece(i)
            yield
        if i < n_blk:
            values_piece(i)
            yield
    last = slice(tokens, tokens + BLK)
    kd_buf[:, 0:BLK, :] = kd_buf[:, last, :]
    vt_buf[:, 0:BLK] = vt_buf[:, last]

    mix = mix_buf[...]
    for c in range(0, x.shape[1], MXU_WIDTH):
        y_buf[:, c:c + MXU_WIDTH] = alpha * x[:, c:c + MXU_WIDTH] + _dot(mix, w_out_ref[:, c:c + MXU_WIDTH])
        yield


def _ffn_steps(rows, y_buf, g1_ref, b1_ref, w1_ref, w2_ref, g2_ref, b2_ref, o_ref, *, alpha):
    x = _layer_norm(y_buf[rows, :], g1_ref[...], b1_ref[...])
    xb = x.astype(_BF16)
    acc = alpha * x
    yield
    for j in range(w1_ref.shape[1] // FFN_CHUNK):
        cols = slice(j * FFN_CHUNK, (j + 1) * FFN_CHUNK)
        h = jnp.maximum(_dot(xb, w1_ref[:, cols]), 0.0)
        yield
        acc = acc + _dot((h * h).astype(_BF16), w2_ref[cols, :])
        yield
    o_ref[rows, :] = _layer_norm(acc, g2_ref[...], b2_ref[...])


def _run(steps, count=None):
    if count is None:
        for _ in steps:
            pass
    else:
        for _ in range(count):
            next(steps)


def _stage_jobs(pairs):
    jobs = []
    for src, dst in pairs:
        assert src.shape[0] % STAGE_ROWS == 0 and src.shape[1] % LANES == 0
        for r0 in range(0, src.shape[0], STAGE_ROWS):
            for c0 in range(0, src.shape[1], STAGE_COLS):
                jobs.append((src, dst, r0, c0, min(STAGE_COLS, src.shape[1] - c0)))
    return jobs


def _stage_steps(jobs, stage, sem):
    copies = [pltpu.make_async_copy(src.at[pl.ds(r0, STAGE_ROWS), pl.ds(c0, width)],
                                    stage.at[i % STAGE_SLOTS, :, pl.ds(0, width)],
                                    sem.at[i % STAGE_SLOTS])
              for i, (src, _, r0, c0, width) in enumerate(jobs)]
    ahead = STAGE_SLOTS - 1
    for copy in copies[:ahead]:
        copy.start()
    for i, (_, dst, r0, c0, width) in enumerate(jobs):
        if i + ahead < len(jobs):
            copies[i + ahead].start()
        copies[i].wait()
        dst[r0:r0 + STAGE_ROWS, c0:c0 + width] = stage[i % STAGE_SLOTS, :, 0:width].astype(_BF16)
        yield


def _layer_kernel(sinks_ref, x_ref, pos_ref, invf_ref, vg_ref, vb_ref, wsp_ref, bsp_ref, g1_ref,
                  b1_ref, g2_ref, b2_ref, w_in_hbm, w_out_hbm, w1_hbm, w2_hbm, o_ref,
                  y_buf, kd_buf, vt_buf, mix_buf, w_in_ref, w_out_ref, w1_ref, w2_ref, stage_sem,
                  *, alpha, n_tiles, tiles_per_seq):
    j = pl.program_id(0)
    seq_start = (j % tiles_per_seq) == 0
    tile = jnp.minimum(j, n_tiles - 1)
    tokens = x_ref.shape[0]
    pos_row = pos_ref[pl.ds(tile // tiles_per_seq, 1),
                      pl.ds(pl.multiple_of((tile % tiles_per_seq) * tokens, tokens), tokens)]

    @pl.when(seq_start)
    def _():
        kd_buf[:, 0:BLK, :] = jnp.zeros((N_KV_HEADS, BLK, LANES), _BF16)
        vt_buf[:, 0:BLK] = jnp.zeros((D_KV, BLK), _BF16)

    mixer = functools.partial(
        _mixer_steps, seq_start, sinks_ref, x_ref, pos_row, invf_ref, w_in_ref, vg_ref, vb_ref,
        wsp_ref, bsp_ref, w_out_ref, y_buf, kd_buf, vt_buf, mix_buf, alpha=alpha)
    half = x_ref.shape[0] // 2
    ffn = [functools.partial(_ffn_steps, slice(i * half, (i + 1) * half), y_buf, g1_ref, b1_ref,
                             w1_ref, w2_ref, g2_ref, b2_ref, o_ref, alpha=alpha) for i in range(2)]

    @pl.when(j == 0)
    def _():
        def first_step(stage):
            _run(_stage_steps(_stage_jobs([(w_in_hbm, w_in_ref)]), stage, stage_sem))
            later = _stage_steps(
                _stage_jobs([(w_out_hbm, w_out_ref), (w1_hbm, w1_ref), (w2_hbm, w2_ref)]),
                stage, stage_sem)
            assert len(_stage_jobs([(w_out_hbm, w_out_ref)])) <= STAGE_SLOTS - 1
            for i, _ in enumerate(mixer()):
                if i % STAGE_POINT_EVERY == STAGE_POINT_EVERY - 1:
                    for _ in range(STAGE_SLOTS - 1):
                        next(later, None)
            _run(later)

        pl.run_scoped(first_step, pltpu.VMEM((STAGE_SLOTS, STAGE_ROWS, STAGE_COLS), _F32))

    @pl.when(j > 0)
    def _():
        fa, fb, m = ffn[0](), ffn[1](), mixer()
        n_blk = x_ref.shape[0] // BLK
        n_proj = w_in_ref.shape[1] // MXU_WIDTH
        ffn_pieces = 2 * (w1_ref.shape[1] // FFN_CHUNK)
        mixer_pieces = [1] * (n_proj + n_blk) + [2] * n_blk + [1]
        assert len(mixer_pieces) == 2 * ffn_pieces
        _run(fa, 1)
        for count in mixer_pieces[:ffn_pieces]:
            _run(m, count)
            _run(fa, 1)
        _run(fa)
        _run(fb, 1)
        for count in mixer_pieces[ffn_pieces:]:
            _run(m, count)
            _run(fb, 1)
        _run(m)
        _run(fb)


def _resident(shape):
    return pl.BlockSpec(shape, lambda *_: (0,) * len(shape), pipeline_mode=pl.Buffered(1))


def _layer(x, pos, invf, sinks, w_in, vg, vb, wsp, bsp, w_out, g1, b1, w1, w2, g2, b2, *, alpha,
           seq):
    n_tok, d_model = x.shape
    tokens = TILE_TOKENS
    assert seq % tokens == 0 and tokens % BLK == 0 and w1.shape[1] % FFN_CHUNK == 0
    n_tiles = n_tok // tokens
    kernel = functools.partial(_layer_kernel, alpha=alpha, n_tiles=n_tiles,
                               tiles_per_seq=seq // tokens)
    in_tile = pl.BlockSpec((tokens, d_model), lambda j: (jnp.minimum(j, n_tiles - 1), 0))
    out_tile = pl.BlockSpec((tokens, d_model), lambda j: (jnp.maximum(j - 1, 0), 0))
    resident = [pos, invf, vg, vb, wsp, bsp, g1, b1, g2, b2]
    weights = [w_in, w_out, w1, w2]
    return pl.pallas_call(
        kernel,
        grid=(n_tiles + 1,),
        in_specs=[pl.BlockSpec(memory_space=pltpu.SMEM), in_tile]
        + [_resident(a.shape) for a in resident]
        + [pl.BlockSpec(memory_space=pl.ANY) for _ in weights],
        out_specs=out_tile,
        out_shape=jax.ShapeDtypeStruct(x.shape, x.dtype),
        scratch_shapes=[
            pltpu.VMEM((tokens, d_model), _F32),
            pltpu.VMEM((N_KV_HEADS, tokens + BLK, LANES), _BF16),
            pltpu.VMEM((D_KV, tokens + BLK), _BF16),
            pltpu.VMEM((tokens, D_GMLP + D_ATTN), _BF16),
        ] + [pltpu.VMEM(w.shape, _BF16) for w in weights] + [pltpu.SemaphoreType.DMA((STAGE_SLOTS,))],
        compiler_params=pltpu.CompilerParams(
            dimension_semantics=("arbitrary",), vmem_limit_bytes=VMEM_LIMIT_BYTES),
        name="layer",
    )(sinks, x, *resident, *weights)


def kernel(x, positions, w_in, v_ln_g, v_ln_b, w_spatial, b_spatial, sinks, w_out, ln1_g, ln1_b,
           w_ff1, w_ff2, ln2_g, ln2_b):
    batch, seq, d_model = x.shape
    depth = w_in.shape[0]
    alpha = (2.0 * depth) ** 0.25
    inv_freq = ROPE_THETA ** (-jnp.arange(0, HEAD_DIM, 2, dtype=_F32) / HEAD_DIM)
    invf = jnp.tile(inv_freq, LANES // HALF)[None, :]
    pos = positions
    x = x.reshape(batch * seq, d_model)
    row = lambda a: a[None, :]
    for l in range(depth):
        x = _layer(x, pos, invf, sinks[l], w_in[l], row(v_ln_g[l]), row(v_ln_b[l]),
                   w_spatial[l], b_spatial[l], w_out[l], row(ln1_g[l]), row(ln1_b[l]),
                   w_ff1[l], w_ff2[l], row(ln2_g[l]), row(ln2_b[l]), alpha=alpha, seq=seq)
    return x.reshape(batch, seq, d_model)
```

```python
import functools

import jax
import jax.numpy as jnp
from jax import lax
from jax.experimental import pallas as pl
from jax.experimental.pallas import tpu as pltpu

HEAD_DIM = 64
N_GMLP_HEADS = 8
D_GMLP = N_GMLP_HEADS * HEAD_DIM
N_Q_HEADS = 8
N_KV_HEADS = 2
GQA_GROUP = N_Q_HEADS // N_KV_HEADS
D_ATTN = N_Q_HEADS * HEAD_DIM
D_KV = N_KV_HEADS * HEAD_DIM
BLK = 128
ROPE_THETA = 10000.0
LN_EPS = 1e-5
NEG_INF = -1e30
LANES = 128
HALF = HEAD_DIM // 2
MXU_WIDTH = 256

TILE_TOKENS = 512
FFN_CHUNK = 1024
STAGE_ROWS, STAGE_COLS = 256, 1024
STAGE_SLOTS = 6
STAGE_POINT_EVERY = 3
VMEM_LIMIT_BYTES = 56 * 1024 * 1024

_BF16 = jnp.bfloat16
_F32 = jnp.float32


def _dot(a, b):
    return jnp.dot(a, b, preferred_element_type=_F32)


def _dot_nt(a, b):
    return lax.dot_general(a, b, (((1,), (1,)), ((), ())), preferred_element_type=_F32)


def _layer_norm(v, g, b):
    mu = jnp.mean(v, axis=-1, keepdims=True)
    c = v - mu
    var = jnp.mean(c * c, axis=-1, keepdims=True)
    return c * lax.rsqrt(var + LN_EPS) * g + b


def _mixer_steps(seq_start, sinks_ref, x_ref, pos_row, invf_ref, w_in_ref, vg_ref, vb_ref, wsp_ref,
                 bsp_ref, w_out_ref, y_buf, kd_buf, vt_buf, mix_buf, *, alpha):
    tokens = x_ref.shape[0]
    n_blk = tokens // BLK
    n_slab_g = D_GMLP // LANES
    n_slab_a = D_ATTN // LANES
    q0 = 2 * D_GMLP
    k0 = q0 + D_ATTN

    x = x_ref[...]
    xb = x.astype(_BF16)
    lane = lax.broadcasted_iota(jnp.int32, (1, LANES), 1)
    lo = lane < HEAD_DIM
    ti = lax.broadcasted_iota(jnp.int32, (BLK, BLK), 0)
    si = lax.broadcasted_iota(jnp.int32, (BLK, BLK), 1)

    def column(row):
        return jnp.sum(jnp.where(si == ti, row, 0.0), axis=1, keepdims=True)

    def project(col0, width):
        parts = []
        for c in range(col0, col0 + width, MXU_WIDTH):
            parts.append(_dot(xb, w_in_ref[:, c:c + MXU_WIDTH]))
            yield
        return parts

    qkv = yield from project(q0, D_ATTN + 2 * D_KV)
    slabs = [part[:, i * LANES:(i + 1) * LANES] for part in qkv for i in range(MXU_WIDTH // LANES)]
    q_raw, k_raw, v = slabs[:n_slab_a], slabs[n_slab_a], slabs[n_slab_a + 1]

    pos = pos_row.astype(_F32)
    invf = invf_ref[...]
    n_grp = LANES // HALF
    quarter = tokens // n_grp
    assert pos.shape == (1, tokens) and quarter == BLK
    grp = lane // HALF
    ang = None
    for a in range(n_grp):
        term = (column(pos[:, a * quarter:(a + 1) * quarter])
                * jnp.where(grp == a, invf, 0.0))
        ang = term if ang is None else ang + term
    cos_packed, sin_packed = jnp.cos(ang), jnp.sin(ang)

    def spread(packed):
        parts = []
        for a in range(n_grp):
            one = jnp.where(grp == a, packed, 0.0)
            two = one + pltpu.roll(one, 2 * HALF, axis=1)
            parts.append(two + pltpu.roll(two, HALF, axis=1))
        return jnp.concatenate(parts, axis=0)

    cos = spread(cos_packed)
    first_half = (lane & HALF) == 0
    sin_signed = jnp.where(first_half, -1.0, 1.0) * spread(sin_packed)

    def rope(t):
        swapped = jnp.where(first_half, pltpu.roll(t, LANES - HALF, axis=1),
                            pltpu.roll(t, HALF, axis=1))
        return t * cos + swapped * sin_signed

    scale = HEAD_DIM ** -0.5
    q_even, q_odd = [], []
    for p in range(n_slab_a):
        qs = rope(q_raw[p]) * scale
        q_even.append(jnp.where(lo, qs, 0.0).astype(_BF16))
        q_odd.append(jnp.where(lo, 0.0, qs).astype(_BF16))
    kr = rope(k_raw)
    kr_sw = pltpu.roll(kr, HEAD_DIM, axis=1)
    cur = slice(BLK, BLK + tokens)
    kd_buf[0, cur, :] = jnp.where(lo, kr, kr_sw).astype(_BF16)
    kd_buf[1, cur, :] = jnp.where(lo, kr_sw, kr).astype(_BF16)
    vt_buf[:, cur] = v.T.astype(_BF16)

    u = jax.nn.gelu(jnp.concatenate((yield from project(0, D_GMLP)), axis=1))
    vgel = jax.nn.gelu(jnp.concatenate((yield from project(D_GMLP, D_GMLP)), axis=1))
    vn = _layer_norm(vgel, vg_ref[...], vb_ref[...])
    lo_g = (lax.broadcasted_iota(jnp.int32, (1, D_GMLP), 1) & HEAD_DIM) == 0
    vn_top = jnp.where(lo_g, vn, 0.0).astype(_BF16)
    vn_bot = jnp.where(lo_g, 0.0, vn).astype(_BF16)

    from_prev = ti > si
    no_prev = from_prev & (ti < jnp.where(seq_start, BLK, 0))
    probs = {}
    for n in range(n_blk):
        rows = slice(n * BLK, (n + 1) * BLK)
        kv_rows = slice(n * BLK, (n + 2) * BLK)
        for g in range(N_KV_HEADS):
            heads = range(GQA_GROUP * g, GQA_GROUP * (g + 1))
            q_stack = jnp.concatenate(
                [(q_odd if h % 2 else q_even)[h // 2][rows] for h in heads], axis=0)
            scores = _dot_nt(kd_buf[g, kv_rows, :], q_stack)
            p_parts, inv_parts = [], []
            for hl, h in enumerate(heads):
                cols = slice(hl * BLK, (hl + 1) * BLK)
                sh = jnp.where(from_prev, scores[0:BLK, cols], scores[BLK:2 * BLK, cols])
                if n == 0:
                    sh = jnp.where(no_prev, NEG_INF, sh)
                sink = sinks_ref[h]
                m = jnp.maximum(jnp.max(sh, axis=0, keepdims=True), sink)
                pr = jnp.exp(sh - m)
                l = jnp.sum(pr, axis=0, keepdims=True) + jnp.exp(sink - m)
                p_parts.append(jnp.concatenate(
                    [jnp.where(from_prev, pr, 0.0), jnp.where(from_prev, 0.0, pr)], axis=0))
                inv_parts.append(1.0 / l)
            probs[n, g] = (jnp.concatenate(p_parts, axis=1).astype(_BF16),
                           jnp.concatenate(inv_parts, axis=1))
        yield

    causal = si <= ti

    def gmlp_piece(p):
        w_pair = jnp.concatenate(
            [jnp.where(causal, wsp_ref[2 * p], 0.0), jnp.where(causal, wsp_ref[2 * p + 1], 0.0)],
            axis=1).astype(_BF16)
        bias = jnp.where(lo, column(bsp_ref[2 * p:2 * p + 1, :]),
                         column(bsp_ref[2 * p + 1:2 * p + 2, :]))
        for c in range(n_blk):
            rows = slice(c * BLK, (c + 1) * BLK)
            cols = slice(p * LANES, (p + 1) * LANES)
            rhs = jnp.concatenate([vn_top[rows, cols], vn_bot[rows, cols]], axis=0)
            mixed = _dot(w_pair, rhs) + bias
            mix_buf[rows, cols] = (u[rows, cols] * mixed).astype(_BF16)

    def values_piece(n):
        rows = slice(n * BLK, (n + 1) * BLK)
        kv_cols = slice(n * BLK, (n + 2) * BLK)
        for g in range(N_KV_HEADS):
            p_all, inv_all = probs[n, g]
            v_t = vt_buf[g * HEAD_DIM:(g + 1) * HEAD_DIM, kv_cols]
            out_t = _dot(v_t, p_all) * inv_all
            for i in range(GQA_GROUP // 2):
                slab = g * (GQA_GROUP // 2) + i
                pair = jnp.concatenate([out_t[:, (2 * i) * BLK:(2 * i + 1) * BLK],
                                        out_t[:, (2 * i + 1) * BLK:(2 * i + 2) * BLK]], axis=0)
                mix_buf[rows, D_GMLP + slab * LANES:D_GMLP + (slab + 1) * LANES] = (
                    pair.T.astype(_BF16))

    for i in range(max(n_slab_g, n_blk)):
        if i < n_slab_g:
            gmlp_piece(i)
            yield
        if i < n_blk:
            values_piece(i)
            yield
    last = slice(tokens, tokens + BLK)
    kd_buf[:, 0:BLK, :] = kd_buf[:, last, :]
    vt_buf[:, 0:BLK] = vt_buf[:, last]

    mix = mix_buf[...]
    for c in range(0, x.shape[1], MXU_WIDTH):
        y_buf[:, c:c + MXU_WIDTH] = alpha * x[:, c:c + MXU_WIDTH] + _dot(mix, w_out_ref[:, c:c + MXU_WIDTH])
        yield


def _ffn_steps(rows, y_buf, g1_ref, b1_ref, w1_ref, w2_ref, g2_ref, b2_ref, o_ref, *, alpha):
    x = _layer_norm(y_buf[rows, :], g1_ref[...], b1_ref[...])
    xb = x.astype(_BF16)
    acc = alpha * x
    yield
    for j in range(w1_ref.shape[1] // FFN_CHUNK):
        cols = slice(j * FFN_CHUNK, (j + 1) * FFN_CHUNK)
        h = jnp.maximum(_dot(xb, w1_ref[:, cols]), 0.0)
        yield
        acc = acc + _dot((h * h).astype(_BF16), w2_ref[cols, :])
        yield
    o_ref[rows, :] = _layer_norm(acc, g2_ref[...], b2_ref[...])


def _run(steps, count=None):
    if count is None:
        for _ in steps:
            pass
    else:
        for _ in range(count):
            next(steps)


def _stage_jobs(pairs):
    jobs = []
    for src, dst in pairs:
        assert src.shape[0] % STAGE_ROWS == 0 and src.shape[1] % LANES == 0
        for r0 in range(0, src.shape[0], STAGE_ROWS):
            for c0 in range(0, src.shape[1], STAGE_COLS):
                jobs.append((src, dst, r0, c0, min(STAGE_COLS, src.shape[1] - c0)))
    return jobs


def _stage_steps(jobs, stage, sem):
    copies = [pltpu.make_async_copy(src.at[pl.ds(r0, STAGE_ROWS), pl.ds(c0, width)],
                                    stage.at[i % STAGE_SLOTS, :, pl.ds(0, width)],
                                    sem.at[i % STAGE_SLOTS])
              for i, (src, _, r0, c0, width) in enumerate(jobs)]
    ahead = STAGE_SLOTS - 1
    for copy in copies[:ahead]:
        copy.start()
    for i, (_, dst, r0, c0, width) in enumerate(jobs):
        if i + ahead < len(jobs):
            copies[i + ahead].start()
        copies[i].wait()
        dst[r0:r0 + STAGE_ROWS, c0:c0 + width] = stage[i % STAGE_SLOTS, :, 0:width].astype(_BF16)
        yield


def _layer_kernel(sinks_ref, x_ref, pos_ref, invf_ref, vg_ref, vb_ref, wsp_ref, bsp_ref, g1_ref,
                  b1_ref, g2_ref, b2_ref, w_in_hbm, w_out_hbm, w1_hbm, w2_hbm, o_ref,
                  y_buf, kd_buf, vt_buf, mix_buf, w_in_ref, w_out_ref, w1_ref, w2_ref, stage_sem,
                  *, alpha, n_tiles, tiles_per_seq):
    j = pl.program_id(0)
    seq_start = (j % tiles_per_seq) == 0
    tile = jnp.minimum(j, n_tiles - 1)
    tokens = x_ref.shape[0]
    pos_row = pos_ref[pl.ds(tile // tiles_per_seq, 1),
                      pl.ds(pl.multiple_of((tile % tiles_per_seq) * tokens, tokens), tokens)]

    @pl.when(seq_start)
    def _():
        kd_buf[:, 0:BLK, :] = jnp.zeros((N_KV_HEADS, BLK, LANES), _BF16)
        vt_buf[:, 0:BLK] = jnp.zeros((D_KV, BLK), _BF16)

    mixer = functools.partial(
        _mixer_steps, seq_start, sinks_ref, x_ref, pos_row, invf_ref, w_in_ref, vg_ref, vb_ref,
        wsp_ref, bsp_ref, w_out_ref, y_buf, kd_buf, vt_buf, mix_buf, alpha=alpha)
    half = x_ref.shape[0] // 2
    ffn = [functools.partial(_ffn_steps, slice(i * half, (i + 1) * half), y_buf, g1_ref, b1_ref,
                             w1_ref, w2_ref, g2_ref, b2_ref, o_ref, alpha=alpha) for i in range(2)]

    @pl.when(j == 0)
    def _():
        def first_step(stage):
            _run(_stage_steps(_stage_jobs([(w_in_hbm, w_in_ref)]), stage, stage_sem))
            later = _stage_steps(
                _stage_jobs([(w_out_hbm, w_out_ref), (w1_hbm, w1_ref), (w2_hbm, w2_ref)]),
                stage, stage_sem)
            assert len(_stage_jobs([(w_out_hbm, w_out_ref)])) <= STAGE_SLOTS - 1
            for i, _ in enumerate(mixer()):
                if i % STAGE_POINT_EVERY == STAGE_POINT_EVERY - 1:
                    for _ in range(STAGE_SLOTS - 1):
                        next(later, None)
            _run(later)

        pl.run_scoped(first_step, pltpu.VMEM((STAGE_SLOTS, STAGE_ROWS, STAGE_COLS), _F32))

    @pl.when(j > 0)
    def _():
        fa, fb, m = ffn[0](), ffn[1](), mixer()
        n_blk = x_ref.shape[0] // BLK
        n_proj = w_in_ref.shape[1] // MXU_WIDTH
        ffn_pieces = 2 * (w1_ref.shape[1] // FFN_CHUNK)
        mixer_pieces = [2] + [1] * (n_proj + n_blk - 2) + [2] * n_blk + [1, 0]
        assert len(mixer_pieces) == 2 * ffn_pieces
        _run(fa, 1)
        for count in mixer_pieces[:ffn_pieces]:
            _run(m, count)
            _run(fa, 1)
        _run(fa)
        _run(fb, 1)
        for count in mixer_pieces[ffn_pieces:]:
            _run(m, count)
            _run(fb, 1)
        _run(m)
        _run(fb)


def _resident(shape):
    return pl.BlockSpec(shape, lambda *_: (0,) * len(shape), pipeline_mode=pl.Buffered(1))


def _layer(x, pos, invf, sinks, w_in, vg, vb, wsp, bsp, w_out, g1, b1, w1, w2, g2, b2, *, alpha,
           seq):
    n_tok, d_model = x.shape
    tokens = TILE_TOKENS
    assert seq % tokens == 0 and tokens % BLK == 0 and w1.shape[1] % FFN_CHUNK == 0
    n_tiles = n_tok // tokens
    kernel = functools.partial(_layer_kernel, alpha=alpha, n_tiles=n_tiles,
                               tiles_per_seq=seq // tokens)
    in_tile = pl.BlockSpec((tokens, d_model), lambda j: (jnp.minimum(j, n_tiles - 1), 0))
    out_tile = pl.BlockSpec((tokens, d_model), lambda j: (jnp.maximum(j - 1, 0), 0))
    resident = [pos, invf, vg, vb, wsp, bsp, g1, b1, g2, b2]
    weights = [w_in, w_out, w1, w2]
    return pl.pallas_call(
        kernel,
        grid=(n_tiles + 1,),
        in_specs=[pl.BlockSpec(memory_space=pltpu.SMEM), in_tile]
        + [_resident(a.shape) for a in resident]
        + [pl.BlockSpec(memory_space=pl.ANY) for _ in weights],
        out_specs=out_tile,
        out_shape=jax.ShapeDtypeStruct(x.shape, x.dtype),
        scratch_shapes=[
            pltpu.VMEM((tokens, d_model), _F32),
            pltpu.VMEM((N_KV_HEADS, tokens + BLK, LANES), _BF16),
            pltpu.VMEM((D_KV, tokens + BLK), _BF16),
            pltpu.VMEM((tokens, D_GMLP + D_ATTN), _BF16),
        ] + [pltpu.VMEM(w.shape, _BF16) for w in weights] + [pltpu.SemaphoreType.DMA((STAGE_SLOTS,))],
        compiler_params=pltpu.CompilerParams(
            dimension_semantics=("arbitrary",), vmem_limit_bytes=VMEM_LIMIT_BYTES),
        name="layer",
    )(sinks, x, *resident, *weights)


def kernel(x, positions, w_in, v_ln_g, v_ln_b, w_spatial, b_spatial, sinks, w_out, ln1_g, ln1_b,
           w_ff1, w_ff2, ln2_g, ln2_b):
    batch, seq, d_model = x.shape
    depth = w_in.shape[0]
    alpha = (2.0 * depth) ** 0.25
    inv_freq = ROPE_THETA ** (-jnp.arange(0, HEAD_DIM, 2, dtype=_F32) / HEAD_DIM)
    invf = jnp.tile(inv_freq, LANES // HALF)[None, :]
    pos = positions
    x = x.reshape(batch * seq, d_model)
    row = lambda a: a[None, :]
    for l in range(depth):
        x = _layer(x, pos, invf, sinks[l], w_in[l], row(v_ln_g[l]), row(v_ln_b[l]),
                   w_spatial[l], b_spatial[l], w_out[l], row(ln1_g[l]), row(ln1_b[l]),
                   w_ff1[l], w_ff2[l], row(ln2_g[l]), row(ln2_b[l]), alpha=alpha, seq=seq)
    return x.reshape(batch, seq, d_model)
```

```python
import functools

import jax
import jax.numpy as jnp
from jax import lax
from jax.experimental import pallas as pl
from jax.experimental.pallas import tpu as pltpu

HEAD_DIM = 64
N_GMLP_HEADS = 8
D_GMLP = N_GMLP_HEADS * HEAD_DIM
N_Q_HEADS = 8
N_KV_HEADS = 2
GQA_GROUP = N_Q_HEADS // N_KV_HEADS
D_ATTN = N_Q_HEADS * HEAD_DIM
D_KV = N_KV_HEADS * HEAD_DIM
BLK = 128
ROPE_THETA = 10000.0
LN_EPS = 1e-5
NEG_INF = -1e30
LANES = 128
HALF = HEAD_DIM // 2
MXU_WIDTH = 256

TILE_TOKENS = 512
FFN_CHUNK = 1024
STAGE_ROWS, STAGE_COLS = 256, 1024
STAGE_SLOTS = 6
STAGE_POINT_EVERY = 3
VMEM_LIMIT_BYTES = 56 * 1024 * 1024

_BF16 = jnp.bfloat16
_F32 = jnp.float32


def _dot(a, b):
    return jnp.dot(a, b, preferred_element_type=_F32)


def _dot_nt(a, b):
    return lax.dot_general(a, b, (((1,), (1,)), ((), ())), preferred_element_type=_F32)


def _layer_norm(v, g, b):
    mu = jnp.mean(v, axis=-1, keepdims=True)
    c = v - mu
    var = jnp.mean(c * c, axis=-1, keepdims=True)
    return c * lax.rsqrt(var + LN_EPS) * g + b


def _mixer_steps(seq_start, sinks_ref, x_ref, pos_row, invf_ref, w_in_ref, vg_ref, vb_ref, wsp_ref,
                 bsp_ref, w_out_ref, y_buf, kd_buf, vt_buf, mix_buf, *, alpha):
    tokens = x_ref.shape[0]
    n_blk = tokens // BLK
    n_slab_g = D_GMLP // LANES
    n_slab_a = D_ATTN // LANES
    q0 = 2 * D_GMLP
    k0 = q0 + D_ATTN

    x = x_ref[...]
    xb = x.astype(_BF16)
    lane = lax.broadcasted_iota(jnp.int32, (1, LANES), 1)
    lo = lane < HEAD_DIM
    ti = lax.broadcasted_iota(jnp.int32, (BLK, BLK), 0)
    si = lax.broadcasted_iota(jnp.int32, (BLK, BLK), 1)

    def column(row):
        return jnp.sum(jnp.where(si == ti, row, 0.0), axis=1, keepdims=True)

    def project(col0, width):
        parts = []
        for c in range(col0, col0 + width, MXU_WIDTH):
            parts.append(_dot(xb, w_in_ref[:, c:c + MXU_WIDTH]))
            yield
        return parts

    qkv = yield from project(q0, D_ATTN + 2 * D_KV)
    slabs = [part[:, i * LANES:(i + 1) * LANES] for part in qkv for i in range(MXU_WIDTH // LANES)]
    q_raw, k_raw, v = slabs[:n_slab_a], slabs[n_slab_a], slabs[n_slab_a + 1]

    pos = pos_row.astype(_F32)
    invf = invf_ref[...]
    n_grp = LANES // HALF
    quarter = tokens // n_grp
    assert pos.shape == (1, tokens) and quarter == BLK
    grp = lane // HALF
    ang = None
    for a in range(n_grp):
        term = (column(pos[:, a * quarter:(a + 1) * quarter])
                * jnp.where(grp == a, invf, 0.0))
        ang = term if ang is None else ang + term
    cos_packed, sin_packed = jnp.cos(ang), jnp.sin(ang)

    def spread(packed):
        parts = []
        for a in range(n_grp):
            one = jnp.where(grp == a, packed, 0.0)
            two = one + pltpu.roll(one, 2 * HALF, axis=1)
            parts.append(two + pltpu.roll(two, HALF, axis=1))
        return jnp.concatenate(parts, axis=0)

    cos = spread(cos_packed)
    first_half = (lane & HALF) == 0
    sin_signed = jnp.where(first_half, -1.0, 1.0) * spread(sin_packed)

    def rope(t):
        swapped = jnp.where(first_half, pltpu.roll(t, LANES - HALF, axis=1),
                            pltpu.roll(t, HALF, axis=1))
        return t * cos + swapped * sin_signed

    scale = HEAD_DIM ** -0.5
    q_even, q_odd = [], []
    for p in range(n_slab_a):
        qs = rope(q_raw[p]) * scale
        q_even.append(jnp.where(lo, qs, 0.0).astype(_BF16))
        q_odd.append(jnp.where(lo, 0.0, qs).astype(_BF16))
    kr = rope(k_raw)
    kr_sw = pltpu.roll(kr, HEAD_DIM, axis=1)
    cur = slice(BLK, BLK + tokens)
    kd_buf[0, cur, :] = jnp.where(lo, kr, kr_sw).astype(_BF16)
    kd_buf[1, cur, :] = jnp.where(lo, kr_sw, kr).astype(_BF16)
    vt_buf[:, cur] = v.T.astype(_BF16)

    u = jax.nn.gelu(jnp.concatenate((yield from project(0, D_GMLP)), axis=1))
    vgel = jax.nn.gelu(jnp.concatenate((yield from project(D_GMLP, D_GMLP)), axis=1))
    vn = _layer_norm(vgel, vg_ref[...], vb_ref[...])
    lo_g = (lax.broadcasted_iota(jnp.int32, (1, D_GMLP), 1) & HEAD_DIM) == 0
    vn_top = jnp.where(lo_g, vn, 0.0).astype(_BF16)
    vn_bot = jnp.where(lo_g, 0.0, vn).astype(_BF16)

    from_prev = ti > si
    no_prev = from_prev & (ti < jnp.where(seq_start, BLK, 0))
    probs = {}
    for n in range(n_blk):
        rows = slice(n * BLK, (n + 1) * BLK)
        kv_rows = slice(n * BLK, (n + 2) * BLK)
        for g in range(N_KV_HEADS):
            heads = range(GQA_GROUP * g, GQA_GROUP * (g + 1))
            q_stack = jnp.concatenate(
                [(q_odd if h % 2 else q_even)[h // 2][rows] for h in heads], axis=0)
            scores = _dot_nt(kd_buf[g, kv_rows, :], q_stack)
            p_parts, inv_parts = [], []
            for hl, h in enumerate(heads):
                cols = slice(hl * BLK, (hl + 1) * BLK)
                sh = jnp.where(from_prev, scores[0:BLK, cols], scores[BLK:2 * BLK, cols])
                if n == 0:
                    sh = jnp.where(no_prev, NEG_INF, sh)
                sink = sinks_ref[h]
                m = jnp.maximum(jnp.max(sh, axis=0, keepdims=True), sink)
                pr = jnp.exp(sh - m)
                l = jnp.sum(pr, axis=0, keepdims=True) + jnp.exp(sink - m)
                p_parts.append(jnp.concatenate(
                    [jnp.where(from_prev, pr, 0.0), jnp.where(from_prev, 0.0, pr)], axis=0))
                inv_parts.append(1.0 / l)
            probs[n, g] = (jnp.concatenate(p_parts, axis=1).astype(_BF16),
                           jnp.concatenate(inv_parts, axis=1))
        yield

    causal = si <= ti

    def gmlp_piece(p):
        w_pair = jnp.concatenate(
            [jnp.where(causal, wsp_ref[2 * p], 0.0), jnp.where(causal, wsp_ref[2 * p + 1], 0.0)],
            axis=1).astype(_BF16)
        bias = jnp.where(lo, column(bsp_ref[2 * p:2 * p + 1, :]),
                         column(bsp_ref[2 * p + 1:2 * p + 2, :]))
        for c in range(n_blk):
            rows = slice(c * BLK, (c + 1) * BLK)
            cols = slice(p * LANES, (p + 1) * LANES)
            rhs = jnp.concatenate([vn_top[rows, cols], vn_bot[rows, cols]], axis=0)
            mixed = _dot(w_pair, rhs) + bias
            mix_buf[rows, cols] = (u[rows, cols] * mixed).astype(_BF16)

    def values_piece(n):
        rows = slice(n * BLK, (n + 1) * BLK)
        kv_cols = slice(n * BLK, (n + 2) * BLK)
        for g in range(N_KV_HEADS):
            p_all, inv_all = probs[n, g]
            v_t = vt_buf[g * HEAD_DIM:(g + 1) * HEAD_DIM, kv_cols]
            out_t = _dot(v_t, p_all) * inv_all
            for i in range(GQA_GROUP // 2):
                slab = g * (GQA_GROUP // 2) + i
                pair = jnp.concatenate([out_t[:, (2 * i) * BLK:(2 * i + 1) * BLK],
                                        out_t[:, (2 * i + 1) * BLK:(2 * i + 2) * BLK]], axis=0)
                mix_buf[rows, D_GMLP + slab * LANES:D_GMLP + (slab + 1) * LANES] = (
                    pair.T.astype(_BF16))

    for i in range(max(n_slab_g, n_blk)):
        if i < n_slab_g:
            gmlp_piece(i)
            yield
        if i < n_blk:
            values_piece(i)
            yield
    last = slice(tokens, tokens + BLK)
    kd_buf[:, 0:BLK, :] = kd_buf[:, last, :]
    vt_buf[:, 0:BLK] = vt_buf[:, last]

    mix = mix_buf[...]
    for c in range(0, x.shape[1], MXU_WIDTH):
        y_buf[:, c:c + MXU_WIDTH] = alpha * x[:, c:c + MXU_WIDTH] + _dot(mix, w_out_ref[:, c:c + MXU_WIDTH])
        yield


def _ffn_steps(rows, y_buf, g1_ref, b1_ref, w1_ref, w2_ref, g2_ref, b2_ref, o_ref, *, alpha):
    x = _layer_norm(y_buf[rows, :], g1_ref[...], b1_ref[...])
    xb = x.astype(_BF16)
    acc = alpha * x
    yield
    for j in range(w1_ref.shape[1] // FFN_CHUNK):
        cols = slice(j * FFN_CHUNK, (j + 1) * FFN_CHUNK)
        h = jnp.maximum(_dot(xb, w1_ref[:, cols]), 0.0)
        yield
        acc = acc + _dot((h * h).astype(_BF16), w2_ref[cols, :])
        yield
    o_ref[rows, :] = _layer_norm(acc, g2_ref[...], b2_ref[...])


def _run(steps, count=None):
    if count is None:
        for _ in steps:
            pass
    else:
        for _ in range(count):
            next(steps)


def _stage_jobs(pairs):
    jobs = []
    for src, dst in pairs:
        assert src.shape[0] % STAGE_ROWS == 0 and src.shape[1] % LANES == 0
        for r0 in range(0, src.shape[0], STAGE_ROWS):
            for c0 in range(0, src.shape[1], STAGE_COLS):
                jobs.append((src, dst, r0, c0, min(STAGE_COLS, src.shape[1] - c0)))
    return jobs


def _stage_steps(jobs, stage, sem):
    copies = [pltpu.make_async_copy(src.at[pl.ds(r0, STAGE_ROWS), pl.ds(c0, width)],
                                    stage.at[i % STAGE_SLOTS, :, pl.ds(0, width)],
                                    sem.at[i % STAGE_SLOTS])
              for i, (src, _, r0, c0, width) in enumerate(jobs)]
    ahead = STAGE_SLOTS - 1
    for copy in copies[:ahead]:
        copy.start()
    for i, (_, dst, r0, c0, width) in enumerate(jobs):
        if i + ahead < len(jobs):
            copies[i + ahead].start()
        copies[i].wait()
        dst[r0:r0 + STAGE_ROWS, c0:c0 + width] = stage[i % STAGE_SLOTS, :, 0:width].astype(_BF16)
        yield


def _layer_kernel(sinks_ref, x_ref, pos_ref, invf_ref, vg_ref, vb_ref, wsp_ref, bsp_ref, g1_ref,
                  b1_ref, g2_ref, b2_ref, w_in_hbm, w_out_hbm, w1_hbm, w2_hbm, o_ref,
                  y_buf, kd_buf, vt_buf, mix_buf, w_in_ref, w_out_ref, w1_ref, w2_ref, stage_sem,
                  *, alpha, n_tiles, tiles_per_seq):
    j = pl.program_id(0)
    seq_start = (j % tiles_per_seq) == 0
    tile = jnp.minimum(j, n_tiles - 1)
    tokens = x_ref.shape[0]
    pos_row = pos_ref[pl.ds(tile // tiles_per_seq, 1),
                      pl.ds(pl.multiple_of((tile % tiles_per_seq) * tokens, tokens), tokens)]

    @pl.when(seq_start)
    def _():
        kd_buf[:, 0:BLK, :] = jnp.zeros((N_KV_HEADS, BLK, LANES), _BF16)
        vt_buf[:, 0:BLK] = jnp.zeros((D_KV, BLK), _BF16)

    mixer = functools.partial(
        _mixer_steps, seq_start, sinks_ref, x_ref, pos_row, invf_ref, w_in_ref, vg_ref, vb_ref,
        wsp_ref, bsp_ref, w_out_ref, y_buf, kd_buf, vt_buf, mix_buf, alpha=alpha)
    half = x_ref.shape[0] // 2
    ffn = [functools.partial(_ffn_steps, slice(i * half, (i + 1) * half), y_buf, g1_ref, b1_ref,
                             w1_ref, w2_ref, g2_ref, b2_ref, o_ref, alpha=alpha) for i in range(2)]

    @pl.when(j == 0)
    def _():
        def first_step(stage):
            _run(_stage_steps(_stage_jobs([(w_in_hbm, w_in_ref)]), stage, stage_sem))
            later = _stage_steps(
                _stage_jobs([(w_out_hbm, w_out_ref), (w1_hbm, w1_ref), (w2_hbm, w2_ref)]),
                stage, stage_sem)
            assert len(_stage_jobs([(w_out_hbm, w_out_ref)])) <= STAGE_SLOTS - 1
            for i, _ in enumerate(mixer()):
                if i % STAGE_POINT_EVERY == STAGE_POINT_EVERY - 1:
                    for _ in range(STAGE_SLOTS - 1):
                        next(later, None)
            _run(later)

        pl.run_scoped(first_step, pltpu.VMEM((STAGE_SLOTS, STAGE_ROWS, STAGE_COLS), _F32))

    @pl.when(j > 0)
    def _():
        fa, fb, m = ffn[0](), ffn[1](), mixer()
        n_blk = x_ref.shape[0] // BLK
        n_proj = w_in_ref.shape[1] // MXU_WIDTH
        ffn_pieces = 2 * (w1_ref.shape[1] // FFN_CHUNK)
        mixer_pieces = [1] * (n_proj + n_blk) + [2] * n_blk + [1]
        assert len(mixer_pieces) == 2 * ffn_pieces
        _run(fa, 1)
        for count in mixer_pieces[:ffn_pieces]:
            _run(fa, 1)
            _run(m, count)
        _run(fa)
        _run(fb, 1)
        for count in mixer_pieces[ffn_pieces:]:
            _run(fb, 1)
            _run(m, count)
        _run(m)
        _run(fb)


def _resident(shape):
    return pl.BlockSpec(shape, lambda *_: (0,) * len(shape), pipeline_mode=pl.Buffered(1))


def _layer(x, pos, invf, sinks, w_in, vg, vb, wsp, bsp, w_out, g1, b1, w1, w2, g2, b2, *, alpha,
           seq):
    n_tok, d_model = x.shape
    tokens = TILE_TOKENS
    assert seq % tokens == 0 and tokens % BLK == 0 and w1.shape[1] % FFN_CHUNK == 0
    n_tiles = n_tok // tokens
    kernel = functools.partial(_layer_kernel, alpha=alpha, n_tiles=n_tiles,
                               tiles_per_seq=seq // tokens)
    in_tile = pl.BlockSpec((tokens, d_model), lambda j: (jnp.minimum(j, n_tiles - 1), 0))
    out_tile = pl.BlockSpec((tokens, d_model), lambda j: (jnp.maximum(j - 1, 0), 0))
    resident = [pos, invf, vg, vb, wsp, bsp, g1, b1, g2, b2]
    weights = [w_in, w_out, w1, w2]
    return pl.pallas_call(
        kernel,
        grid=(n_tiles + 1,),
        in_specs=[pl.BlockSpec(memory_space=pltpu.SMEM), in_tile]
        + [_resident(a.shape) for a in resident]
        + [pl.BlockSpec(memory_space=pl.ANY) for _ in weights],
        out_specs=out_tile,
        out_shape=jax.ShapeDtypeStruct(x.shape, x.dtype),
        scratch_shapes=[
            pltpu.VMEM((tokens, d_model), _F32),
            pltpu.VMEM((N_KV_HEADS, tokens + BLK, LANES), _BF16),
            pltpu.VMEM((D_KV, tokens + BLK), _BF16),
            pltpu.VMEM((tokens, D_GMLP + D_ATTN), _BF16),
        ] + [pltpu.VMEM(w.shape, _BF16) for w in weights] + [pltpu.SemaphoreType.DMA((STAGE_SLOTS,))],
        compiler_params=pltpu.CompilerParams(
            dimension_semantics=("arbitrary",), vmem_limit_bytes=VMEM_LIMIT_BYTES),
        name="layer",
    )(sinks, x, *resident, *weights)


def kernel(x, positions, w_in, v_ln_g, v_ln_b, w_spatial, b_spatial, sinks, w_out, ln1_g, ln1_b,
           w_ff1, w_ff2, ln2_g, ln2_b):
    batch, seq, d_model = x.shape
    depth = w_in.shape[0]
    alpha = (2.0 * depth) ** 0.25
    inv_freq = ROPE_THETA ** (-jnp.arange(0, HEAD_DIM, 2, dtype=_F32) / HEAD_DIM)
    invf = jnp.tile(inv_freq, LANES // HALF)[None, :]
    pos = positions
    x = x.reshape(batch * seq, d_model)
    row = lambda a: a[None, :]
    for l in range(depth):
        x = _layer(x, pos, invf, sinks[l], w_in[l], row(v_ln_g[l]), row(v_ln_b[l]),
                   w_spatial[l], b_spatial[l], w_out[l], row(ln1_g[l]), row(ln1_b[l]),
                   w_ff1[l], w_ff2[l], row(ln2_g[l]), row(ln2_b[l]), alpha=alpha, seq=seq)
    return x.reshape(batch, seq, d_model)
```

```python
import functools

import jax
import jax.numpy as jnp
from jax import lax
from jax.experimental import pallas as pl
from jax.experimental.pallas import tpu as pltpu

HEAD_DIM = 64
N_GMLP_HEADS = 8
D_GMLP = N_GMLP_HEADS * HEAD_DIM
N_Q_HEADS = 8
N_KV_HEADS = 2
GQA_GROUP = N_Q_HEADS // N_KV_HEADS
D_ATTN = N_Q_HEADS * HEAD_DIM
D_KV = N_KV_HEADS * HEAD_DIM
BLK = 128
ROPE_THETA = 10000.0
LN_EPS = 1e-5
NEG_INF = -1e30
LANES = 128
HALF = HEAD_DIM // 2
MXU_WIDTH = 256

TILE_TOKENS = 512
FFN_CHUNK = 1024
STAGE_ROWS, STAGE_COLS = 256, 1024
STAGE_SLOTS = 6
STAGE_POINT_EVERY = 3
VMEM_LIMIT_BYTES = 56 * 1024 * 1024

_BF16 = jnp.bfloat16
_F32 = jnp.float32


def _dot(a, b):
    return jnp.dot(a, b, preferred_element_type=_F32)


def _dot_nt(a, b):
    return lax.dot_general(a, b, (((1,), (1,)), ((), ())), preferred_element_type=_F32)


def _layer_norm(v, g, b):
    mu = jnp.mean(v, axis=-1, keepdims=True)
    c = v - mu
    var = jnp.mean(c * c, axis=-1, keepdims=True)
    return c * lax.rsqrt(var + LN_EPS) * g + b


def _mixer_steps(seq_start, sinks_ref, x_ref, pos_row, invf_ref, w_in_ref, vg_ref, vb_ref, wsp_ref,
                 bsp_ref, w_out_ref, y_buf, kd_buf, vt_buf, mix_buf, *, alpha):
    tokens = x_ref.shape[0]
    n_blk = tokens // BLK
    n_slab_g = D_GMLP // LANES
    n_slab_a = D_ATTN // LANES
    q0 = 2 * D_GMLP
    k0 = q0 + D_ATTN

    x = x_ref[...]
    xb = x.astype(_BF16)
    lane = lax.broadcasted_iota(jnp.int32, (1, LANES), 1)
    lo = lane < HEAD_DIM
    ti = lax.broadcasted_iota(jnp.int32, (BLK, BLK), 0)
    si = lax.broadcasted_iota(jnp.int32, (BLK, BLK), 1)

    def column(row):
        return jnp.sum(jnp.where(si == ti, row, 0.0), axis=1, keepdims=True)

    def project(col0, width):
        parts = []
        for c in range(col0, col0 + width, MXU_WIDTH):
            parts.append(_dot(xb, w_in_ref[:, c:c + MXU_WIDTH]))
            yield
        return parts

    qkv = yield from project(q0, D_ATTN + 2 * D_KV)
    slabs = [part[:, i * LANES:(i + 1) * LANES] for part in qkv for i in range(MXU_WIDTH // LANES)]
    q_raw, k_raw, v = slabs[:n_slab_a], slabs[n_slab_a], slabs[n_slab_a + 1]

    pos = pos_row.astype(_F32)
    invf = invf_ref[...]
    n_grp = LANES // HALF
    quarter = tokens // n_grp
    assert pos.shape == (1, tokens) and quarter == BLK
    grp = lane // HALF
    ang = None
    for a in range(n_grp):
        term = (column(pos[:, a * quarter:(a + 1) * quarter])
                * jnp.where(grp == a, invf, 0.0))
        ang = term if ang is None else ang + term
    cos_packed, sin_packed = jnp.cos(ang), jnp.sin(ang)

    def spread(packed):
        parts = []
        for a in range(n_grp):
            one = jnp.where(grp == a, packed, 0.0)
            two = one + pltpu.roll(one, 2 * HALF, axis=1)
            parts.append(two + pltpu.roll(two, HALF, axis=1))
        return jnp.concatenate(parts, axis=0)

    cos = spread(cos_packed)
    first_half = (lane & HALF) == 0
    sin_signed = jnp.where(first_half, -1.0, 1.0) * spread(sin_packed)

    def rope(t):
        swapped = jnp.where(first_half, pltpu.roll(t, LANES - HALF, axis=1),
                            pltpu.roll(t, HALF, axis=1))
        return t * cos + swapped * sin_signed

    scale = HEAD_DIM ** -0.5
    q_even, q_odd = [], []
    for p in range(n_slab_a):
        qs = rope(q_raw[p]) * scale
        q_even.append(jnp.where(lo, qs, 0.0).astype(_BF16))
        q_odd.append(jnp.where(lo, 0.0, qs).astype(_BF16))
    kr = rope(k_raw)
    kr_sw = pltpu.roll(kr, HEAD_DIM, axis=1)
    cur = slice(BLK, BLK + tokens)
    kd_buf[0, cur, :] = jnp.where(lo, kr, kr_sw).astype(_BF16)
    kd_buf[1, cur, :] = jnp.where(lo, kr_sw, kr).astype(_BF16)
    vt_buf[:, cur] = v.T.astype(_BF16)

    u = jax.nn.gelu(jnp.concatenate((yield from project(0, D_GMLP)), axis=1))
    vgel = jax.nn.gelu(jnp.concatenate((yield from project(D_GMLP, D_GMLP)), axis=1))
    vn = _layer_norm(vgel, vg_ref[...], vb_ref[...])
    lo_g = (lax.broadcasted_iota(jnp.int32, (1, D_GMLP), 1) & HEAD_DIM) == 0
    vn_top = jnp.where(lo_g, vn, 0.0).astype(_BF16)
    vn_bot = jnp.where(lo_g, 0.0, vn).astype(_BF16)

    from_prev = ti > si
    no_prev = from_prev & (ti < jnp.where(seq_start, BLK, 0))
    probs = {}
    for n in range(n_blk):
        rows = slice(n * BLK, (n + 1) * BLK)
        kv_rows = slice(n * BLK, (n + 2) * BLK)
        for g in range(N_KV_HEADS):
            heads = range(GQA_GROUP * g, GQA_GROUP * (g + 1))
            q_stack = jnp.concatenate(
                [(q_odd if h % 2 else q_even)[h // 2][rows] for h in heads], axis=0)
            scores = _dot_nt(kd_buf[g, kv_rows, :], q_stack)
            p_parts, inv_parts = [], []
            for hl, h in enumerate(heads):
                cols = slice(hl * BLK, (hl + 1) * BLK)
                sh = jnp.where(from_prev, scores[0:BLK, cols], scores[BLK:2 * BLK, cols])
                if n == 0:
                    sh = jnp.where(no_prev, NEG_INF, sh)
                sink = sinks_ref[h]
                m = jnp.maximum(jnp.max(sh, axis=0, keepdims=True), sink)
                pr = jnp.exp(sh - m)
                l = jnp.sum(pr, axis=0, keepdims=True) + jnp.exp(sink - m)
                p_parts.append(jnp.concatenate(
                    [jnp.where(from_prev, pr, 0.0), jnp.where(from_prev, 0.0, pr)], axis=0))
                inv_parts.append(1.0 / l)
            probs[n, g] = (jnp.concatenate(p_parts, axis=1).astype(_BF16),
                           jnp.concatenate(inv_parts, axis=1))
        yield

    causal = si <= ti

    def gmlp_piece(p):
        w_pair = jnp.concatenate(
            [jnp.where(causal, wsp_ref[2 * p], 0.0), jnp.where(causal, wsp_ref[2 * p + 1], 0.0)],
            axis=1).astype(_BF16)
        bias = jnp.where(lo, column(bsp_ref[2 * p:2 * p + 1, :]),
                         column(bsp_ref[2 * p + 1:2 * p + 2, :]))
        for c in range(n_blk):
            rows = slice(c * BLK, (c + 1) * BLK)
            cols = slice(p * LANES, (p + 1) * LANES)
            rhs = jnp.concatenate([vn_top[rows, cols], vn_bot[rows, cols]], axis=0)
            mixed = _dot(w_pair, rhs) + bias
            mix_buf[rows, cols] = (u[rows, cols] * mixed).astype(_BF16)

    def values_piece(n):
        rows = slice(n * BLK, (n + 1) * BLK)
        kv_cols = slice(n * BLK, (n + 2) * BLK)
        for g in range(N_KV_HEADS):
            p_all, inv_all = probs[n, g]
            v_t = vt_buf[g * HEAD_DIM:(g + 1) * HEAD_DIM, kv_cols]
            out_t = _dot(v_t, p_all) * inv_all
            for i in range(GQA_GROUP // 2):
                slab = g * (GQA_GROUP // 2) + i
                pair = jnp.concatenate([out_t[:, (2 * i) * BLK:(2 * i + 1) * BLK],
                                        out_t[:, (2 * i + 1) * BLK:(2 * i + 2) * BLK]], axis=0)
                mix_buf[rows, D_GMLP + slab * LANES:D_GMLP + (slab + 1) * LANES] = (
                    pair.T.astype(_BF16))

    for i in range(max(n_slab_g, n_blk)):
        if i < n_slab_g:
            gmlp_piece(i)
            yield
        if i < n_blk:
            values_piece(i)
            yield
    last = slice(tokens, tokens + BLK)
    kd_buf[:, 0:BLK, :] = kd_buf[:, last, :]
    vt_buf[:, 0:BLK] = vt_buf[:, last]

    mix = mix_buf[...]
    for c in range(0, x.shape[1], MXU_WIDTH):
        y_buf[:, c:c + MXU_WIDTH] = alpha * x[:, c:c + MXU_WIDTH] + _dot(mix, w_out_ref[:, c:c + MXU_WIDTH])
        yield


def _ffn_steps(rows, y_buf, g1_ref, b1_ref, w1_ref, w2_ref, g2_ref, b2_ref, o_ref, *, alpha):
    x = _layer_norm(y_buf[rows, :], g1_ref[...], b1_ref[...])
    xb = x.astype(_BF16)
    acc = alpha * x
    yield
    for j in range(w1_ref.shape[1] // FFN_CHUNK):
        cols = slice(j * FFN_CHUNK, (j + 1) * FFN_CHUNK)
        h = jnp.maximum(_dot(xb, w1_ref[:, cols]), 0.0)
        yield
        acc = acc + _dot((h * h).astype(_BF16), w2_ref[cols, :])
        yield
    o_ref[rows, :] = _layer_norm(acc, g2_ref[...], b2_ref[...])


def _run(steps, count=None):
    if count is None:
        for _ in steps:
            pass
    else:
        for _ in range(count):
            next(steps)


def _stage_jobs(pairs):
    jobs = []
    for src, dst in pairs:
        assert src.shape[0] % STAGE_ROWS == 0 and src.shape[1] % LANES == 0
        for r0 in range(0, src.shape[0], STAGE_ROWS):
            for c0 in range(0, src.shape[1], STAGE_COLS):
                jobs.append((src, dst, r0, c0, min(STAGE_COLS, src.shape[1] - c0)))
    return jobs


def _stage_steps(jobs, stage, sem):
    copies = [pltpu.make_async_copy(src.at[pl.ds(r0, STAGE_ROWS), pl.ds(c0, width)],
                                    stage.at[i % STAGE_SLOTS, :, pl.ds(0, width)],
                                    sem.at[i % STAGE_SLOTS])
              for i, (src, _, r0, c0, width) in enumerate(jobs)]
    ahead = STAGE_SLOTS - 1
    for copy in copies[:ahead]:
        copy.start()
    for i, (_, dst, r0, c0, width) in enumerate(jobs):
        if i + ahead < len(jobs):
            copies[i + ahead].start()
        copies[i].wait()
        dst[r0:r0 + STAGE_ROWS, c0:c0 + width] = stage[i % STAGE_SLOTS, :, 0:width].astype(_BF16)
        yield


def _layer_kernel(sinks_ref, x_ref, pos_ref, invf_ref, vg_ref, vb_ref, wsp_ref, bsp_ref, g1_ref,
                  b1_ref, g2_ref, b2_ref, w_in_hbm, w_out_hbm, w1_hbm, w2_hbm, o_ref,
                  y_buf, kd_buf, vt_buf, mix_buf, w_in_ref, w_out_ref, w1_ref, w2_ref, stage_sem,
                  *, alpha, n_tiles, tiles_per_seq):
    j = pl.program_id(0)
    seq_start = (j % tiles_per_seq) == 0
    tile = jnp.minimum(j, n_tiles - 1)
    tokens = x_ref.shape[0]
    pos_row = pos_ref[pl.ds(tile // tiles_per_seq, 1),
                      pl.ds(pl.multiple_of((tile % tiles_per_seq) * tokens, tokens), tokens)]

    @pl.when(seq_start)
    def _():
        kd_buf[:, 0:BLK, :] = jnp.zeros((N_KV_HEADS, BLK, LANES), _BF16)
        vt_buf[:, 0:BLK] = jnp.zeros((D_KV, BLK), _BF16)

    mixer = functools.partial(
        _mixer_steps, seq_start, sinks_ref, x_ref, pos_row, invf_ref, w_in_ref, vg_ref, vb_ref,
        wsp_ref, bsp_ref, w_out_ref, y_buf, kd_buf, vt_buf, mix_buf, alpha=alpha)
    half = x_ref.shape[0] // 2
    ffn = [functools.partial(_ffn_steps, slice(i * half, (i + 1) * half), y_buf, g1_ref, b1_ref,
                             w1_ref, w2_ref, g2_ref, b2_ref, o_ref, alpha=alpha) for i in range(2)]

    @pl.when(j == 0)
    def _():
        def first_step(stage):
            _run(_stage_steps(_stage_jobs([(w_in_hbm, w_in_ref)]), stage, stage_sem))
            later = _stage_steps(
                _stage_jobs([(w_out_hbm, w_out_ref), (w1_hbm, w1_ref), (w2_hbm, w2_ref)]),
                stage, stage_sem)
            assert len(_stage_jobs([(w_out_hbm, w_out_ref)])) <= STAGE_SLOTS - 1
            for i, _ in enumerate(mixer()):
                if i % STAGE_POINT_EVERY == STAGE_POINT_EVERY - 1:
                    for _ in range(STAGE_SLOTS - 1):
                        next(later, None)
            _run(later)

        pl.run_scoped(first_step, pltpu.VMEM((STAGE_SLOTS, STAGE_ROWS, STAGE_COLS), _F32))

    @pl.when(j > 0)
    def _():
        fa, fb, m = ffn[0](), ffn[1](), mixer()
        n_blk = x_ref.shape[0] // BLK
        n_proj = w_in_ref.shape[1] // MXU_WIDTH
        ffn_pieces = 2 * (w1_ref.shape[1] // FFN_CHUNK)
        mixer_pieces = [1] * (n_proj + n_blk) + [2] * n_blk + [0]
        assert len(mixer_pieces) == 2 * ffn_pieces
        _run(fa, 1)
        for count in mixer_pieces[:ffn_pieces]:
            _run(m, count)
            _run(fa, 1)
        _run(fa)
        _run(fb, 1)
        for count in mixer_pieces[ffn_pieces:]:
            _run(m, count)
            _run(fb, 1)
        _run(m)
        _run(fb)


def _resident(shape):
    return pl.BlockSpec(shape, lambda *_: (0,) * len(shape), pipeline_mode=pl.Buffered(1))


def _layer(x, pos, invf, sinks, w_in, vg, vb, wsp, bsp, w_out, g1, b1, w1, w2, g2, b2, *, alpha,
           seq):
    n_tok, d_model = x.shape
    tokens = TILE_TOKENS
    assert seq % tokens == 0 and tokens % BLK == 0 and w1.shape[1] % FFN_CHUNK == 0
    n_tiles = n_tok // tokens
    kernel = functools.partial(_layer_kernel, alpha=alpha, n_tiles=n_tiles,
                               tiles_per_seq=seq // tokens)
    in_tile = pl.BlockSpec((tokens, d_model), lambda j: (jnp.minimum(j, n_tiles - 1), 0))
    out_tile = pl.BlockSpec((tokens, d_model), lambda j: (jnp.maximum(j - 1, 0), 0))
    resident = [pos, invf, vg, vb, wsp, bsp, g1, b1, g2, b2]
    weights = [w_in, w_out, w1, w2]
    return pl.pallas_call(
        kernel,
        grid=(n_tiles + 1,),
        in_specs=[pl.BlockSpec(memory_space=pltpu.SMEM), in_tile]
        + [_resident(a.shape) for a in resident]
        + [pl.BlockSpec(memory_space=pl.ANY) for _ in weights],
        out_specs=out_tile,
        out_shape=jax.ShapeDtypeStruct(x.shape, x.dtype),
        scratch_shapes=[
            pltpu.VMEM((tokens, d_model), _F32),
            pltpu.VMEM((N_KV_HEADS, tokens + BLK, LANES), _BF16),
            pltpu.VMEM((D_KV, tokens + BLK), _BF16),
            pltpu.VMEM((tokens, D_GMLP + D_ATTN), _BF16),
        ] + [pltpu.VMEM(w.shape, _BF16) for w in weights] + [pltpu.SemaphoreType.DMA((STAGE_SLOTS,))],
        compiler_params=pltpu.CompilerParams(
            dimension_semantics=("arbitrary",), vmem_limit_bytes=VMEM_LIMIT_BYTES),
        name="layer",
    )(sinks, x, *resident, *weights)


def kernel(x, positions, w_in, v_ln_g, v_ln_b, w_spatial, b_spatial, sinks, w_out, ln1_g, ln1_b,
           w_ff1, w_ff2, ln2_g, ln2_b):
    batch, seq, d_model = x.shape
    depth = w_in.shape[0]
    alpha = (2.0 * depth) ** 0.25
    inv_freq = ROPE_THETA ** (-jnp.arange(0, HEAD_DIM, 2, dtype=_F32) / HEAD_DIM)
    invf = jnp.tile(inv_freq, LANES // HALF)[None, :]
    pos = positions
    x = x.reshape(batch * seq, d_model)
    row = lambda a: a[None, :]
    for l in range(depth):
        x = _layer(x, pos, invf, sinks[l], w_in[l], row(v_ln_g[l]), row(v_ln_b[l]),
                   w_spatial[l], b_spatial[l], w_out[l], row(ln1_g[l]), row(ln1_b[l]),
                   w_ff1[l], w_ff2[l], row(ln2_g[l]), row(ln2_b[l]), alpha=alpha, seq=seq)
    return x.reshape(batch, seq, d_model)
```

```python
import functools

import jax
import jax.numpy as jnp
from jax import lax
from jax.experimental import pallas as pl
from jax.experimental.pallas import tpu as pltpu

HEAD_DIM = 64
N_GMLP_HEADS = 8
D_GMLP = N_GMLP_HEADS * HEAD_DIM
N_Q_HEADS = 8
N_KV_HEADS = 2
GQA_GROUP = N_Q_HEADS // N_KV_HEADS
D_ATTN = N_Q_HEADS * HEAD_DIM
D_KV = N_KV_HEADS * HEAD_DIM
BLK = 128
ROPE_THETA = 10000.0
LN_EPS = 1e-5
NEG_INF = -1e30
LANES = 128
HALF = HEAD_DIM // 2
MXU_WIDTH = 256

TILE_TOKENS = 512
FFN_CHUNK = 1024
STAGE_ROWS, STAGE_COLS = 256, 1024
STAGE_SLOTS = 6
STAGE_POINT_EVERY = 3
VMEM_LIMIT_BYTES = 56 * 1024 * 1024

_BF16 = jnp.bfloat16
_F32 = jnp.float32


def _dot(a, b):
    return jnp.dot(a, b, preferred_element_type=_F32)


def _dot_nt(a, b):
    return lax.dot_general(a, b, (((1,), (1,)), ((), ())), preferred_element_type=_F32)


def _layer_norm(v, g, b):
    mu = jnp.mean(v, axis=-1, keepdims=True)
    c = v - mu
    var = jnp.mean(c * c, axis=-1, keepdims=True)
    return c * lax.rsqrt(var + LN_EPS) * g + b


def _mixer_steps(seq_start, sinks_ref, x_ref, pos_row, invf_ref, w_in_ref, vg_ref, vb_ref, wsp_ref,
                 bsp_ref, w_out_ref, y_buf, kd_buf, vt_buf, mix_buf, *, alpha):
    tokens = x_ref.shape[0]
    n_blk = tokens // BLK
    n_slab_g = D_GMLP // LANES
    n_slab_a = D_ATTN // LANES
    q0 = 2 * D_GMLP
    k0 = q0 + D_ATTN

    x = x_ref[...]
    xb = x.astype(_BF16)
    lane = lax.broadcasted_iota(jnp.int32, (1, LANES), 1)
    lo = lane < HEAD_DIM
    ti = lax.broadcasted_iota(jnp.int32, (BLK, BLK), 0)
    si = lax.broadcasted_iota(jnp.int32, (BLK, BLK), 1)

    def column(row):
        return jnp.sum(jnp.where(si == ti, row, 0.0), axis=1, keepdims=True)

    def project(col0, width):
        parts = []
        for c in range(col0, col0 + width, MXU_WIDTH):
            parts.append(_dot(xb, w_in_ref[:, c:c + MXU_WIDTH]))
            yield
        return parts

    qkv = yield from project(q0, D_ATTN + 2 * D_KV)
    slabs = [part[:, i * LANES:(i + 1) * LANES] for part in qkv for i in range(MXU_WIDTH // LANES)]
    q_raw, k_raw, v = slabs[:n_slab_a], slabs[n_slab_a], slabs[n_slab_a + 1]

    pos = pos_row.astype(_F32)
    invf = invf_ref[...]
    n_grp = LANES // HALF
    quarter = tokens // n_grp
    assert pos.shape == (1, tokens) and quarter == BLK
    grp = lane // HALF
    ang = None
    for a in range(n_grp):
        term = (column(pos[:, a * quarter:(a + 1) * quarter])
                * jnp.where(grp == a, invf, 0.0))
        ang = term if ang is None else ang + term
    cos_packed, sin_packed = jnp.cos(ang), jnp.sin(ang)

    def spread(packed):
        parts = []
        for a in range(n_grp):
            one = jnp.where(grp == a, packed, 0.0)
            two = one + pltpu.roll(one, 2 * HALF, axis=1)
            parts.append(two + pltpu.roll(two, HALF, axis=1))
        return jnp.concatenate(parts, axis=0)

    cos = spread(cos_packed)
    first_half = (lane & HALF) == 0
    sin_signed = jnp.where(first_half, -1.0, 1.0) * spread(sin_packed)

    def rope(t):
        swapped = jnp.where(first_half, pltpu.roll(t, LANES - HALF, axis=1),
                            pltpu.roll(t, HALF, axis=1))
        return t * cos + swapped * sin_signed

    scale = HEAD_DIM ** -0.5
    q_even, q_odd = [], []
    for p in range(n_slab_a):
        qs = rope(q_raw[p]) * scale
        q_even.append(jnp.where(lo, qs, 0.0).astype(_BF16))
        q_odd.append(jnp.where(lo, 0.0, qs).astype(_BF16))
    kr = rope(k_raw)
    kr_sw = pltpu.roll(kr, HEAD_DIM, axis=1)
    cur = slice(BLK, BLK + tokens)
    kd_buf[0, cur, :] = jnp.where(lo, kr, kr_sw).astype(_BF16)
    kd_buf[1, cur, :] = jnp.where(lo, kr_sw, kr).astype(_BF16)
    vt_buf[:, cur] = v.T.astype(_BF16)

    u = jax.nn.gelu(jnp.concatenate((yield from project(0, D_GMLP)), axis=1))
    vgel = jax.nn.gelu(jnp.concatenate((yield from project(D_GMLP, D_GMLP)), axis=1))
    vn = _layer_norm(vgel, vg_ref[...], vb_ref[...])
    lo_g = (lax.broadcasted_iota(jnp.int32, (1, D_GMLP), 1) & HEAD_DIM) == 0
    vn_top = jnp.where(lo_g, vn, 0.0).astype(_BF16)
    vn_bot = jnp.where(lo_g, 0.0, vn).astype(_BF16)

    from_prev = ti > si
    no_prev = from_prev & (ti < jnp.where(seq_start, BLK, 0))
    probs = {}
    for n in range(n_blk):
        rows = slice(n * BLK, (n + 1) * BLK)
        kv_rows = slice(n * BLK, (n + 2) * BLK)
        for g in range(N_KV_HEADS):
            heads = range(GQA_GROUP * g, GQA_GROUP * (g + 1))
            q_stack = jnp.concatenate(
                [(q_odd if h % 2 else q_even)[h // 2][rows] for h in heads], axis=0)
            scores = _dot_nt(kd_buf[g, kv_rows, :], q_stack)
            p_parts, inv_parts = [], []
            for hl, h in enumerate(heads):
                cols = slice(hl * BLK, (hl + 1) * BLK)
                sh = jnp.where(from_prev, scores[0:BLK, cols], scores[BLK:2 * BLK, cols])
                if n == 0:
                    sh = jnp.where(no_prev, NEG_INF, sh)
                sink = sinks_ref[h]
                m = jnp.maximum(jnp.max(sh, axis=0, keepdims=True), sink)
                pr = jnp.exp(sh - m)
                l = jnp.sum(pr, axis=0, keepdims=True) + jnp.exp(sink - m)
                p_parts.append(jnp.concatenate(
                    [jnp.where(from_prev, pr, 0.0), jnp.where(from_prev, 0.0, pr)], axis=0))
                inv_parts.append(1.0 / l)
            probs[n, g] = (jnp.concatenate(p_parts, axis=1).astype(_BF16),
                           jnp.concatenate(inv_parts, axis=1))
        yield

    causal = si <= ti

    def gmlp_piece(p):
        w_pair = jnp.concatenate(
            [jnp.where(causal, wsp_ref[2 * p], 0.0), jnp.where(causal, wsp_ref[2 * p + 1], 0.0)],
            axis=1).astype(_BF16)
        bias = jnp.where(lo, column(bsp_ref[2 * p:2 * p + 1, :]),
                         column(bsp_ref[2 * p + 1:2 * p + 2, :]))
        for c in range(n_blk):
            rows = slice(c * BLK, (c + 1) * BLK)
            cols = slice(p * LANES, (p + 1) * LANES)
            rhs = jnp.concatenate([vn_top[rows, cols], vn_bot[rows, cols]], axis=0)
            mixed = _dot(w_pair, rhs) + bias
            mix_buf[rows, cols] = (u[rows, cols] * mixed).astype(_BF16)

    def values_piece(n):
        rows = slice(n * BLK, (n + 1) * BLK)
        kv_cols = slice(n * BLK, (n + 2) * BLK)
        for g in range(N_KV_HEADS):
            p_all, inv_all = probs[n, g]
            v_t = vt_buf[g * HEAD_DIM:(g + 1) * HEAD_DIM, kv_cols]
            out_t = _dot(v_t, p_all) * inv_all
            for i in range(GQA_GROUP // 2):
                slab = g * (GQA_GROUP // 2) + i
                pair = jnp.concatenate([out_t[:, (2 * i) * BLK:(2 * i + 1) * BLK],
                                        out_t[:, (2 * i + 1) * BLK:(2 * i + 2) * BLK]], axis=0)
                mix_buf[rows, D_GMLP + slab * LANES:D_GMLP + (slab + 1) * LANES] = (
                    pair.T.astype(_BF16))

    for i in range(max(n_slab_g, n_blk)):
        if i < n_slab_g:
            gmlp_piece(i)
            yield
        if i < n_blk:
            values_piece(i)
            yield
    last = slice(tokens, tokens + BLK)
    kd_buf[:, 0:BLK, :] = kd_buf[:, last, :]
    vt_buf[:, 0:BLK] = vt_buf[:, last]

    mix = mix_buf[...]
    for c in range(0, x.shape[1], MXU_WIDTH):
        y_buf[:, c:c + MXU_WIDTH] = alpha * x[:, c:c + MXU_WIDTH] + _dot(mix, w_out_ref[:, c:c + MXU_WIDTH])
        yield


def _ffn_steps(rows, y_buf, g1_ref, b1_ref, w1_ref, w2_ref, g2_ref, b2_ref, o_ref, *, alpha):
    x = _layer_norm(y_buf[rows, :], g1_ref[...], b1_ref[...])
    xb = x.astype(_BF16)
    acc = alpha * x
    yield
    for j in range(w1_ref.shape[1] // FFN_CHUNK):
        cols = slice(j * FFN_CHUNK, (j + 1) * FFN_CHUNK)
        h = jnp.maximum(_dot(xb, w1_ref[:, cols]), 0.0)
        yield
        acc = acc + _dot((h * h).astype(_BF16), w2_ref[cols, :])
        yield
    o_ref[rows, :] = _layer_norm(acc, g2_ref[...], b2_ref[...])


def _run(steps, count=None):
    if count is None:
        for _ in steps:
            pass
    else:
        for _ in range(count):
            next(steps)


def _stage_jobs(pairs):
    jobs = []
    for src, dst in pairs:
        assert src.shape[0] % STAGE_ROWS == 0 and src.shape[1] % LANES == 0
        for r0 in range(0, src.shape[0], STAGE_ROWS):
            for c0 in range(0, src.shape[1], STAGE_COLS):
                jobs.append((src, dst, r0, c0, min(STAGE_COLS, src.shape[1] - c0)))
    return jobs


def _stage_steps(jobs, stage, sem):
    copies = [pltpu.make_async_copy(src.at[pl.ds(r0, STAGE_ROWS), pl.ds(c0, width)],
                                    stage.at[i % STAGE_SLOTS, :, pl.ds(0, width)],
                                    sem.at[i % STAGE_SLOTS])
              for i, (src, _, r0, c0, width) in enumerate(jobs)]
    ahead = STAGE_SLOTS - 1
    for copy in copies[:ahead]:
        copy.start()
    for i, (_, dst, r0, c0, width) in enumerate(jobs):
        if i + ahead < len(jobs):
            copies[i + ahead].start()
        copies[i].wait()
        dst[r0:r0 + STAGE_ROWS, c0:c0 + width] = stage[i % STAGE_SLOTS, :, 0:width].astype(_BF16)
        yield


def _layer_kernel(sinks_ref, x_ref, pos_ref, invf_ref, vg_ref, vb_ref, wsp_ref, bsp_ref, g1_ref,
                  b1_ref, g2_ref, b2_ref, w_in_hbm, w_out_hbm, w1_hbm, w2_hbm, o_ref,
                  y_buf, kd_buf, vt_buf, mix_buf, w_in_ref, w_out_ref, w1_ref, w2_ref, stage_sem,
                  *, alpha, n_tiles, tiles_per_seq):
    j = pl.program_id(0)
    seq_start = (j % tiles_per_seq) == 0
    tile = jnp.minimum(j, n_tiles - 1)
    tokens = x_ref.shape[0]
    pos_row = pos_ref[pl.ds(tile // tiles_per_seq, 1),
                      pl.ds(pl.multiple_of((tile % tiles_per_seq) * tokens, tokens), tokens)]

    @pl.when(seq_start)
    def _():
        kd_buf[:, 0:BLK, :] = jnp.zeros((N_KV_HEADS, BLK, LANES), _BF16)
        vt_buf[:, 0:BLK] = jnp.zeros((D_KV, BLK), _BF16)

    mixer = functools.partial(
        _mixer_steps, seq_start, sinks_ref, x_ref, pos_row, invf_ref, w_in_ref, vg_ref, vb_ref,
        wsp_ref, bsp_ref, w_out_ref, y_buf, kd_buf, vt_buf, mix_buf, alpha=alpha)
    half = x_ref.shape[0] // 2
    ffn = [functools.partial(_ffn_steps, slice(i * half, (i + 1) * half), y_buf, g1_ref, b1_ref,
                             w1_ref, w2_ref, g2_ref, b2_ref, o_ref, alpha=alpha) for i in range(2)]

    @pl.when(j == 0)
    def _():
        def first_step(stage):
            _run(_stage_steps(_stage_jobs([(w_in_hbm, w_in_ref)]), stage, stage_sem))
            later = _stage_steps(
                _stage_jobs([(w_out_hbm, w_out_ref), (w1_hbm, w1_ref), (w2_hbm, w2_ref)]),
                stage, stage_sem)
            assert len(_stage_jobs([(w_out_hbm, w_out_ref)])) <= STAGE_SLOTS - 1
            for i, _ in enumerate(mixer()):
                if i % STAGE_POINT_EVERY == STAGE_POINT_EVERY - 1:
                    for _ in range(STAGE_SLOTS - 1):
                        next(later, None)
            _run(later)

        pl.run_scoped(first_step, pltpu.VMEM((STAGE_SLOTS, STAGE_ROWS, STAGE_COLS), _F32))

    @pl.when(j > 0)
    def _():
        fa, fb, m = ffn[0](), ffn[1](), mixer()
        n_blk = x_ref.shape[0] // BLK
        n_proj = w_in_ref.shape[1] // MXU_WIDTH
        ffn_pieces = 2 * (w1_ref.shape[1] // FFN_CHUNK)
        mixer_pieces = [1] * n_proj + [2] * (n_blk // 2) + [2] + [1] * (2 * n_blk - 2)
        assert len(mixer_pieces) == 2 * ffn_pieces
        _run(fa, 1)
        for count in mixer_pieces[:ffn_pieces]:
            _run(m, count)
            _run(fa, 1)
        _run(fa)
        _run(fb, 1)
        for count in mixer_pieces[ffn_pieces:]:
            _run(m, count)
            _run(fb, 1)
        _run(m)
        _run(fb)


def _resident(shape):
    return pl.BlockSpec(shape, lambda *_: (0,) * len(shape), pipeline_mode=pl.Buffered(1))


def _layer(x, pos, invf, sinks, w_in, vg, vb, wsp, bsp, w_out, g1, b1, w1, w2, g2, b2, *, alpha,
           seq):
    n_tok, d_model = x.shape
    tokens = TILE_TOKENS
    assert seq % tokens == 0 and tokens % BLK == 0 and w1.shape[1] % FFN_CHUNK == 0
    n_tiles = n_tok // tokens
    kernel = functools.partial(_layer_kernel, alpha=alpha, n_tiles=n_tiles,
                               tiles_per_seq=seq // tokens)
    in_tile = pl.BlockSpec((tokens, d_model), lambda j: (jnp.minimum(j, n_tiles - 1), 0))
    out_tile = pl.BlockSpec((tokens, d_model), lambda j: (jnp.maximum(j - 1, 0), 0))
    resident = [pos, invf, vg, vb, wsp, bsp, g1, b1, g2, b2]
    weights = [w_in, w_out, w1, w2]
    return pl.pallas_call(
        kernel,
        grid=(n_tiles + 1,),
        in_specs=[pl.BlockSpec(memory_space=pltpu.SMEM), in_tile]
        + [_resident(a.shape) for a in resident]
        + [pl.BlockSpec(memory_space=pl.ANY) for _ in weights],
        out_specs=out_tile,
        out_shape=jax.ShapeDtypeStruct(x.shape, x.dtype),
        scratch_shapes=[
            pltpu.VMEM((tokens, d_model), _F32),
            pltpu.VMEM((N_KV_HEADS, tokens + BLK, LANES), _BF16),
            pltpu.VMEM((D_KV, tokens + BLK), _BF16),
            pltpu.VMEM((tokens, D_GMLP + D_ATTN), _BF16),
        ] + [pltpu.VMEM(w.shape, _BF16) for w in weights] + [pltpu.SemaphoreType.DMA((STAGE_SLOTS,))],
        compiler_params=pltpu.CompilerParams(
            dimension_semantics=("arbitrary",), vmem_limit_bytes=VMEM_LIMIT_BYTES),
        name="layer",
    )(sinks, x, *resident, *weights)


def kernel(x, positions, w_in, v_ln_g, v_ln_b, w_spatial, b_spatial, sinks, w_out, ln1_g, ln1_b,
           w_ff1, w_ff2, ln2_g, ln2_b):
    batch, seq, d_model = x.shape
    depth = w_in.shape[0]
    alpha = (2.0 * depth) ** 0.25
    inv_freq = ROPE_THETA ** (-jnp.arange(0, HEAD_DIM, 2, dtype=_F32) / HEAD_DIM)
    invf = jnp.tile(inv_freq, LANES // HALF)[None, :]
    pos = positions
    x = x.reshape(batch * seq, d_model)
    row = lambda a: a[None, :]
    for l in range(depth):
        x = _layer(x, pos, invf, sinks[l], w_in[l], row(v_ln_g[l]), row(v_ln_b[l]),
                   w_spatial[l], b_spatial[l], w_out[l], row(ln1_g[l]), row(ln1_b[l]),
                   w_ff1[l], w_ff2[l], row(ln2_g[l]), row(ln2_b[l]), alpha=alpha, seq=seq)
    return x.reshape(batch, seq, d_model)
```

```python
import functools

import jax
import jax.numpy as jnp
from jax import lax
from jax.experimental import pallas as pl
from jax.experimental.pallas import tpu as pltpu

HEAD_DIM = 64
N_GMLP_HEADS = 8
D_GMLP = N_GMLP_HEADS * HEAD_DIM
N_Q_HEADS = 8
N_KV_HEADS = 2
GQA_GROUP = N_Q_HEADS // N_KV_HEADS
D_ATTN = N_Q_HEADS * HEAD_DIM
D_KV = N_KV_HEADS * HEAD_DIM
BLK = 128
ROPE_THETA = 10000.0
LN_EPS = 1e-5
NEG_INF = -1e30
LANES = 128
HALF = HEAD_DIM // 2
MXU_WIDTH = 256

TILE_TOKENS = 512
FFN_CHUNK = 1024
STAGE_ROWS, STAGE_COLS = 256, 1024
STAGE_SLOTS = 6
STAGE_POINT_EVERY = 3
VMEM_LIMIT_BYTES = 56 * 1024 * 1024

_BF16 = jnp.bfloat16
_F32 = jnp.float32


def _dot(a, b):
    return jnp.dot(a, b, preferred_element_type=_F32)


def _dot_nt(a, b):
    return lax.dot_general(a, b, (((1,), (1,)), ((), ())), preferred_element_type=_F32)


def _layer_norm(v, g, b):
    mu = jnp.mean(v, axis=-1, keepdims=True)
    c = v - mu
    var = jnp.mean(c * c, axis=-1, keepdims=True)
    return c * lax.rsqrt(var + LN_EPS) * g + b


def _mixer_steps(seq_start, sinks_ref, x_ref, pos_row, invf_ref, w_in_ref, vg_ref, vb_ref, wsp_ref,
                 bsp_ref, w_out_ref, y_buf, kd_buf, vt_buf, mix_buf, *, alpha):
    tokens = x_ref.shape[0]
    n_blk = tokens // BLK
    n_slab_g = D_GMLP // LANES
    n_slab_a = D_ATTN // LANES
    q0 = 2 * D_GMLP
    k0 = q0 + D_ATTN

    x = x_ref[...]
    xb = x.astype(_BF16)
    lane = lax.broadcasted_iota(jnp.int32, (1, LANES), 1)
    lo = lane < HEAD_DIM
    ti = lax.broadcasted_iota(jnp.int32, (BLK, BLK), 0)
    si = lax.broadcasted_iota(jnp.int32, (BLK, BLK), 1)

    def column(row):
        return jnp.sum(jnp.where(si == ti, row, 0.0), axis=1, keepdims=True)

    def project(col0, width):
        parts = []
        for c in range(col0, col0 + width, MXU_WIDTH):
            parts.append(_dot(xb, w_in_ref[:, c:c + MXU_WIDTH]))
            yield
        return parts

    qkv = yield from project(q0, D_ATTN + 2 * D_KV)
    slabs = [part[:, i * LANES:(i + 1) * LANES] for part in qkv for i in range(MXU_WIDTH // LANES)]
    q_raw, k_raw, v = slabs[:n_slab_a], slabs[n_slab_a], slabs[n_slab_a + 1]

    pos = pos_row.astype(_F32)
    invf = invf_ref[...]
    n_grp = LANES // HALF
    quarter = tokens // n_grp
    assert pos.shape == (1, tokens) and quarter == BLK
    grp = lane // HALF
    ang = None
    for a in range(n_grp):
        term = (column(pos[:, a * quarter:(a + 1) * quarter])
                * jnp.where(grp == a, invf, 0.0))
        ang = term if ang is None else ang + term
    cos_packed, sin_packed = jnp.cos(ang), jnp.sin(ang)

    def spread(packed):
        parts = []
        for a in range(n_grp):
            one = jnp.where(grp == a, packed, 0.0)
            two = one + pltpu.roll(one, 2 * HALF, axis=1)
            parts.append(two + pltpu.roll(two, HALF, axis=1))
        return jnp.concatenate(parts, axis=0)

    cos = spread(cos_packed)
    first_half = (lane & HALF) == 0
    sin_signed = jnp.where(first_half, -1.0, 1.0) * spread(sin_packed)

    def rope(t):
        swapped = jnp.where(first_half, pltpu.roll(t, LANES - HALF, axis=1),
                            pltpu.roll(t, HALF, axis=1))
        return t * cos + swapped * sin_signed

    scale = HEAD_DIM ** -0.5
    q_even, q_odd = [], []
    for p in range(n_slab_a):
        qs = rope(q_raw[p]) * scale
        q_even.append(jnp.where(lo, qs, 0.0).astype(_BF16))
        q_odd.append(jnp.where(lo, 0.0, qs).astype(_BF16))
    kr = rope(k_raw)
    kr_sw = pltpu.roll(kr, HEAD_DIM, axis=1)
    cur = slice(BLK, BLK + tokens)
    kd_buf[0, cur, :] = jnp.where(lo, kr, kr_sw).astype(_BF16)
    kd_buf[1, cur, :] = jnp.where(lo, kr_sw, kr).astype(_BF16)
    vt_buf[:, cur] = v.T.astype(_BF16)

    u = jax.nn.gelu(jnp.concatenate((yield from project(0, D_GMLP)), axis=1))
    vgel = jax.nn.gelu(jnp.concatenate((yield from project(D_GMLP, D_GMLP)), axis=1))
    vn = _layer_norm(vgel, vg_ref[...], vb_ref[...])
    lo_g = (lax.broadcasted_iota(jnp.int32, (1, D_GMLP), 1) & HEAD_DIM) == 0
    vn_top = jnp.where(lo_g, vn, 0.0).astype(_BF16)
    vn_bot = jnp.where(lo_g, 0.0, vn).astype(_BF16)

    from_prev = ti > si
    no_prev = from_prev & (ti < jnp.where(seq_start, BLK, 0))
    probs = {}
    for n in range(n_blk):
        rows = slice(n * BLK, (n + 1) * BLK)
        kv_rows = slice(n * BLK, (n + 2) * BLK)
        for g in range(N_KV_HEADS):
            heads = range(GQA_GROUP * g, GQA_GROUP * (g + 1))
            q_stack = jnp.concatenate(
                [(q_odd if h % 2 else q_even)[h // 2][rows] for h in heads], axis=0)
            scores = _dot_nt(kd_buf[g, kv_rows, :], q_stack)
            p_parts, inv_parts = [], []
            for hl, h in enumerate(heads):
                cols = slice(hl * BLK, (hl + 1) * BLK)
                sh = jnp.where(from_prev, scores[0:BLK, cols], scores[BLK:2 * BLK, cols])
                if n == 0:
                    sh = jnp.where(no_prev, NEG_INF, sh)
                sink = sinks_ref[h]
                m = jnp.maximum(jnp.max(sh, axis=0, keepdims=True), sink)
                pr = jnp.exp(sh - m)
                l = jnp.sum(pr, axis=0, keepdims=True) + jnp.exp(sink - m)
                p_parts.append(jnp.concatenate(
                    [jnp.where(from_prev, pr, 0.0), jnp.where(from_prev, 0.0, pr)], axis=0))
                inv_parts.append(1.0 / l)
            probs[n, g] = (jnp.concatenate(p_parts, axis=1).astype(_BF16),
                           jnp.concatenate(inv_parts, axis=1))
        yield

    causal = si <= ti

    def gmlp_piece(p):
        w_pair = jnp.concatenate(
            [jnp.where(causal, wsp_ref[2 * p], 0.0), jnp.where(causal, wsp_ref[2 * p + 1], 0.0)],
            axis=1).astype(_BF16)
        bias = jnp.where(lo, column(bsp_ref[2 * p:2 * p + 1, :]),
                         column(bsp_ref[2 * p + 1:2 * p + 2, :]))
        for c in range(n_blk):
            rows = slice(c * BLK, (c + 1) * BLK)
            cols = slice(p * LANES, (p + 1) * LANES)
            rhs = jnp.concatenate([vn_top[rows, cols], vn_bot[rows, cols]], axis=0)
            mixed = _dot(w_pair, rhs) + bias
            mix_buf[rows, cols] = (u[rows, cols] * mixed).astype(_BF16)

    def values_piece(n):
        rows = slice(n * BLK, (n + 1) * BLK)
        kv_cols = slice(n * BLK, (n + 2) * BLK)
        for g in range(N_KV_HEADS):
            p_all, inv_all = probs[n, g]
            v_t = vt_buf[g * HEAD_DIM:(g + 1) * HEAD_DIM, kv_cols]
            out_t = _dot(v_t, p_all) * inv_all
            for i in range(GQA_GROUP // 2):
                slab = g * (GQA_GROUP // 2) + i
                pair = jnp.concatenate([out_t[:, (2 * i) * BLK:(2 * i + 1) * BLK],
                                        out_t[:, (2 * i + 1) * BLK:(2 * i + 2) * BLK]], axis=0)
                mix_buf[rows, D_GMLP + slab * LANES:D_GMLP + (slab + 1) * LANES] = (
                    pair.T.astype(_BF16))

    for i in range(max(n_slab_g, n_blk)):
        if i < n_slab_g:
            gmlp_piece(i)
            yield
        if i < n_blk:
            values_piece(i)
            yield
    last = slice(tokens, tokens + BLK)
    kd_buf[:, 0:BLK, :] = kd_buf[:, last, :]
    vt_buf[:, 0:BLK] = vt_buf[:, last]

    mix = mix_buf[...]
    for c in range(0, x.shape[1], MXU_WIDTH):
        y_buf[:, c:c + MXU_WIDTH] = alpha * x[:, c:c + MXU_WIDTH] + _dot(mix, w_out_ref[:, c:c + MXU_WIDTH])
        yield


def _ffn_steps(rows, y_buf, g1_ref, b1_ref, w1_ref, w2_ref, g2_ref, b2_ref, o_ref, *, alpha):
    x = _layer_norm(y_buf[rows, :], g1_ref[...], b1_ref[...])
    xb = x.astype(_BF16)
    acc = alpha * x
    yield
    for j in range(w1_ref.shape[1] // FFN_CHUNK):
        cols = slice(j * FFN_CHUNK, (j + 1) * FFN_CHUNK)
        h = jnp.maximum(_dot(xb, w1_ref[:, cols]), 0.0)
        yield
        acc = acc + _dot((h * h).astype(_BF16), w2_ref[cols, :])
        yield
    o_ref[rows, :] = _layer_norm(acc, g2_ref[...], b2_ref[...])


def _run(steps, count=None):
    if count is None:
        for _ in steps:
            pass
    else:
        for _ in range(count):
            next(steps)


def _stage_jobs(pairs):
    jobs = []
    for src, dst in pairs:
        assert src.shape[0] % STAGE_ROWS == 0 and src.shape[1] % LANES == 0
        for r0 in range(0, src.shape[0], STAGE_ROWS):
            for c0 in range(0, src.shape[1], STAGE_COLS):
                jobs.append((src, dst, r0, c0, min(STAGE_COLS, src.shape[1] - c0)))
    return jobs


def _stage_steps(jobs, stage, sem):
    copies = [pltpu.make_async_copy(src.at[pl.ds(r0, STAGE_ROWS), pl.ds(c0, width)],
                                    stage.at[i % STAGE_SLOTS, :, pl.ds(0, width)],
                                    sem.at[i % STAGE_SLOTS])
              for i, (src, _, r0, c0, width) in enumerate(jobs)]
    ahead = STAGE_SLOTS - 1
    for copy in copies[:ahead]:
        copy.start()
    for i, (_, dst, r0, c0, width) in enumerate(jobs):
        if i + ahead < len(jobs):
            copies[i + ahead].start()
        copies[i].wait()
        dst[r0:r0 + STAGE_ROWS, c0:c0 + width] = stage[i % STAGE_SLOTS, :, 0:width].astype(_BF16)
        yield


def _layer_kernel(sinks_ref, x_ref, pos_ref, invf_ref, vg_ref, vb_ref, wsp_ref, bsp_ref, g1_ref,
                  b1_ref, g2_ref, b2_ref, w_in_hbm, w_out_hbm, w1_hbm, w2_hbm, o_ref,
                  y_buf, kd_buf, vt_buf, mix_buf, w_in_ref, w_out_ref, w1_ref, w2_ref, stage_sem,
                  *, alpha, n_tiles, tiles_per_seq):
    j = pl.program_id(0)
    seq_start = (j % tiles_per_seq) == 0
    tile = jnp.minimum(j, n_tiles - 1)
    tokens = x_ref.shape[0]
    pos_row = pos_ref[pl.ds(tile // tiles_per_seq, 1),
                      pl.ds(pl.multiple_of((tile % tiles_per_seq) * tokens, tokens), tokens)]

    @pl.when(seq_start)
    def _():
        kd_buf[:, 0:BLK, :] = jnp.zeros((N_KV_HEADS, BLK, LANES), _BF16)
        vt_buf[:, 0:BLK] = jnp.zeros((D_KV, BLK), _BF16)

    mixer = functools.partial(
        _mixer_steps, seq_start, sinks_ref, x_ref, pos_row, invf_ref, w_in_ref, vg_ref, vb_ref,
        wsp_ref, bsp_ref, w_out_ref, y_buf, kd_buf, vt_buf, mix_buf, alpha=alpha)
    half = x_ref.shape[0] // 2
    ffn = [functools.partial(_ffn_steps, slice(i * half, (i + 1) * half), y_buf, g1_ref, b1_ref,
                             w1_ref, w2_ref, g2_ref, b2_ref, o_ref, alpha=alpha) for i in range(2)]

    @pl.when(j == 0)
    def _():
        def first_step(stage):
            _run(_stage_steps(_stage_jobs([(w_in_hbm, w_in_ref)]), stage, stage_sem))
            later = _stage_steps(
                _stage_jobs([(w_out_hbm, w_out_ref), (w1_hbm, w1_ref), (w2_hbm, w2_ref)]),
                stage, stage_sem)
            assert len(_stage_jobs([(w_out_hbm, w_out_ref)])) <= STAGE_SLOTS - 1
            for i, _ in enumerate(mixer()):
                if i % STAGE_POINT_EVERY == STAGE_POINT_EVERY - 1:
                    for _ in range(STAGE_SLOTS - 1):
                        next(later, None)
            _run(later)

        pl.run_scoped(first_step, pltpu.VMEM((STAGE_SLOTS, STAGE_ROWS, STAGE_COLS), _F32))

    @pl.when(j > 0)
    def _():
        fa, fb, m = ffn[0](), ffn[1](), mixer()
        n_blk = x_ref.shape[0] // BLK
        n_proj = w_in_ref.shape[1] // MXU_WIDTH
        ffn_pieces = 2 * (w1_ref.shape[1] // FFN_CHUNK)
        mixer_pieces = [1] * (n_proj + n_blk - 1) + [2] * n_blk + [1, 0]
        assert len(mixer_pieces) == 2 * ffn_pieces
        _run(fa, 1)
        for count in mixer_pieces[:ffn_pieces]:
            _run(m, count)
            _run(fa, 1)
        _run(fa)
        _run(fb, 1)
        for count in mixer_pieces[ffn_pieces:]:
            _run(m, count)
            _run(fb, 1)
        _run(m)
        _run(fb)


def _resident(shape):
    return pl.BlockSpec(shape, lambda *_: (0,) * len(shape), pipeline_mode=pl.Buffered(1))


def _layer(x, pos, invf, sinks, w_in, vg, vb, wsp, bsp, w_out, g1, b1, w1, w2, g2, b2, *, alpha,
           seq):
    n_tok, d_model = x.shape
    tokens = TILE_TOKENS
    assert seq % tokens == 0 and tokens % BLK == 0 and w1.shape[1] % FFN_CHUNK == 0
    n_tiles = n_tok // tokens
    kernel = functools.partial(_layer_kernel, alpha=alpha, n_tiles=n_tiles,
                               tiles_per_seq=seq // tokens)
    in_tile = pl.BlockSpec((tokens, d_model), lambda j: (jnp.minimum(j, n_tiles - 1), 0))
    out_tile = pl.BlockSpec((tokens, d_model), lambda j: (jnp.maximum(j - 1, 0), 0))
    resident = [pos, invf, vg, vb, wsp, bsp, g1, b1, g2, b2]
    weights = [w_in, w_out, w1, w2]
    return pl.pallas_call(
        kernel,
        grid=(n_tiles + 1,),
        in_specs=[pl.BlockSpec(memory_space=pltpu.SMEM), in_tile]
        + [_resident(a.shape) for a in resident]
        + [pl.BlockSpec(memory_space=pl.ANY) for _ in weights],
        out_specs=out_tile,
        out_shape=jax.ShapeDtypeStruct(x.shape, x.dtype),
        scratch_shapes=[
            pltpu.VMEM((tokens, d_model), _F32),
            pltpu.VMEM((N_KV_HEADS, tokens + BLK, LANES), _BF16),
            pltpu.VMEM((D_KV, tokens + BLK), _BF16),
            pltpu.VMEM((tokens, D_GMLP + D_ATTN), _BF16),
        ] + [pltpu.VMEM(w.shape, _BF16) for w in weights] + [pltpu.SemaphoreType.DMA((STAGE_SLOTS,))],
        compiler_params=pltpu.CompilerParams(
            dimension_semantics=("arbitrary",), vmem_limit_bytes=VMEM_LIMIT_BYTES),
        name="layer",
    )(sinks, x, *resident, *weights)


def kernel(x, positions, w_in, v_ln_g, v_ln_b, w_spatial, b_spatial, sinks, w_out, ln1_g, ln1_b,
           w_ff1, w_ff2, ln2_g, ln2_b):
    batch, seq, d_model = x.shape
    depth = w_in.shape[0]
    alpha = (2.0 * depth) ** 0.25
    inv_freq = ROPE_THETA ** (-jnp.arange(0, HEAD_DIM, 2, dtype=_F32) / HEAD_DIM)
    invf = jnp.tile(inv_freq, LANES // HALF)[None, :]
    pos = positions
    x = x.reshape(batch * seq, d_model)
    row = lambda a: a[None, :]
    for l in range(depth):
        x = _layer(x, pos, invf, sinks[l], w_in[l], row(v_ln_g[l]), row(v_ln_b[l]),
                   w_spatial[l], b_spatial[l], w_out[l], row(ln1_g[l]), row(ln1_b[l]),
                   w_ff1[l], w_ff2[l], row(ln2_g[l]), row(ln2_b[l]), alpha=alpha, seq=seq)
    return x.reshape(batch, seq, d_model)
```

```python
import functools

import jax
import jax.numpy as jnp
from jax import lax
from jax.experimental import pallas as pl
from jax.experimental.pallas import tpu as pltpu

HEAD_DIM = 64
N_GMLP_HEADS = 8
D_GMLP = N_GMLP_HEADS * HEAD_DIM
N_Q_HEADS = 8
N_KV_HEADS = 2
GQA_GROUP = N_Q_HEADS // N_KV_HEADS
D_ATTN = N_Q_HEADS * HEAD_DIM
D_KV = N_KV_HEADS * HEAD_DIM
BLK = 128
ROPE_THETA = 10000.0
LN_EPS = 1e-5
NEG_INF = -1e30
LANES = 128
HALF = HEAD_DIM // 2
MXU_WIDTH = 256

TILE_TOKENS = 512
FFN_CHUNK = 1024
STAGE_ROWS, STAGE_COLS = 256, 1024
STAGE_SLOTS = 6
STAGE_POINT_EVERY = 3
VMEM_LIMIT_BYTES = 56 * 1024 * 1024

_BF16 = jnp.bfloat16
_F32 = jnp.float32


def _dot(a, b):
    return jnp.dot(a, b, preferred_element_type=_F32)


def _dot_nt(a, b):
    return lax.dot_general(a, b, (((1,), (1,)), ((), ())), preferred_element_type=_F32)


def _layer_norm(v, g, b):
    mu = jnp.mean(v, axis=-1, keepdims=True)
    c = v - mu
    var = jnp.mean(c * c, axis=-1, keepdims=True)
    return c * lax.rsqrt(var + LN_EPS) * g + b


def _mixer_steps(seq_start, sinks_ref, x_ref, pos_row, invf_ref, w_in_ref, vg_ref, vb_ref, wsp_ref,
                 bsp_ref, w_out_ref, y_buf, kd_buf, vt_buf, mix_buf, *, alpha):
    tokens = x_ref.shape[0]
    n_blk = tokens // BLK
    n_slab_g = D_GMLP // LANES
    n_slab_a = D_ATTN // LANES
    q0 = 2 * D_GMLP
    k0 = q0 + D_ATTN

    x = x_ref[...]
    xb = x.astype(_BF16)
    lane = lax.broadcasted_iota(jnp.int32, (1, LANES), 1)
    lo = lane < HEAD_DIM
    ti = lax.broadcasted_iota(jnp.int32, (BLK, BLK), 0)
    si = lax.broadcasted_iota(jnp.int32, (BLK, BLK), 1)

    def column(row):
        return jnp.sum(jnp.where(si == ti, row, 0.0), axis=1, keepdims=True)

    def project(col0, width):
        parts = []
        for c in range(col0, col0 + width, MXU_WIDTH):
            parts.append(_dot(xb, w_in_ref[:, c:c + MXU_WIDTH]))
            yield
        return parts

    qkv = yield from project(q0, D_ATTN + 2 * D_KV)
    slabs = [part[:, i * LANES:(i + 1) * LANES] for part in qkv for i in range(MXU_WIDTH // LANES)]
    q_raw, k_raw, v = slabs[:n_slab_a], slabs[n_slab_a], slabs[n_slab_a + 1]

    pos = pos_row.astype(_F32)
    invf = invf_ref[...]
    n_grp = LANES // HALF
    quarter = tokens // n_grp
    assert pos.shape == (1, tokens) and quarter == BLK
    grp = lane // HALF
    ang = None
    for a in range(n_grp):
        term = (column(pos[:, a * quarter:(a + 1) * quarter])
                * jnp.where(grp == a, invf, 0.0))
        ang = term if ang is None else ang + term
    cos_packed, sin_packed = jnp.cos(ang), jnp.sin(ang)

    def spread(packed):
        parts = []
        for a in range(n_grp):
            one = jnp.where(grp == a, packed, 0.0)
            two = one + pltpu.roll(one, 2 * HALF, axis=1)
            parts.append(two + pltpu.roll(two, HALF, axis=1))
        return jnp.concatenate(parts, axis=0)

    cos = spread(cos_packed)
    first_half = (lane & HALF) == 0
    sin_signed = jnp.where(first_half, -1.0, 1.0) * spread(sin_packed)

    def rope(t):
        swapped = jnp.where(first_half, pltpu.roll(t, LANES - HALF, axis=1),
                            pltpu.roll(t, HALF, axis=1))
        return t * cos + swapped * sin_signed

    scale = HEAD_DIM ** -0.5
    q_even, q_odd = [], []
    for p in range(n_slab_a):
        qs = rope(q_raw[p]) * scale
        q_even.append(jnp.where(lo, qs, 0.0).astype(_BF16))
        q_odd.append(jnp.where(lo, 0.0, qs).astype(_BF16))
    kr = rope(k_raw)
    kr_sw = pltpu.roll(kr, HEAD_DIM, axis=1)
    cur = slice(BLK, BLK + tokens)
    kd_buf[0, cur, :] = jnp.where(lo, kr, kr_sw).astype(_BF16)
    kd_buf[1, cur, :] = jnp.where(lo, kr_sw, kr).astype(_BF16)
    vt_buf[:, cur] = v.T.astype(_BF16)

    u = jax.nn.gelu(jnp.concatenate((yield from project(0, D_GMLP)), axis=1))
    vgel = jax.nn.gelu(jnp.concatenate((yield from project(D_GMLP, D_GMLP)), axis=1))
    vn = _layer_norm(vgel, vg_ref[...], vb_ref[...])
    lo_g = (lax.broadcasted_iota(jnp.int32, (1, D_GMLP), 1) & HEAD_DIM) == 0
    vn_top = jnp.where(lo_g, vn, 0.0).astype(_BF16)
    vn_bot = jnp.where(lo_g, 0.0, vn).astype(_BF16)

    from_prev = ti > si
    no_prev = from_prev & (ti < jnp.where(seq_start, BLK, 0))
    probs = {}
    for n in range(n_blk):
        rows = slice(n * BLK, (n + 1) * BLK)
        kv_rows = slice(n * BLK, (n + 2) * BLK)
        for g in range(N_KV_HEADS):
            heads = range(GQA_GROUP * g, GQA_GROUP * (g + 1))
            q_stack = jnp.concatenate(
                [(q_odd if h % 2 else q_even)[h // 2][rows] for h in heads], axis=0)
            scores = _dot_nt(kd_buf[g, kv_rows, :], q_stack)
            p_parts, inv_parts = [], []
            for hl, h in enumerate(heads):
                cols = slice(hl * BLK, (hl + 1) * BLK)
                sh = jnp.where(from_prev, scores[0:BLK, cols], scores[BLK:2 * BLK, cols])
                if n == 0:
                    sh = jnp.where(no_prev, NEG_INF, sh)
                sink = sinks_ref[h]
                m = jnp.maximum(jnp.max(sh, axis=0, keepdims=True), sink)
                pr = jnp.exp(sh - m)
                l = jnp.sum(pr, axis=0, keepdims=True) + jnp.exp(sink - m)
                p_parts.append(jnp.concatenate(
                    [jnp.where(from_prev, pr, 0.0), jnp.where(from_prev, 0.0, pr)], axis=0))
                inv_parts.append(1.0 / l)
            probs[n, g] = (jnp.concatenate(p_parts, axis=1).astype(_BF16),
                           jnp.concatenate(inv_parts, axis=1))
        yield

    causal = si <= ti

    def gmlp_piece(p):
        w_pair = jnp.concatenate(
            [jnp.where(causal, wsp_ref[2 * p], 0.0), jnp.where(causal, wsp_ref[2 * p + 1], 0.0)],
            axis=1).astype(_BF16)
        bias = jnp.where(lo, column(bsp_ref[2 * p:2 * p + 1, :]),
                         column(bsp_ref[2 * p + 1:2 * p + 2, :]))
        for c in range(n_blk):
            rows = slice(c * BLK, (c + 1) * BLK)
            cols = slice(p * LANES, (p + 1) * LANES)
            rhs = jnp.concatenate([vn_top[rows, cols], vn_bot[rows, cols]], axis=0)
            mixed = _dot(w_pair, rhs) + bias
            mix_buf[rows, cols] = (u[rows, cols] * mixed).astype(_BF16)

    def values_piece(n):
        rows = slice(n * BLK, (n + 1) * BLK)
        kv_cols = slice(n * BLK, (n + 2) * BLK)
        for g in range(N_KV_HEADS):
            p_all, inv_all = probs[n, g]
            v_t = vt_buf[g * HEAD_DIM:(g + 1) * HEAD_DIM, kv_cols]
            out_t = _dot(v_t, p_all) * inv_all
            for i in range(GQA_GROUP // 2):
                slab = g * (GQA_GROUP // 2) + i
                pair = jnp.concatenate([out_t[:, (2 * i) * BLK:(2 * i + 1) * BLK],
                                        out_t[:, (2 * i + 1) * BLK:(2 * i + 2) * BLK]], axis=0)
                mix_buf[rows, D_GMLP + slab * LANES:D_GMLP + (slab + 1) * LANES] = (
                    pair.T.astype(_BF16))

    for i in range(max(n_slab_g, n_blk)):
        if i < n_slab_g:
            gmlp_piece(i)
            yield
        if i < n_blk:
            values_piece(i)
            yield
    last = slice(tokens, tokens + BLK)
    kd_buf[:, 0:BLK, :] = kd_buf[:, last, :]
    vt_buf[:, 0:BLK] = vt_buf[:, last]

    mix = mix_buf[...]
    for c in range(0, x.shape[1], MXU_WIDTH):
        y_buf[:, c:c + MXU_WIDTH] = alpha * x[:, c:c + MXU_WIDTH] + _dot(mix, w_out_ref[:, c:c + MXU_WIDTH])
        yield


def _ffn_steps(rows, y_buf, g1_ref, b1_ref, w1_ref, w2_ref, g2_ref, b2_ref, o_ref, *, alpha):
    x = _layer_norm(y_buf[rows, :], g1_ref[...], b1_ref[...])
    xb = x.astype(_BF16)
    acc = alpha * x
    yield
    for j in range(w1_ref.shape[1] // FFN_CHUNK):
        cols = slice(j * FFN_CHUNK, (j + 1) * FFN_CHUNK)
        h = jnp.maximum(_dot(xb, w1_ref[:, cols]), 0.0)
        yield
        acc = acc + _dot((h * h).astype(_BF16), w2_ref[cols, :])
        yield
    o_ref[rows, :] = _layer_norm(acc, g2_ref[...], b2_ref[...])


def _run(steps, count=None):
    if count is None:
        for _ in steps:
            pass
    else:
        for _ in range(count):
            next(steps)


def _stage_jobs(pairs):
    jobs = []
    for src, dst in pairs:
        assert src.shape[0] % STAGE_ROWS == 0 and src.shape[1] % LANES == 0
        for r0 in range(0, src.shape[0], STAGE_ROWS):
            for c0 in range(0, src.shape[1], STAGE_COLS):
                jobs.append((src, dst, r0, c0, min(STAGE_COLS, src.shape[1] - c0)))
    return jobs


def _stage_steps(jobs, stage, sem):
    copies = [pltpu.make_async_copy(src.at[pl.ds(r0, STAGE_ROWS), pl.ds(c0, width)],
                                    stage.at[i % STAGE_SLOTS, :, pl.ds(0, width)],
                                    sem.at[i % STAGE_SLOTS])
              for i, (src, _, r0, c0, width) in enumerate(jobs)]
    ahead = STAGE_SLOTS - 1
    for copy in copies[:ahead]:
        copy.start()
    for i, (_, dst, r0, c0, width) in enumerate(jobs):
        if i + ahead < len(jobs):
            copies[i + ahead].start()
        copies[i].wait()
        dst[r0:r0 + STAGE_ROWS, c0:c0 + width] = stage[i % STAGE_SLOTS, :, 0:width].astype(_BF16)
        yield


def _layer_kernel(sinks_ref, x_ref, pos_ref, invf_ref, vg_ref, vb_ref, wsp_ref, bsp_ref, g1_ref,
                  b1_ref, g2_ref, b2_ref, w_in_hbm, w_out_hbm, w1_hbm, w2_hbm, o_ref,
                  y_buf, kd_buf, vt_buf, mix_buf, w_in_ref, w_out_ref, w1_ref, w2_ref, stage_sem,
                  *, alpha, n_tiles, tiles_per_seq):
    j = pl.program_id(0)
    seq_start = (j % tiles_per_seq) == 0
    tile = jnp.minimum(j, n_tiles - 1)
    tokens = x_ref.shape[0]
    pos_row = pos_ref[pl.ds(tile // tiles_per_seq, 1),
                      pl.ds(pl.multiple_of((tile % tiles_per_seq) * tokens, tokens), tokens)]

    @pl.when(seq_start)
    def _():
        kd_buf[:, 0:BLK, :] = jnp.zeros((N_KV_HEADS, BLK, LANES), _BF16)
        vt_buf[:, 0:BLK] = jnp.zeros((D_KV, BLK), _BF16)

    mixer = functools.partial(
        _mixer_steps, seq_start, sinks_ref, x_ref, pos_row, invf_ref, w_in_ref, vg_ref, vb_ref,
        wsp_ref, bsp_ref, w_out_ref, y_buf, kd_buf, vt_buf, mix_buf, alpha=alpha)
    half = x_ref.shape[0] // 2
    ffn = [functools.partial(_ffn_steps, slice(i * half, (i + 1) * half), y_buf, g1_ref, b1_ref,
                             w1_ref, w2_ref, g2_ref, b2_ref, o_ref, alpha=alpha) for i in range(2)]

    @pl.when(j == 0)
    def _():
        def first_step(stage):
            _run(_stage_steps(_stage_jobs([(w_in_hbm, w_in_ref)]), stage, stage_sem))
            later = _stage_steps(
                _stage_jobs([(w_out_hbm, w_out_ref), (w1_hbm, w1_ref), (w2_hbm, w2_ref)]),
                stage, stage_sem)
            assert len(_stage_jobs([(w_out_hbm, w_out_ref)])) <= STAGE_SLOTS - 1
            for i, _ in enumerate(mixer()):
                if i % STAGE_POINT_EVERY == STAGE_POINT_EVERY - 1:
                    for _ in range(STAGE_SLOTS - 1):
                        next(later, None)
            _run(later)

        pl.run_scoped(first_step, pltpu.VMEM((STAGE_SLOTS, STAGE_ROWS, STAGE_COLS), _F32))

    @pl.when(j > 0)
    def _():
        fa, fb, m = ffn[0](), ffn[1](), mixer()
        n_blk = x_ref.shape[0] // BLK
        n_proj = w_in_ref.shape[1] // MXU_WIDTH
        ffn_pieces = 2 * (w1_ref.shape[1] // FFN_CHUNK)
        mixer_pieces = [1] * (n_proj + n_blk - 3) + [2] * (n_blk + 1) + [1, 0, 0]
        assert len(mixer_pieces) == 2 * ffn_pieces
        _run(fa, 1)
        for count in mixer_pieces[:ffn_pieces]:
            _run(m, count)
            _run(fa, 1)
        _run(fa)
        _run(fb, 1)
        for count in mixer_pieces[ffn_pieces:]:
            _run(m, count)
            _run(fb, 1)
        _run(m)
        _run(fb)


def _resident(shape):
    return pl.BlockSpec(shape, lambda *_: (0,) * len(shape), pipeline_mode=pl.Buffered(1))


def _layer(x, pos, invf, sinks, w_in, vg, vb, wsp, bsp, w_out, g1, b1, w1, w2, g2, b2, *, alpha,
           seq):
    n_tok, d_model = x.shape
    tokens = TILE_TOKENS
    assert seq % tokens == 0 and tokens % BLK == 0 and w1.shape[1] % FFN_CHUNK == 0
    n_tiles = n_tok // tokens
    kernel = functools.partial(_layer_kernel, alpha=alpha, n_tiles=n_tiles,
                               tiles_per_seq=seq // tokens)
    in_tile = pl.BlockSpec((tokens, d_model), lambda j: (jnp.minimum(j, n_tiles - 1), 0))
    out_tile = pl.BlockSpec((tokens, d_model), lambda j: (jnp.maximum(j - 1, 0), 0))
    resident = [pos, invf, vg, vb, wsp, bsp, g1, b1, g2, b2]
    weights = [w_in, w_out, w1, w2]
    return pl.pallas_call(
        kernel,
        grid=(n_tiles + 1,),
        in_specs=[pl.BlockSpec(memory_space=pltpu.SMEM), in_tile]
        + [_resident(a.shape) for a in resident]
        + [pl.BlockSpec(memory_space=pl.ANY) for _ in weights],
        out_specs=out_tile,
        out_shape=jax.ShapeDtypeStruct(x.shape, x.dtype),
        scratch_shapes=[
            pltpu.VMEM((tokens, d_model), _F32),
            pltpu.VMEM((N_KV_HEADS, tokens + BLK, LANES), _BF16),
            pltpu.VMEM((D_KV, tokens + BLK), _BF16),
            pltpu.VMEM((tokens, D_GMLP + D_ATTN), _BF16),
        ] + [pltpu.VMEM(w.shape, _BF16) for w in weights] + [pltpu.SemaphoreType.DMA((STAGE_SLOTS,))],
        compiler_params=pltpu.CompilerParams(
            dimension_semantics=("arbitrary",), vmem_limit_bytes=VMEM_LIMIT_BYTES),
        name="layer",
    )(sinks, x, *resident, *weights)


def kernel(x, positions, w_in, v_ln_g, v_ln_b, w_spatial, b_spatial, sinks, w_out, ln1_g, ln1_b,
           w_ff1, w_ff2, ln2_g, ln2_b):
    batch, seq, d_model = x.shape
    depth = w_in.shape[0]
    alpha = (2.0 * depth) ** 0.25
    inv_freq = ROPE_THETA ** (-jnp.arange(0, HEAD_DIM, 2, dtype=_F32) / HEAD_DIM)
    invf = jnp.tile(inv_freq, LANES // HALF)[None, :]
    pos = positions
    x = x.reshape(batch * seq, d_model)
    row = lambda a: a[None, :]
    for l in range(depth):
        x = _layer(x, pos, invf, sinks[l], w_in[l], row(v_ln_g[l]), row(v_ln_b[l]),
                   w_spatial[l], b_spatial[l], w_out[l], row(ln1_g[l]), row(ln1_b[l]),
                   w_ff1[l], w_ff2[l], row(ln2_g[l]), row(ln2_b[l]), alpha=alpha, seq=seq)
    return x.reshape(batch, seq, d_model)
```

```python
import functools

import jax
import jax.numpy as jnp
from jax import lax
from jax.experimental import pallas as pl
from jax.experimental.pallas import tpu as pltpu

HEAD_DIM = 64
N_GMLP_HEADS = 8
D_GMLP = N_GMLP_HEADS * HEAD_DIM
N_Q_HEADS = 8
N_KV_HEADS = 2
GQA_GROUP = N_Q_HEADS // N_KV_HEADS
D_ATTN = N_Q_HEADS * HEAD_DIM
D_KV = N_KV_HEADS * HEAD_DIM
BLK = 128
ROPE_THETA = 10000.0
LN_EPS = 1e-5
NEG_INF = -1e30
LANES = 128
HALF = HEAD_DIM // 2
MXU_WIDTH = 256

TILE_TOKENS = 512
FFN_CHUNK = 1024
STAGE_ROWS, STAGE_COLS = 256, 1024
STAGE_SLOTS = 6
STAGE_POINT_EVERY = 3
VMEM_LIMIT_BYTES = 56 * 1024 * 1024

_BF16 = jnp.bfloat16
_F32 = jnp.float32


def _dot(a, b):
    return jnp.dot(a, b, preferred_element_type=_F32)


def _dot_nt(a, b):
    return lax.dot_general(a, b, (((1,), (1,)), ((), ())), preferred_element_type=_F32)


def _layer_norm(v, g, b):
    mu = jnp.mean(v, axis=-1, keepdims=True)
    c = v - mu
    var = jnp.mean(c * c, axis=-1, keepdims=True)
    return c * lax.rsqrt(var + LN_EPS) * g + b


def _mixer_steps(seq_start, sinks_ref, x_ref, pos_row, invf_ref, w_in_ref, vg_ref, vb_ref, wsp_ref,
                 bsp_ref, w_out_ref, y_buf, kd_buf, vt_buf, mix_buf, *, alpha):
    tokens = x_ref.shape[0]
    n_blk = tokens // BLK
    n_slab_g = D_GMLP // LANES
    n_slab_a = D_ATTN // LANES
    q0 = 2 * D_GMLP
    k0 = q0 + D_ATTN

    x = x_ref[...]
    xb = x.astype(_BF16)
    lane = lax.broadcasted_iota(jnp.int32, (1, LANES), 1)
    lo = lane < HEAD_DIM
    ti = lax.broadcasted_iota(jnp.int32, (BLK, BLK), 0)
    si = lax.broadcasted_iota(jnp.int32, (BLK, BLK), 1)

    def column(row):
        return jnp.sum(jnp.where(si == ti, row, 0.0), axis=1, keepdims=True)

    def project(col0, width):
        parts = []
        for c in range(col0, col0 + width, MXU_WIDTH):
            parts.append(_dot(xb, w_in_ref[:, c:c + MXU_WIDTH]))
            yield
        return parts

    qkv = yield from project(q0, D_ATTN + 2 * D_KV)
    slabs = [part[:, i * LANES:(i + 1) * LANES] for part in qkv for i in range(MXU_WIDTH // LANES)]
    q_raw, k_raw, v = slabs[:n_slab_a], slabs[n_slab_a], slabs[n_slab_a + 1]

    pos = pos_row.astype(_F32)
    invf = invf_ref[...]
    n_grp = LANES // HALF
    quarter = tokens // n_grp
    assert pos.shape == (1, tokens) and quarter == BLK
    grp = lane // HALF
    ang = None
    for a in range(n_grp):
        term = (column(pos[:, a * quarter:(a + 1) * quarter])
                * jnp.where(grp == a, invf, 0.0))
        ang = term if ang is None else ang + term
    cos_packed, sin_packed = jnp.cos(ang), jnp.sin(ang)

    def spread(packed):
        parts = []
        for a in range(n_grp):
            one = jnp.where(grp == a, packed, 0.0)
            two = one + pltpu.roll(one, 2 * HALF, axis=1)
            parts.append(two + pltpu.roll(two, HALF, axis=1))
        return jnp.concatenate(parts, axis=0)

    cos = spread(cos_packed)
    first_half = (lane & HALF) == 0
    sin_signed = jnp.where(first_half, -1.0, 1.0) * spread(sin_packed)

    def rope(t):
        swapped = jnp.where(first_half, pltpu.roll(t, LANES - HALF, axis=1),
                            pltpu.roll(t, HALF, axis=1))
        return t * cos + swapped * sin_signed

    scale = HEAD_DIM ** -0.5
    q_even, q_odd = [], []
    for p in range(n_slab_a):
        qs = rope(q_raw[p]) * scale
        q_even.append(jnp.where(lo, qs, 0.0).astype(_BF16))
        q_odd.append(jnp.where(lo, 0.0, qs).astype(_BF16))
    kr = rope(k_raw)
    kr_sw = pltpu.roll(kr, HEAD_DIM, axis=1)
    cur = slice(BLK, BLK + tokens)
    kd_buf[0, cur, :] = jnp.where(lo, kr, kr_sw).astype(_BF16)
    kd_buf[1, cur, :] = jnp.where(lo, kr_sw, kr).astype(_BF16)
    vt_buf[:, cur] = v.T.astype(_BF16)

    u = jax.nn.gelu(jnp.concatenate((yield from project(0, D_GMLP)), axis=1))
    vgel = jax.nn.gelu(jnp.concatenate((yield from project(D_GMLP, D_GMLP)), axis=1))
    vn = _layer_norm(vgel, vg_ref[...], vb_ref[...])
    lo_g = (lax.broadcasted_iota(jnp.int32, (1, D_GMLP), 1) & HEAD_DIM) == 0
    vn_top = jnp.where(lo_g, vn, 0.0).astype(_BF16)
    vn_bot = jnp.where(lo_g, 0.0, vn).astype(_BF16)

    from_prev = ti > si
    no_prev = from_prev & (ti < jnp.where(seq_start, BLK, 0))
    probs = {}
    for n in range(n_blk):
        rows = slice(n * BLK, (n + 1) * BLK)
        kv_rows = slice(n * BLK, (n + 2) * BLK)
        for g in range(N_KV_HEADS):
            heads = range(GQA_GROUP * g, GQA_GROUP * (g + 1))
            q_stack = jnp.concatenate(
                [(q_odd if h % 2 else q_even)[h // 2][rows] for h in heads], axis=0)
            scores = _dot_nt(kd_buf[g, kv_rows, :], q_stack)
            p_parts, inv_parts = [], []
            for hl, h in enumerate(heads):
                cols = slice(hl * BLK, (hl + 1) * BLK)
                sh = jnp.where(from_prev, scores[0:BLK, cols], scores[BLK:2 * BLK, cols])
                if n == 0:
                    sh = jnp.where(no_prev, NEG_INF, sh)
                sink = sinks_ref[h]
                m = jnp.maximum(jnp.max(sh, axis=0, keepdims=True), sink)
                pr = jnp.exp(sh - m)
                l = jnp.sum(pr, axis=0, keepdims=True) + jnp.exp(sink - m)
                p_parts.append(jnp.concatenate(
                    [jnp.where(from_prev, pr, 0.0), jnp.where(from_prev, 0.0, pr)], axis=0))
                inv_parts.append(1.0 / l)
            probs[n, g] = (jnp.concatenate(p_parts, axis=1).astype(_BF16),
                           jnp.concatenate(inv_parts, axis=1))
        yield

    causal = si <= ti

    def gmlp_piece(p):
        w_pair = jnp.concatenate(
            [jnp.where(causal, wsp_ref[2 * p], 0.0), jnp.where(causal, wsp_ref[2 * p + 1], 0.0)],
            axis=1).astype(_BF16)
        bias = jnp.where(lo, column(bsp_ref[2 * p:2 * p + 1, :]),
                         column(bsp_ref[2 * p + 1:2 * p + 2, :]))
        for c in range(n_blk):
            rows = slice(c * BLK, (c + 1) * BLK)
            cols = slice(p * LANES, (p + 1) * LANES)
            rhs = jnp.concatenate([vn_top[rows, cols], vn_bot[rows, cols]], axis=0)
            mixed = _dot(w_pair, rhs) + bias
            mix_buf[rows, cols] = (u[rows, cols] * mixed).astype(_BF16)

    def values_piece(n):
        rows = slice(n * BLK, (n + 1) * BLK)
        kv_cols = slice(n * BLK, (n + 2) * BLK)
        for g in range(N_KV_HEADS):
            p_all, inv_all = probs[n, g]
            v_t = vt_buf[g * HEAD_DIM:(g + 1) * HEAD_DIM, kv_cols]
            out_t = _dot(v_t, p_all) * inv_all
            for i in range(GQA_GROUP // 2):
                slab = g * (GQA_GROUP // 2) + i
                pair = jnp.concatenate([out_t[:, (2 * i) * BLK:(2 * i + 1) * BLK],
                                        out_t[:, (2 * i + 1) * BLK:(2 * i + 2) * BLK]], axis=0)
                mix_buf[rows, D_GMLP + slab * LANES:D_GMLP + (slab + 1) * LANES] = (
                    pair.T.astype(_BF16))

    for i in range(max(n_slab_g, n_blk)):
        if i < n_slab_g:
            gmlp_piece(i)
            yield
        if i < n_blk:
            values_piece(i)
            yield
    last = slice(tokens, tokens + BLK)
    kd_buf[:, 0:BLK, :] = kd_buf[:, last, :]
    vt_buf[:, 0:BLK] = vt_buf[:, last]

    mix = mix_buf[...]
    for c in range(0, x.shape[1], MXU_WIDTH):
        y_buf[:, c:c + MXU_WIDTH] = alpha * x[:, c:c + MXU_WIDTH] + _dot(mix, w_out_ref[:, c:c + MXU_WIDTH])
        yield


def _ffn_steps(rows, y_buf, g1_ref, b1_ref, w1_ref, w2_ref, g2_ref, b2_ref, o_ref, *, alpha):
    x = _layer_norm(y_buf[rows, :], g1_ref[...], b1_ref[...])
    xb = x.astype(_BF16)
    acc = alpha * x
    yield
    for j in range(w1_ref.shape[1] // FFN_CHUNK):
        cols = slice(j * FFN_CHUNK, (j + 1) * FFN_CHUNK)
        h = jnp.maximum(_dot(xb, w1_ref[:, cols]), 0.0)
        yield
        acc = acc + _dot((h * h).astype(_BF16), w2_ref[cols, :])
        yield
    o_ref[rows, :] = _layer_norm(acc, g2_ref[...], b2_ref[...])


def _run(steps, count=None):
    if count is None:
        for _ in steps:
            pass
    else:
        for _ in range(count):
            next(steps)


def _stage_jobs(pairs):
    jobs = []
    for src, dst in pairs:
        assert src.shape[0] % STAGE_ROWS == 0 and src.shape[1] % LANES == 0
        for r0 in range(0, src.shape[0], STAGE_ROWS):
            for c0 in range(0, src.shape[1], STAGE_COLS):
                jobs.append((src, dst, r0, c0, min(STAGE_COLS, src.shape[1] - c0)))
    return jobs


def _stage_steps(jobs, stage, sem):
    copies = [pltpu.make_async_copy(src.at[pl.ds(r0, STAGE_ROWS), pl.ds(c0, width)],
                                    stage.at[i % STAGE_SLOTS, :, pl.ds(0, width)],
                                    sem.at[i % STAGE_SLOTS])
              for i, (src, _, r0, c0, width) in enumerate(jobs)]
    ahead = STAGE_SLOTS - 1
    for copy in copies[:ahead]:
        copy.start()
    for i, (_, dst, r0, c0, width) in enumerate(jobs):
        if i + ahead < len(jobs):
            copies[i + ahead].start()
        copies[i].wait()
        dst[r0:r0 + STAGE_ROWS, c0:c0 + width] = stage[i % STAGE_SLOTS, :, 0:width].astype(_BF16)
        yield


def _layer_kernel(sinks_ref, x_ref, pos_ref, invf_ref, vg_ref, vb_ref, wsp_ref, bsp_ref, g1_ref,
                  b1_ref, g2_ref, b2_ref, w_in_hbm, w_out_hbm, w1_hbm, w2_hbm, o_ref,
                  y_buf, kd_buf, vt_buf, mix_buf, w_in_ref, w_out_ref, w1_ref, w2_ref, stage_sem,
                  *, alpha, n_tiles, tiles_per_seq):
    j = pl.program_id(0)
    seq_start = (j % tiles_per_seq) == 0
    tile = jnp.minimum(j, n_tiles - 1)
    tokens = x_ref.shape[0]
    pos_row = pos_ref[pl.ds(tile // tiles_per_seq, 1),
                      pl.ds(pl.multiple_of((tile % tiles_per_seq) * tokens, tokens), tokens)]

    @pl.when(seq_start)
    def _():
        kd_buf[:, 0:BLK, :] = jnp.zeros((N_KV_HEADS, BLK, LANES), _BF16)
        vt_buf[:, 0:BLK] = jnp.zeros((D_KV, BLK), _BF16)

    mixer = functools.partial(
        _mixer_steps, seq_start, sinks_ref, x_ref, pos_row, invf_ref, w_in_ref, vg_ref, vb_ref,
        wsp_ref, bsp_ref, w_out_ref, y_buf, kd_buf, vt_buf, mix_buf, alpha=alpha)
    half = x_ref.shape[0] // 2
    ffn = [functools.partial(_ffn_steps, slice(i * half, (i + 1) * half), y_buf, g1_ref, b1_ref,
                             w1_ref, w2_ref, g2_ref, b2_ref, o_ref, alpha=alpha) for i in range(2)]

    @pl.when(j == 0)
    def _():
        def first_step(stage):
            _run(_stage_steps(_stage_jobs([(w_in_hbm, w_in_ref)]), stage, stage_sem))
            later = _stage_steps(
                _stage_jobs([(w_out_hbm, w_out_ref), (w1_hbm, w1_ref), (w2_hbm, w2_ref)]),
                stage, stage_sem)
            assert len(_stage_jobs([(w_out_hbm, w_out_ref)])) <= STAGE_SLOTS - 1
            for i, _ in enumerate(mixer()):
                if i % STAGE_POINT_EVERY == STAGE_POINT_EVERY - 1:
                    for _ in range(STAGE_SLOTS - 1):
                        next(later, None)
            _run(later)

        pl.run_scoped(first_step, pltpu.VMEM((STAGE_SLOTS, STAGE_ROWS, STAGE_COLS), _F32))

    @pl.when(j > 0)
    def _():
        fa, fb, m = ffn[0](), ffn[1](), mixer()
        n_blk = x_ref.shape[0] // BLK
        n_proj = w_in_ref.shape[1] // MXU_WIDTH
        ffn_pieces = 2 * (w1_ref.shape[1] // FFN_CHUNK)
        mixer_pieces = [1] * (n_proj + n_blk - 1) + [2] * n_blk + [0, 1]
        assert len(mixer_pieces) == 2 * ffn_pieces
        _run(fa, 1)
        for count in mixer_pieces[:ffn_pieces]:
            _run(m, count)
            _run(fa, 1)
        _run(fa)
        _run(fb, 1)
        for count in mixer_pieces[ffn_pieces:]:
            _run(m, count)
            _run(fb, 1)
        _run(m)
        _run(fb)


def _resident(shape):
    return pl.BlockSpec(shape, lambda *_: (0,) * len(shape), pipeline_mode=pl.Buffered(1))


def _layer(x, pos, invf, sinks, w_in, vg, vb, wsp, bsp, w_out, g1, b1, w1, w2, g2, b2, *, alpha,
           seq):
    n_tok, d_model = x.shape
    tokens = TILE_TOKENS
    assert seq % tokens == 0 and tokens % BLK == 0 and w1.shape[1] % FFN_CHUNK == 0
    n_tiles = n_tok // tokens
    kernel = functools.partial(_layer_kernel, alpha=alpha, n_tiles=n_tiles,
                               tiles_per_seq=seq // tokens)
    in_tile = pl.BlockSpec((tokens, d_model), lambda j: (jnp.minimum(j, n_tiles - 1), 0))
    out_tile = pl.BlockSpec((tokens, d_model), lambda j: (jnp.maximum(j - 1, 0), 0))
    resident = [pos, invf, vg, vb, wsp, bsp, g1, b1, g2, b2]
    weights = [w_in, w_out, w1, w2]
    return pl.pallas_call(
        kernel,
        grid=(n_tiles + 1,),
        in_specs=[pl.BlockSpec(memory_space=pltpu.SMEM), in_tile]
        + [_resident(a.shape) for a in resident]
        + [pl.BlockSpec(memory_space=pl.ANY) for _ in weights],
        out_specs=out_tile,
        out_shape=jax.ShapeDtypeStruct(x.shape, x.dtype),
        scratch_shapes=[
            pltpu.VMEM((tokens, d_model), _F32),
            pltpu.VMEM((N_KV_HEADS, tokens + BLK, LANES), _BF16),
            pltpu.VMEM((D_KV, tokens + BLK), _BF16),
            pltpu.VMEM((tokens, D_GMLP + D_ATTN), _BF16),
        ] + [pltpu.VMEM(w.shape, _BF16) for w in weights] + [pltpu.SemaphoreType.DMA((STAGE_SLOTS,))],
        compiler_params=pltpu.CompilerParams(
            dimension_semantics=("arbitrary",), vmem_limit_bytes=VMEM_LIMIT_BYTES),
        name="layer",
    )(sinks, x, *resident, *weights)


def kernel(x, positions, w_in, v_ln_g, v_ln_b, w_spatial, b_spatial, sinks, w_out, ln1_g, ln1_b,
           w_ff1, w_ff2, ln2_g, ln2_b):
    batch, seq, d_model = x.shape
    depth = w_in.shape[0]
    alpha = (2.0 * depth) ** 0.25
    inv_freq = ROPE_THETA ** (-jnp.arange(0, HEAD_DIM, 2, dtype=_F32) / HEAD_DIM)
    invf = jnp.tile(inv_freq, LANES // HALF)[None, :]
    pos = positions
    x = x.reshape(batch * seq, d_model)
    row = lambda a: a[None, :]
    for l in range(depth):
        x = _layer(x, pos, invf, sinks[l], w_in[l], row(v_ln_g[l]), row(v_ln_b[l]),
                   w_spatial[l], b_spatial[l], w_out[l], row(ln1_g[l]), row(ln1_b[l]),
                   w_ff1[l], w_ff2[l], row(ln2_g[l]), row(ln2_b[l]), alpha=alpha, seq=seq)
    return x.reshape(batch, seq, d_model)
```

```python
import functools

import jax
import jax.numpy as jnp
from jax import lax
from jax.experimental import pallas as pl
from jax.experimental.pallas import tpu as pltpu

HEAD_DIM = 64
N_GMLP_HEADS = 8
D_GMLP = N_GMLP_HEADS * HEAD_DIM
N_Q_HEADS = 8
N_KV_HEADS = 2
GQA_GROUP = N_Q_HEADS // N_KV_HEADS
D_ATTN = N_Q_HEADS * HEAD_DIM
D_KV = N_KV_HEADS * HEAD_DIM
BLK = 128
ROPE_THETA = 10000.0
LN_EPS = 1e-5
NEG_INF = -1e30
LANES = 128
HALF = HEAD_DIM // 2
MXU_WIDTH = 256

TILE_TOKENS = 512
FFN_CHUNK = 1024
STAGE_ROWS, STAGE_COLS = 256, 1024
STAGE_SLOTS = 6
STAGE_POINT_EVERY = 3
VMEM_LIMIT_BYTES = 56 * 1024 * 1024

_BF16 = jnp.bfloat16
_F32 = jnp.float32


def _dot(a, b):
    return jnp.dot(a, b, preferred_element_type=_F32)


def _dot_nt(a, b):
    return lax.dot_general(a, b, (((1,), (1,)), ((), ())), preferred_element_type=_F32)


def _layer_norm(v, g, b):
    mu = jnp.mean(v, axis=-1, keepdims=True)
    c = v - mu
    var = jnp.mean(c * c, axis=-1, keepdims=True)
    return c * lax.rsqrt(var + LN_EPS) * g + b


def _mixer_steps(seq_start, sinks_ref, x_ref, pos_row, invf_ref, w_in_ref, vg_ref, vb_ref, wsp_ref,
                 bsp_ref, w_out_ref, y_buf, kd_buf, vt_buf, mix_buf, *, alpha):
    tokens = x_ref.shape[0]
    n_blk = tokens // BLK
    n_slab_g = D_GMLP // LANES
    n_slab_a = D_ATTN // LANES
    q0 = 2 * D_GMLP
    k0 = q0 + D_ATTN

    x = x_ref[...]
    xb = x.astype(_BF16)
    lane = lax.broadcasted_iota(jnp.int32, (1, LANES), 1)
    lo = lane < HEAD_DIM
    ti = lax.broadcasted_iota(jnp.int32, (BLK, BLK), 0)
    si = lax.broadcasted_iota(jnp.int32, (BLK, BLK), 1)

    def column(row):
        return jnp.sum(jnp.where(si == ti, row, 0.0), axis=1, keepdims=True)

    def project(col0, width):
        parts = []
        for c in range(col0, col0 + width, MXU_WIDTH):
            parts.append(_dot(xb, w_in_ref[:, c:c + MXU_WIDTH]))
            yield
        return parts

    qkv = yield from project(q0, D_ATTN + 2 * D_KV)
    slabs = [part[:, i * LANES:(i + 1) * LANES] for part in qkv for i in range(MXU_WIDTH // LANES)]
    q_raw, k_raw, v = slabs[:n_slab_a], slabs[n_slab_a], slabs[n_slab_a + 1]

    pos = pos_row.astype(_F32)
    invf = invf_ref[...]
    n_grp = LANES // HALF
    quarter = tokens // n_grp
    assert pos.shape == (1, tokens) and quarter == BLK
    grp = lane // HALF
    ang = None
    for a in range(n_grp):
        term = (column(pos[:, a * quarter:(a + 1) * quarter])
                * jnp.where(grp == a, invf, 0.0))
        ang = term if ang is None else ang + term
    cos_packed, sin_packed = jnp.cos(ang), jnp.sin(ang)

    def spread(packed):
        parts = []
        for a in range(n_grp):
            one = jnp.where(grp == a, packed, 0.0)
            two = one + pltpu.roll(one, 2 * HALF, axis=1)
            parts.append(two + pltpu.roll(two, HALF, axis=1))
        return jnp.concatenate(parts, axis=0)

    cos = spread(cos_packed)
    first_half = (lane & HALF) == 0
    sin_signed = jnp.where(first_half, -1.0, 1.0) * spread(sin_packed)

    def rope(t):
        swapped = jnp.where(first_half, pltpu.roll(t, LANES - HALF, axis=1),
                            pltpu.roll(t, HALF, axis=1))
        return t * cos + swapped * sin_signed

    scale = HEAD_DIM ** -0.5
    q_even, q_odd = [], []
    for p in range(n_slab_a):
        qs = rope(q_raw[p]) * scale
        q_even.append(jnp.where(lo, qs, 0.0).astype(_BF16))
        q_odd.append(jnp.where(lo, 0.0, qs).astype(_BF16))
    kr = rope(k_raw)
    kr_sw = pltpu.roll(kr, HEAD_DIM, axis=1)
    cur = slice(BLK, BLK + tokens)
    kd_buf[0, cur, :] = jnp.where(lo, kr, kr_sw).astype(_BF16)
    kd_buf[1, cur, :] = jnp.where(lo, kr_sw, kr).astype(_BF16)
    vt_buf[:, cur] = v.T.astype(_BF16)

    u = jax.nn.gelu(jnp.concatenate((yield from project(0, D_GMLP)), axis=1))
    vgel = jax.nn.gelu(jnp.concatenate((yield from project(D_GMLP, D_GMLP)), axis=1))
    vn = _layer_norm(vgel, vg_ref[...], vb_ref[...])
    lo_g = (lax.broadcasted_iota(jnp.int32, (1, D_GMLP), 1) & HEAD_DIM) == 0
    vn_top = jnp.where(lo_g, vn, 0.0).astype(_BF16)
    vn_bot = jnp.where(lo_g, 0.0, vn).astype(_BF16)

    from_prev = ti > si
    no_prev = from_prev & (ti < jnp.where(seq_start, BLK, 0))
    probs = {}
    for n in range(n_blk):
        rows = slice(n * BLK, (n + 1) * BLK)
        kv_rows = slice(n * BLK, (n + 2) * BLK)
        for g in range(N_KV_HEADS):
            heads = range(GQA_GROUP * g, GQA_GROUP * (g + 1))
            q_stack = jnp.concatenate(
                [(q_odd if h % 2 else q_even)[h // 2][rows] for h in heads], axis=0)
            scores = _dot_nt(kd_buf[g, kv_rows, :], q_stack)
            p_parts, inv_parts = [], []
            for hl, h in enumerate(heads):
                cols = slice(hl * BLK, (hl + 1) * BLK)
                sh = jnp.where(from_prev, scores[0:BLK, cols], scores[BLK:2 * BLK, cols])
                if n == 0:
                    sh = jnp.where(no_prev, NEG_INF, sh)
                sink = sinks_ref[h]
                m = jnp.maximum(jnp.max(sh, axis=0, keepdims=True), sink)
                pr = jnp.exp(sh - m)
                l = jnp.sum(pr, axis=0, keepdims=True) + jnp.exp(sink - m)
                p_parts.append(jnp.concatenate(
                    [jnp.where(from_prev, pr, 0.0), jnp.where(from_prev, 0.0, pr)], axis=0))
                inv_parts.append(1.0 / l)
            probs[n, g] = (jnp.concatenate(p_parts, axis=1).astype(_BF16),
                           jnp.concatenate(inv_parts, axis=1))
        yield

    causal = si <= ti

    def gmlp_piece(p):
        w_pair = jnp.concatenate(
            [jnp.where(causal, wsp_ref[2 * p], 0.0), jnp.where(causal, wsp_ref[2 * p + 1], 0.0)],
            axis=1).astype(_BF16)
        bias = jnp.where(lo, column(bsp_ref[2 * p:2 * p + 1, :]),
                         column(bsp_ref[2 * p + 1:2 * p + 2, :]))
        for c in range(n_blk):
            rows = slice(c * BLK, (c + 1) * BLK)
            cols = slice(p * LANES, (p + 1) * LANES)
            rhs = jnp.concatenate([vn_top[rows, cols], vn_bot[rows, cols]], axis=0)
            mixed = _dot(w_pair, rhs) + bias
            mix_buf[rows, cols] = (u[rows, cols] * mixed).astype(_BF16)

    def values_piece(n):
        rows = slice(n * BLK, (n + 1) * BLK)
        kv_cols = slice(n * BLK, (n + 2) * BLK)
        for g in range(N_KV_HEADS):
            p_all, inv_all = probs[n, g]
            v_t = vt_buf[g * HEAD_DIM:(g + 1) * HEAD_DIM, kv_cols]
            out_t = _dot(v_t, p_all) * inv_all
            for i in range(GQA_GROUP // 2):
                slab = g * (GQA_GROUP // 2) + i
                pair = jnp.concatenate([out_t[:, (2 * i) * BLK:(2 * i + 1) * BLK],
                                        out_t[:, (2 * i + 1) * BLK:(2 * i + 2) * BLK]], axis=0)
                mix_buf[rows, D_GMLP + slab * LANES:D_GMLP + (slab + 1) * LANES] = (
                    pair.T.astype(_BF16))

    for i in range(max(n_slab_g, n_blk)):
        if i < n_slab_g:
            gmlp_piece(i)
            yield
        if i < n_blk:
            values_piece(i)
            yield
    last = slice(tokens, tokens + BLK)
    kd_buf[:, 0:BLK, :] = kd_buf[:, last, :]
    vt_buf[:, 0:BLK] = vt_buf[:, last]

    mix = mix_buf[...]
    for c in range(0, x.shape[1], MXU_WIDTH):
        y_buf[:, c:c + MXU_WIDTH] = alpha * x[:, c:c + MXU_WIDTH] + _dot(mix, w_out_ref[:, c:c + MXU_WIDTH])
        yield


def _ffn_steps(rows, y_buf, g1_ref, b1_ref, w1_ref, w2_ref, g2_ref, b2_ref, o_ref, *, alpha):
    x = _layer_norm(y_buf[rows, :], g1_ref[...], b1_ref[...])
    xb = x.astype(_BF16)
    acc = alpha * x
    yield
    for j in range(w1_ref.shape[1] // FFN_CHUNK):
        cols = slice(j * FFN_CHUNK, (j + 1) * FFN_CHUNK)
        h = jnp.maximum(_dot(xb, w1_ref[:, cols]), 0.0)
        yield
        acc = acc + _dot((h * h).astype(_BF16), w2_ref[cols, :])
        yield
    o_ref[rows, :] = _layer_norm(acc, g2_ref[...], b2_ref[...])


def _run(steps, count=None):
    if count is None:
        for _ in steps:
            pass
    else:
        for _ in range(count):
            next(steps)


def _stage_jobs(pairs):
    jobs = []
    for src, dst in pairs:
        assert src.shape[0] % STAGE_ROWS == 0 and src.shape[1] % LANES == 0
        for r0 in range(0, src.shape[0], STAGE_ROWS):
            for c0 in range(0, src.shape[1], STAGE_COLS):
                jobs.append((src, dst, r0, c0, min(STAGE_COLS, src.shape[1] - c0)))
    return jobs


def _stage_steps(jobs, stage, sem):
    copies = [pltpu.make_async_copy(src.at[pl.ds(r0, STAGE_ROWS), pl.ds(c0, width)],
                                    stage.at[i % STAGE_SLOTS, :, pl.ds(0, width)],
                                    sem.at[i % STAGE_SLOTS])
              for i, (src, _, r0, c0, width) in enumerate(jobs)]
    ahead = STAGE_SLOTS - 1
    for copy in copies[:ahead]:
        copy.start()
    for i, (_, dst, r0, c0, width) in enumerate(jobs):
        if i + ahead < len(jobs):
            copies[i + ahead].start()
        copies[i].wait()
        dst[r0:r0 + STAGE_ROWS, c0:c0 + width] = stage[i % STAGE_SLOTS, :, 0:width].astype(_BF16)
        yield


def _layer_kernel(sinks_ref, x_ref, pos_ref, invf_ref, vg_ref, vb_ref, wsp_ref, bsp_ref, g1_ref,
                  b1_ref, g2_ref, b2_ref, w_in_hbm, w_out_hbm, w1_hbm, w2_hbm, o_ref,
                  y_buf, kd_buf, vt_buf, mix_buf, w_in_ref, w_out_ref, w1_ref, w2_ref, stage_sem,
                  *, alpha, n_tiles, tiles_per_seq):
    j = pl.program_id(0)
    seq_start = (j % tiles_per_seq) == 0
    tile = jnp.minimum(j, n_tiles - 1)
    tokens = x_ref.shape[0]
    pos_row = pos_ref[pl.ds(tile // tiles_per_seq, 1),
                      pl.ds(pl.multiple_of((tile % tiles_per_seq) * tokens, tokens), tokens)]

    @pl.when(seq_start)
    def _():
        kd_buf[:, 0:BLK, :] = jnp.zeros((N_KV_HEADS, BLK, LANES), _BF16)
        vt_buf[:, 0:BLK] = jnp.zeros((D_KV, BLK), _BF16)

    mixer = functools.partial(
        _mixer_steps, seq_start, sinks_ref, x_ref, pos_row, invf_ref, w_in_ref, vg_ref, vb_ref,
        wsp_ref, bsp_ref, w_out_ref, y_buf, kd_buf, vt_buf, mix_buf, alpha=alpha)
    half = x_ref.shape[0] // 2
    ffn = [functools.partial(_ffn_steps, slice(i * half, (i + 1) * half), y_buf, g1_ref, b1_ref,
                             w1_ref, w2_ref, g2_ref, b2_ref, o_ref, alpha=alpha) for i in range(2)]

    @pl.when(j == 0)
    def _():
        def first_step(stage):
            _run(_stage_steps(_stage_jobs([(w_in_hbm, w_in_ref)]), stage, stage_sem))
            later = _stage_steps(
                _stage_jobs([(w_out_hbm, w_out_ref), (w1_hbm, w1_ref), (w2_hbm, w2_ref)]),
                stage, stage_sem)
            assert len(_stage_jobs([(w_out_hbm, w_out_ref)])) <= STAGE_SLOTS - 1
            for i, _ in enumerate(mixer()):
                if i % STAGE_POINT_EVERY == STAGE_POINT_EVERY - 1:
                    for _ in range(STAGE_SLOTS - 1):
                        next(later, None)
            _run(later)

        pl.run_scoped(first_step, pltpu.VMEM((STAGE_SLOTS, STAGE_ROWS, STAGE_COLS), _F32))

    @pl.when(j > 0)
    def _():
        fa, fb, m = ffn[0](), ffn[1](), mixer()
        n_blk = x_ref.shape[0] // BLK
        n_proj = w_in_ref.shape[1] // MXU_WIDTH
        ffn_pieces = 2 * (w1_ref.shape[1] // FFN_CHUNK)
        mixer_pieces = [1] * (n_proj + n_blk - 1) + [2] * n_blk + [1, 1]
        assert len(mixer_pieces) == 2 * ffn_pieces
        _run(fa, 1)
        for count in mixer_pieces[:ffn_pieces]:
            _run(m, count)
            _run(fa, 1)
        _run(fa)
        _run(fb, 1)
        for count in mixer_pieces[ffn_pieces:]:
            _run(m, count)
            _run(fb, 1)
        _run(m)
        _run(fb)


def _resident(shape):
    return pl.BlockSpec(shape, lambda *_: (0,) * len(shape), pipeline_mode=pl.Buffered(1))


def _layer(x, pos, invf, sinks, w_in, vg, vb, wsp, bsp, w_out, g1, b1, w1, w2, g2, b2, *, alpha,
           seq):
    n_tok, d_model = x.shape
    tokens = TILE_TOKENS
    assert seq % tokens == 0 and tokens % BLK == 0 and w1.shape[1] % FFN_CHUNK == 0
    n_tiles = n_tok // tokens
    kernel = functools.partial(_layer_kernel, alpha=alpha, n_tiles=n_tiles,
                               tiles_per_seq=seq // tokens)
    in_tile = pl.BlockSpec((tokens, d_model), lambda j: (jnp.minimum(j, n_tiles - 1), 0))
    out_tile = pl.BlockSpec((tokens, d_model), lambda j: (jnp.maximum(j - 1, 0), 0))
    resident = [pos, invf, vg, vb, wsp, bsp, g1, b1, g2, b2]
    weights = [w_in, w_out, w1, w2]
    return pl.pallas_call(
        kernel,
        grid=(n_tiles + 1,),
        in_specs=[pl.BlockSpec(memory_space=pltpu.SMEM), in_tile]
        + [_resident(a.shape) for a in resident]
        + [pl.BlockSpec(memory_space=pl.ANY) for _ in weights],
        out_specs=out_tile,
        out_shape=jax.ShapeDtypeStruct(x.shape, x.dtype),
        scratch_shapes=[
            pltpu.VMEM((tokens, d_model), _F32),
            pltpu.VMEM((N_KV_HEADS, tokens + BLK, LANES), _BF16),
            pltpu.VMEM((D_KV, tokens + BLK), _BF16),
            pltpu.VMEM((tokens, D_GMLP + D_ATTN), _BF16),
        ] + [pltpu.VMEM(w.shape, _BF16) for w in weights] + [pltpu.SemaphoreType.DMA((STAGE_SLOTS,))],
        compiler_params=pltpu.CompilerParams(
            dimension_semantics=("arbitrary",), vmem_limit_bytes=VMEM_LIMIT_BYTES),
        name="layer",
    )(sinks, x, *resident, *weights)


def kernel(x, positions, w_in, v_ln_g, v_ln_b, w_spatial, b_spatial, sinks, w_out, ln1_g, ln1_b,
           w_ff1, w_ff2, ln2_g, ln2_b):
    batch, seq, d_model = x.shape
    depth = w_in.shape[0]
    alpha = (2.0 * depth) ** 0.25
    inv_freq = ROPE_THETA ** (-jnp.arange(0, HEAD_DIM, 2, dtype=_F32) / HEAD_DIM)
    invf = jnp.tile(inv_freq, LANES // HALF)[None, :]
    pos = positions
    x = x.reshape(batch * seq, d_model)
    row = lambda a: a[None, :]
    for l in range(depth):
        x = _layer(x, pos, invf, sinks[l], w_in[l], row(v_ln_g[l]), row(v_ln_b[l]),
                   w_spatial[l], b_spatial[l], w_out[l], row(ln1_g[l]), row(ln1_b[l]),
                   w_ff1[l], w_ff2[l], row(ln2_g[l]), row(ln2_b[l]), alpha=alpha, seq=seq)
    return x.reshape(batch, seq, d_model)
```
